```python
import math
import jax, jax.numpy as jnp
from jax import lax
import numpy as np

D_MODEL = 1024
BATCH = 2
SEQ = 8192
DEPTH = 1
DEC_BATCH = 16
DEC_SEQ = 64
PAST_LEN = 4096

CHUNK = 64
HEAD_DIM = 64
SB_WIDTH = D_MODEL // 2
SB_HEADS = SB_WIDTH // HEAD_DIM
POOL_WIDTH = D_MODEL // 4
POOL_WINDOWS = (2, 4, 8, 16)
POOL_GROUPS = len(POOL_WINDOWS)
POOL_GROUP_DIM = POOL_WIDTH // POOL_GROUPS
POOL_STATE = max(POOL_WINDOWS) - 1
XA_WIDTH = D_MODEL // 4
XA_HEADS = 4
XA_HEAD_DIM = XA_WIDTH // XA_HEADS
N_MEM = 256
MIX_WIDTH = SB_WIDTH + POOL_WIDTH + XA_WIDTH
IN_SIZES = (SB_WIDTH, SB_WIDTH, SB_WIDTH, SB_WIDTH, POOL_WIDTH, POOL_WIDTH, XA_WIDTH, XA_WIDTH)
IN_WIDTH = sum(IN_SIZES)
IN_SPLITS = tuple(int(i) for i in np.cumsum(IN_SIZES)[:-1])
Q_BLOCK = 128
EPS = 1e-6

kernel_name = "stick_breaking_pool_memory_hybrid_step"


def rms_norm(x, g):
    xf = x.astype(jnp.float32)
    y = xf * lax.rsqrt(jnp.mean(xf * xf, axis=-1, keepdims=True) + EPS)
    return (y * g.astype(jnp.float32)).astype(x.dtype)


def _sb_block(q_blk, q_pos, k, v, k_pos):
    z = jnp.einsum('bqhd,bkhd->bhqk', q_blk, k).astype(jnp.float32) / math.sqrt(HEAD_DIM)
    causal = k_pos[None, :] < q_pos[:, None]
    log_beta = jax.nn.log_sigmoid(z)
    log_rest = jnp.where(causal, jax.nn.log_sigmoid(-z), 0.0)
    tail = lax.cumsum(log_rest, axis=3, reverse=True) - log_rest
    w = jnp.where(causal, jnp.exp(log_beta + tail), 0.0)
    return jnp.einsum('bhqk,bkhd->bqhd', w.astype(v.dtype), v)


def stick_breaking(q, k, v, q_pos, k_pos):
    B, T, H, Dh = q.shape
    if T <= Q_BLOCK:
        return _sb_block(q, q_pos, k, v, k_pos)
    nb = T // Q_BLOCK
    qb = q.reshape(B, nb, Q_BLOCK, H, Dh).transpose(1, 0, 2, 3, 4)
    pb = q_pos.reshape(nb, Q_BLOCK)
    ob = lax.map(lambda a: _sb_block(a[0], a[1], k, v, k_pos), (qb, pb))
    return ob.transpose(1, 0, 2, 3, 4).reshape(B, T, H, Dh)


def multiscale_pool(u, hist, start):
    B, T, C = u.shape
    up = jnp.concatenate([hist, u], axis=1).astype(jnp.float32)
    cs = jnp.concatenate([jnp.zeros((B, 1, C), jnp.float32), jnp.cumsum(up, axis=1)], axis=1)
    pos = start + jnp.arange(T)
    hi = cs[:, POOL_STATE + 1:POOL_STATE + 1 + T]
    means = []
    for g, w in enumerate(POOL_WINDOWS):
        sl = slice(g * POOL_GROUP_DIM, (g + 1) * POOL_GROUP_DIM)
        lo = cs[:, POOL_STATE + 1 - w:POOL_STATE + 1 - w + T, sl]
        cnt = jnp.minimum(pos + 1, w).astype(jnp.float32)
        means.append((hi[..., sl] - lo) / cnt[None, :, None])
    mean = jnp.concatenate(means, axis=-1)
    return (mean - u.astype(jnp.float32)).astype(u.dtype)


def memory_kv(mem, g_mem, w_mem_kv):
    B, N, _ = mem.shape
    kv = rms_norm(mem, g_mem) @ w_mem_kv
    mk, mv = jnp.split(kv, 2, axis=-1)
    return (mk.reshape(B, N, XA_HEADS, XA_HEAD_DIM), mv.reshape(B, N, XA_HEADS, XA_HEAD_DIM))


def memory_attend(q, mk, mv):
    s = jnp.einsum('bthd,bnhd->bhtn', q, mk).astype(jnp.float32) / math.sqrt(XA_HEAD_DIM)
    p = jax.nn.softmax(s, axis=-1)
    return jnp.einsum('bhtn,bnhd->bthd', p.astype(mv.dtype), mv)


def mixer_layer(x, start, k_past, v_past, pool_hist, mk, mv, g_norm, w_in, pool_w, pool_scale, w_out):
    B, T, _ = x.shape
    h = rms_norm(x, g_norm)
    z = h @ w_in
    q_sb, k_sb, v_sb, g_sb, u_pool, g_pool, q_xa, g_xa = jnp.split(z, IN_SPLITS, axis=-1)
    q_sb = q_sb.reshape(B, T, SB_HEADS, HEAD_DIM)
    k_new = k_sb.reshape(B, T, SB_HEADS, HEAD_DIM)
    v_new = v_sb.reshape(B, T, SB_HEADS, HEAD_DIM)
    if k_past is None:
        k_all, v_all = k_new, v_new
    else:
        k_all = jnp.concatenate([k_past, k_new], axis=1)
        v_all = jnp.concatenate([v_past, v_new], axis=1)
    q_pos = start + jnp.arange(T)
    k_pos = jnp.arange(k_all.shape[1])
    o_sb = stick_breaking(q_sb, k_all, v_all, q_pos, k_pos).reshape(B, T, SB_WIDTH)
    pooled = multiscale_pool(u_pool, pool_hist, start).reshape(B, T, POOL_GROUPS, POOL_GROUP_DIM)
    o_pool = jnp.einsum('btgc,gcd->btgd', pooled, pool_w).reshape(B, T, POOL_WIDTH) * pool_scale
    new_hist = jnp.concatenate([pool_hist, u_pool], axis=1)[:, -POOL_STATE:]
    o_xa = memory_attend(q_xa.reshape(B, T, XA_HEADS, XA_HEAD_DIM), mk, mv).reshape(B, T, XA_WIDTH)
    mixed = jnp.concatenate([o_sb * jax.nn.silu(g_sb), o_pool * jax.nn.silu(g_pool), o_xa * jax.nn.silu(g_xa)], axis=-1)
    return x + mixed @ w_out, k_new, v_new, new_hist


def setup_inputs(seed: int = 0) -> dict:
    key = jax.random.key(seed)
    ks = jax.random.split(key, 17)

    def nrm(k, shape, s=1.0):
        return s * jax.random.normal(k, shape, jnp.float32)

    return {
        "x_prompt": nrm(ks[0], (BATCH, SEQ, D_MODEL)),
        "x_sample": nrm(ks[1], (DEC_BATCH, DEC_SEQ, D_MODEL)),
        "cache_sb_k": nrm(ks[2], (DEPTH, DEC_BATCH, PAST_LEN, SB_HEADS, HEAD_DIM)),
        "cache_sb_v": nrm(ks[3], (DEPTH, DEC_BATCH, PAST_LEN, SB_HEADS, HEAD_DIM)),
        "state_pool": nrm(ks[4], (DEPTH, DEC_BATCH, POOL_STATE, POOL_WIDTH)),
        "cache_mem_k": nrm(ks[5], (DEPTH, DEC_BATCH, N_MEM, XA_HEADS, XA_HEAD_DIM)),
        "cache_mem_v": nrm(ks[6], (DEPTH, DEC_BATCH, N_MEM, XA_HEADS, XA_HEAD_DIM)),
        "mem_prompt": nrm(ks[7], (BATCH, N_MEM, D_MODEL)),
        "g_norm": 1.0 + nrm(ks[8], (DEPTH, D_MODEL), 0.01),
        "w_in": nrm(ks[9], (DEPTH, D_MODEL, IN_WIDTH), D_MODEL ** -0.5),
        "pool_w": nrm(ks[10], (DEPTH, POOL_GROUPS, POOL_GROUP_DIM, POOL_GROUP_DIM), POOL_GROUP_DIM ** -0.5),
        "pool_scale": 1.0 + nrm(ks[11], (DEPTH, POOL_WIDTH), 0.1),
        "g_mem": 1.0 + nrm(ks[12], (DEPTH, D_MODEL), 0.01),
        "w_mem_kv": nrm(ks[13], (DEPTH, D_MODEL, 2 * XA_WIDTH), D_MODEL ** -0.5),
        "w_out": nrm(ks[14], (DEPTH, MIX_WIDTH, D_MODEL), MIX_WIDTH ** -0.5),
        "g_final": 1.0 + nrm(ks[15], (D_MODEL,), 0.01),
    }


def reference(x_prompt, x_sample, cache_sb_k, cache_sb_v, state_pool, cache_mem_k, cache_mem_v, mem_prompt,
              g_norm, w_in, pool_w, pool_scale, g_mem, w_mem_kv, w_out, g_final):
    past = cache_sb_k.shape[2]
    yp, ys = x_prompt, x_sample
    kp_l, vp_l, hp_l, mkp_l, mvp_l, ks_l, vs_l, hs_l = [], [], [], [], [], [], [], []
    for l in range(DEPTH):
        mk, mv = memory_kv(mem_prompt, g_mem[l], w_mem_kv[l])
        hist0 = jnp.zeros((x_prompt.shape[0], POOL_STATE, POOL_WIDTH), x_prompt.dtype)
        yp, kp, vp, hp = mixer_layer(yp, 0, None, None, hist0, mk, mv,
                                     g_norm[l], w_in[l], pool_w[l], pool_scale[l], w_out[l])
        ys, kn, vn, hn = mixer_layer(ys, past, cache_sb_k[l], cache_sb_v[l], state_pool[l],
                                     cache_mem_k[l], cache_mem_v[l],
                                     g_norm[l], w_in[l], pool_w[l], pool_scale[l], w_out[l])
        kp_l.append(kp); vp_l.append(vp); hp_l.append(hp); mkp_l.append(mk); mvp_l.append(mv)
        ks_l.append(kn); vs_l.append(vn); hs_l.append(hn)
    y_prompt = rms_norm(yp, g_final)
    y_sample = rms_norm(ys, g_final)
    sb_k_prompt = jnp.stack(kp_l, axis=0)
    sb_v_prompt = jnp.stack(vp_l, axis=0)
    pool_prompt = jnp.stack(hp_l, axis=0)
    mem_k_prompt = jnp.stack(mkp_l, axis=0)
    mem_v_prompt = jnp.stack(mvp_l, axis=0)
    sb_k_sample = jnp.stack(ks_l, axis=0)
    sb_v_sample = jnp.stack(vs_l, axis=0)
    pool_sample = jnp.stack(hs_l, axis=0)
    return (y_prompt, y_sample, sb_k_prompt, sb_v_prompt, pool_prompt, mem_k_prompt, mem_v_prompt, sb_k_sample, sb_v_sample, pool_sample)
```

```python
import functools

import jax
import jax.numpy as jnp
from jax import lax
from jax.experimental import pallas as pl
from jax.experimental.pallas import tpu as pltpu

F32 = jnp.float32
BF16 = jnp.bfloat16

HEAD_DIM = 64
LANES = 128
POOL_WINDOWS = (2, 4, 8, 16)
POOL_STATE = max(POOL_WINDOWS) - 1
HIST_ROWS = 16
EPS = 1e-6
VMEM_LIMIT = 56 * 1024 * 1024


def _rms(x, g):
    ms = jnp.mean(x * x, axis=-1, keepdims=True)
    return (x * lax.rsqrt(ms + EPS)) * g


def _proj_kernel(x_ref, g_ref, w_ref, k_ref, v_ref, qkv_ref, rest_ref, *, sb_width):
    h = _rms(x_ref[...], g_ref[...]).astype(BF16)
    s = sb_width
    q = jnp.dot(h, w_ref[:, 0:s], preferred_element_type=F32)
    qkv_ref[:, 0:s] = (q * (HEAD_DIM ** -0.5)).astype(BF16)
    k = jnp.dot(h, w_ref[:, s:2 * s], preferred_element_type=F32)
    k_ref[...] = k
    qkv_ref[:, s:2 * s] = k.astype(BF16)
    v = jnp.dot(h, w_ref[:, 2 * s:3 * s], preferred_element_type=F32)
    v_ref[...] = v
    qkv_ref[:, 2 * s:3 * s] = v.astype(BF16)
    rest_ref[...] = jnp.dot(h, w_ref[:, 3 * s:], preferred_element_type=F32)


def _project(x2d, g, w_bf16, sb_width, tm):
    m, d = x2d.shape
    n = w_bf16.shape[1]
    n_rest = n - 3 * sb_width
    return pl.pallas_call(
        functools.partial(_proj_kernel, sb_width=sb_width),
        grid=(m // tm,),
        in_specs=[
            pl.BlockSpec((tm, d), lambda i: (i, 0)),
            pl.BlockSpec((1, d), lambda i: (0, 0)),
            pl.BlockSpec((d, n), lambda i: (0, 0)),
        ],
        out_specs=[
            pl.BlockSpec((tm, sb_width), lambda i: (i, 0)),
            pl.BlockSpec((tm, sb_width), lambda i: (i, 0)),
            pl.BlockSpec((tm, 3 * sb_width), lambda i: (i, 0)),
            pl.BlockSpec((tm, n_rest), lambda i: (i, 0)),
        ],
        out_shape=[
            jax.ShapeDtypeStruct((m, sb_width), F32),
            jax.ShapeDtypeStruct((m, sb_width), F32),
            jax.ShapeDtypeStruct((m, 3 * sb_width), BF16),
            jax.ShapeDtypeStruct((m, n_rest), F32),
        ],
        compiler_params=pltpu.CompilerParams(
            dimension_semantics=("arbitrary",), vmem_limit_bytes=VMEM_LIMIT),
        name="proj",
    )(x2d, g, w_bf16)


def _memkv_kernel(x_ref, g_ref, w_ref, mk_ref, mv_ref, *, xa_width):
    h = _rms(x_ref[...], g_ref[...]).astype(BF16)
    kv = jnp.dot(h, w_ref[...], preferred_element_type=F32)
    mk_ref[...] = kv[:, :xa_width]
    mv_ref[...] = kv[:, xa_width:]


def _memory_kv(mem2d, g, w_bf16, tm):
    m, d = mem2d.shape
    xa_width = w_bf16.shape[1] // 2
    return pl.pallas_call(
        functools.partial(_memkv_kernel, xa_width=xa_width),
        grid=(m // tm,),
        in_specs=[
            pl.BlockSpec((tm, d), lambda i: (i, 0)),
            pl.BlockSpec((1, d), lambda i: (0, 0)),
            pl.BlockSpec((d, 2 * xa_width), lambda i: (0, 0)),
        ],
        out_specs=[
            pl.BlockSpec((tm, xa_width), lambda i: (i, 0)),
            pl.BlockSpec((tm, xa_width), lambda i: (i, 0)),
        ],
        out_shape=[
            jax.ShapeDtypeStruct((m, xa_width), F32),
            jax.ShapeDtypeStruct((m, xa_width), F32),
        ],
        compiler_params=pltpu.CompilerParams(
            dimension_semantics=("arbitrary",), vmem_limit_bytes=VMEM_LIMIT),
        name="memkv",
    )(mem2d, g, w_bf16)


def _strict_upper(n):
    r = lax.broadcasted_iota(jnp.int32, (n, n), 0)
    c = lax.broadcasted_iota(jnp.int32, (n, n), 1)
    return jnp.where(r > c, 1.0, 0.0).astype(BF16)


def _sb_block(qh, kblk, vblk, upper, carry, mask):
    z = lax.dot_general(qh, kblk, (((1,), (1,)), ((), ())), preferred_element_type=F32)
    sp = jnp.maximum(z, 0.0) + jnp.log(1.0 + jnp.exp(-jnp.abs(z)))
    if mask is not None:
        sp = jnp.where(mask, sp, 0.0)
    hi = sp.astype(BF16)
    lo = (sp - hi.astype(F32)).astype(BF16)
    later = (jnp.dot(hi, upper, preferred_element_type=F32)
             + jnp.dot(lo, upper, preferred_element_type=F32))
    w = jnp.exp((z - sp) - later - carry)
    if mask is not None:
        w = jnp.where(mask, w, 0.0)
    out = jnp.dot(w.astype(BF16), vblk, preferred_element_type=F32)
    return out, carry + jnp.sum(sp, axis=-1, keepdims=True)


def _sb_kernel(q_ref, kd_ref, vd_ref, kp_ref, vp_ref, o_ref, acc0_ref, acc1_ref,
               *, tq, tk, n_past_static):
    q = q_ref[0]
    lane = lax.broadcasted_iota(jnp.int32, (tq, LANES), 1)
    first = lane < HEAD_DIM
    zero = jnp.zeros_like(q)
    q0 = jnp.where(first, q, zero)
    q1 = jnp.where(first, zero, q)

    r = lax.broadcasted_iota(jnp.int32, (tq, tq), 0)
    c = lax.broadcasted_iota(jnp.int32, (tq, tq), 1)
    causal = c < r
    upper_d = _strict_upper(tq)
    kd = kd_ref[0]
    vd = vd_ref[0]
    zc = jnp.zeros((tq, 1), F32)
    o0, c0 = _sb_block(q0, kd, vd, upper_d, zc, causal)
    o1, c1 = _sb_block(q1, kd, vd, upper_d, zc, causal)
    acc0_ref[...] = o0
    acc1_ref[...] = o1

    if n_past_static is None:
        n_past = pl.program_id(2) * (tq // tk)
    else:
        n_past = n_past_static
    upper = _strict_upper(tk)

    def body(j, carry):
        c0, c1 = carry
        start = pl.multiple_of((n_past - 1 - j) * tk, tk)
        kblk = kp_ref[0, pl.ds(start, tk), :].astype(BF16)
        vblk = vp_ref[0, pl.ds(start, tk), :].astype(BF16)
        o0, c0 = _sb_block(q0, kblk, vblk, upper, c0, None)
        o1, c1 = _sb_block(q1, kblk, vblk, upper, c1, None)
        acc0_ref[...] += o0
        acc1_ref[...] += o1
        return c0, c1

    lax.fori_loop(0, n_past, body, (c0, c1))
    o_ref[0] = jnp.where(first, acc0_ref[...], acc1_ref[...])


def _stick_breaking(qkv, k_past, v_past, *, tq, tk, sb_width):
    b, t, _ = qkv.shape
    pairs = sb_width // LANES
    if k_past is None:
        k_past, v_past = qkv, qkv
        k_off, v_off = pairs, 2 * pairs
        n_past_static = None
    else:
        k_off, v_off = 0, 0
        n_past_static = k_past.shape[1] // tk
    p = k_past.shape[1]
    return pl.pallas_call(
        functools.partial(_sb_kernel, tq=tq, tk=tk, n_past_static=n_past_static),
        grid=(b, pairs, t // tq),
        in_specs=[
            pl.BlockSpec((1, tq, LANES), lambda bi, hp, i: (bi, i, hp)),
            pl.BlockSpec((1, tq, LANES), lambda bi, hp, i: (bi, i, pairs + hp)),
            pl.BlockSpec((1, tq, LANES), lambda bi, hp, i: (bi, i, 2 * pairs + hp)),
            pl.BlockSpec((1, p, LANES), lambda bi, hp, i: (bi, 0, k_off + hp)),
            pl.BlockSpec((1, p, LANES), lambda bi, hp, i: (bi, 0, v_off + hp)),
        ],
        out_specs=pl.BlockSpec((1, tq, LANES), lambda bi, hp, i: (bi, i, hp)),
        out_shape=jax.ShapeDtypeStruct((b, t, sb_width), F32),
        scratch_shapes=[pltpu.VMEM((tq, LANES), F32), pltpu.VMEM((tq, LANES), F32)],
        compiler_params=pltpu.CompilerParams(
            dimension_semantics=("arbitrary", "arbitrary", "arbitrary"),
            vmem_limit_bytes=VMEM_LIMIT),
        name="stick_breaking",
    )(qkv, qkv, qkv, k_past, v_past)


def _silu(g):
    return g / (1.0 + jnp.exp(-g))


def _epilogue_kernel(x_ref, osb_ref, rest_ref, prev_ref, hist0_ref, mk_ref, mv_ref,
                     poolw_ref, pscale_ref, wout_ref, gfin_ref, y_ref,
                     *, tt, start, sb_width, pool_width, xa_width, xa_heads, apply_final):
    i = pl.program_id(1)
    s, pw, xw = sb_width, pool_width, xa_width
    g_sb = rest_ref[0, :, 0:s]
    u = rest_ref[0, :, s:s + pw]
    g_pool = rest_ref[0, :, s + pw:s + 2 * pw]
    q_xa = rest_ref[0, :, s + 2 * pw:s + 2 * pw + xw]
    g_xa = rest_ref[0, :, s + 2 * pw + xw:s + 2 * pw + 2 * xw]

    hist = jnp.where(i == 0, hist0_ref[0], prev_ref[0])
    ext = jnp.concatenate([hist, u], axis=0)
    sums = {1: ext}
    width = 1
    while width < max(POOL_WINDOWS):
        prev = sums[width]
        sums[2 * width] = prev + pltpu.roll(prev, width, 0)
        width *= 2
    lane = lax.broadcasted_iota(jnp.int32, (tt, pw), 1)
    group = lane // (pw // len(POOL_WINDOWS))
    pos = start + i * tt + lax.broadcasted_iota(jnp.int32, (tt, pw), 0)
    win_sum = jnp.zeros((tt, pw), F32)
    win = jnp.zeros((tt, pw), jnp.int32)
    for gi, w in enumerate(POOL_WINDOWS):
        sel = group == gi
        win_sum = jnp.where(sel, sums[w][HIST_ROWS:], win_sum)
        win = jnp.where(sel, w, win)
    cnt = jnp.minimum(pos + 1, win).astype(F32)
    pooled = win_sum / cnt - u
    o_pool = jnp.dot(pooled.astype(BF16), poolw_ref[...],
                     preferred_element_type=F32) * pscale_ref[...]

    mk = mk_ref[0].astype(BF16)
    mv = mv_ref[0].astype(BF16)
    xa_dim = xw // xa_heads
    lane_x = lax.broadcasted_iota(jnp.int32, (tt, xw), 1)
    head_x = lane_x // xa_dim
    qs = q_xa * (xa_dim ** -0.5)
    o_xa = jnp.zeros((tt, xw), F32)
    for hh in range(xa_heads):
        sel = head_x == hh
        qh = jnp.where(sel, qs, 0.0).astype(BF16)
        sc = lax.dot_general(qh, mk, (((1,), (1,)), ((), ())), preferred_element_type=F32)
        e = jnp.exp(sc - jnp.max(sc, axis=-1, keepdims=True))
        p = e / jnp.sum(e, axis=-1, keepdims=True)
        oh = jnp.dot(p.astype(BF16), mv, preferred_element_type=F32)
        o_xa = jnp.where(sel, oh, o_xa)

    m_sb = (osb_ref[0] * _silu(g_sb)).astype(BF16)
    m_pool = (o_pool * _silu(g_pool)).astype(BF16)
    m_xa = (o_xa * _silu(g_xa)).astype(BF16)
    y = (x_ref[0]
         + jnp.dot(m_sb, wout_ref[0:s, :], preferred_element_type=F32)
         + jnp.dot(m_pool, wout_ref[s:s + pw, :], preferred_element_type=F32)
         + jnp.dot(m_xa, wout_ref[s + pw:s + pw + xw, :], preferred_element_type=F32))
    if apply_final:
        y = _rms(y, gfin_ref[...])
    y_ref[0] = y


def _epilogue(x, o_sb, rest, hist0, mk, mv, pool_bd, pool_scale, w_out_bf16, g_final,
              *, tt, start, sb_width, pool_width, xa_width, xa_heads, apply_final):
    b, t, d = x.shape
    n_rest = rest.shape[-1]
    n_mem = mk.shape[1]
    hist_blocks = tt // HIST_ROWS
    u_col = sb_width // pool_width
    kern = functools.partial(
        _epilogue_kernel, tt=tt, start=start, sb_width=sb_width, pool_width=pool_width,
        xa_width=xa_width, xa_heads=xa_heads, apply_final=apply_final)
    return pl.pallas_call(
        kern,
        grid=(b, t // tt),
        in_specs=[
            pl.BlockSpec((1, tt, d), lambda bi, i: (bi, i, 0)),
            pl.BlockSpec((1, tt, sb_width), lambda bi, i: (bi, i, 0)),
            pl.BlockSpec((1, tt, n_rest), lambda bi, i: (bi, i, 0)),
            pl.BlockSpec((1, HIST_ROWS, pool_width),
                         lambda bi, i: (bi, jnp.maximum(i * hist_blocks - 1, 0), u_col)),
            pl.BlockSpec((1, HIST_ROWS, pool_width), lambda bi, i: (bi, 0, 0)),
            pl.BlockSpec((1, n_mem, xa_width), lambda bi, i: (bi, 0, 0)),
            pl.BlockSpec((1, n_mem, xa_width), lambda bi, i: (bi, 0, 0)),
            pl.BlockSpec((pool_width, pool_width), lambda bi, i: (0, 0)),
            pl.BlockSpec((1, pool_width), lambda bi, i: (0, 0)),
            pl.BlockSpec((d, d), lambda bi, i: (0, 0)),
            pl.BlockSpec((1, d), lambda bi, i: (0, 0)),
        ],
        out_specs=pl.BlockSpec((1, tt, d), lambda bi, i: (bi, i, 0)),
        out_shape=jax.ShapeDtypeStruct((b, t, d), F32),
        compiler_params=pltpu.CompilerParams(
            dimension_semantics=("arbitrary", "arbitrary"), vmem_limit_bytes=VMEM_LIMIT),
        name="epilogue",
    )(x, o_sb, rest, rest, hist0, mk, mv, pool_bd, pool_scale, w_out_bf16, g_final)


def _block_diag(pool_w):
    g, c, _ = pool_w.shape
    eye = jnp.eye(g, dtype=pool_w.dtype)
    return (eye[:, None, :, None] * pool_w[:, :, None, :]).reshape(g * c, g * c)


def _pad_hist(hist):
    return jnp.pad(hist, ((0, 0), (HIST_ROWS - hist.shape[1], 0), (0, 0)))


def _mixer_layer(x, start, k_past, v_past, pool_hist, mk, mv, g_norm, w_in_bf16, pool_bd,
                 pool_scale, w_out_bf16, g_final, *, dims, tiles, apply_final):
    sb_width, pool_width, xa_width, xa_heads = dims
    tm, tq, tk, tt = tiles
    b, t, d = x.shape
    k2d, v2d, qkv, rest = _project(x.reshape(b * t, d), g_norm, w_in_bf16, sb_width, tm)
    qkv = qkv.reshape(b, t, -1)
    rest = rest.reshape(b, t, -1)
    o_sb = _stick_breaking(qkv, k_past, v_past, tq=tq, tk=tk, sb_width=sb_width)
    y = _epilogue(x, o_sb, rest, _pad_hist(pool_hist), mk, mv, pool_bd, pool_scale,
                  w_out_bf16, g_final, tt=tt, start=start, sb_width=sb_width,
                  pool_width=pool_width, xa_width=xa_width, xa_heads=xa_heads,
                  apply_final=apply_final)
    u_pool = rest[:, :, sb_width:sb_width + pool_width]
    new_hist = jnp.concatenate([pool_hist, u_pool], axis=1)[:, -POOL_STATE:]
    return y, k2d, v2d, new_hist


def kernel(x_prompt, x_sample, cache_sb_k, cache_sb_v, state_pool, cache_mem_k, cache_mem_v,
           mem_prompt, g_norm, w_in, pool_w, pool_scale, g_mem, w_mem_kv, w_out, g_final):
    depth = g_norm.shape[0]
    bp, tp, d = x_prompt.shape
    bs, ts, _ = x_sample.shape
    past = cache_sb_k.shape[2]
    sb_heads, head_dim = cache_sb_k.shape[3], cache_sb_k.shape[4]
    assert head_dim == HEAD_DIM
    sb_width = sb_heads * head_dim
    pool_width = pool_w.shape[1] * pool_w.shape[2]
    n_mem, xa_heads, xa_dim = cache_mem_k.shape[2:]
    xa_width = xa_heads * xa_dim
    dims = (sb_width, pool_width, xa_width, xa_heads)
    g_fin = g_final.reshape(1, d)

    yp, ys = x_prompt, x_sample
    kp_l, vp_l, hp_l, mkp_l, mvp_l, ks_l, vs_l, hs_l = [], [], [], [], [], [], [], []
    for l in range(depth):
        final = l == depth - 1
        g_l = g_norm[l].reshape(1, d)
        w_in_b = w_in[l].astype(BF16)
        w_out_b = w_out[l].astype(BF16)
        pool_bd = _block_diag(pool_w[l]).astype(BF16)
        p_scale = pool_scale[l].reshape(1, pool_width)
        mk, mv = _memory_kv(mem_prompt.reshape(bp * n_mem, d), g_mem[l].reshape(1, d),
                            w_mem_kv[l].astype(BF16), tm=256)
        mk = mk.reshape(bp, n_mem, xa_width)
        mv = mv.reshape(bp, n_mem, xa_width)
        hist0 = jnp.zeros((bp, POOL_STATE, pool_width), x_prompt.dtype)
        yp, kp, vp, hp = _mixer_layer(
            yp, 0, None, None, hist0, mk, mv, g_l, w_in_b, pool_bd, p_scale, w_out_b, g_fin,
            dims=dims, tiles=(512, 256, 256, 256), apply_final=final)
        ys, kn, vn, hn = _mixer_layer(
            ys, past, cache_sb_k[l].reshape(bs, past, sb_width),
            cache_sb_v[l].reshape(bs, past, sb_width), state_pool[l],
            cache_mem_k[l].reshape(bs, n_mem, xa_width),
            cache_mem_v[l].reshape(bs, n_mem, xa_width),
            g_l, w_in_b, pool_bd, p_scale, w_out_b, g_fin,
            dims=dims, tiles=(512, ts, 256, ts), apply_final=final)
        kp_l.append(kp.reshape(bp, tp, sb_heads, head_dim))
        vp_l.append(vp.reshape(bp, tp, sb_heads, head_dim))
        hp_l.append(hp)
        mkp_l.append(mk.reshape(bp, n_mem, xa_heads, xa_dim))
        mvp_l.append(mv.reshape(bp, n_mem, xa_heads, xa_dim))
        ks_l.append(kn.reshape(bs, ts, sb_heads, head_dim))
        vs_l.append(vn.reshape(bs, ts, sb_heads, head_dim))
        hs_l.append(hn)
    stack = lambda xs: jnp.stack(xs, axis=0)
    return (yp, ys, stack(kp_l), stack(vp_l), stack(hp_l), stack(mkp_l), stack(mvp_l),
            stack(ks_l), stack(vs_l), stack(hs_l))
```

```python
import functools

import jax
import jax.numpy as jnp
from jax import lax
from jax.experimental import pallas as pl
from jax.experimental.pallas import tpu as pltpu

F32 = jnp.float32
BF16 = jnp.bfloat16

HEAD_DIM = 64
LANES = 128
POOL_WINDOWS = (2, 4, 8, 16)
POOL_STATE = max(POOL_WINDOWS) - 1
HIST_ROWS = 16
EPS = 1e-6
DEAD_CARRY = 104.0
VMEM_LIMIT = 56 * 1024 * 1024


def _rms(x, g):
    ms = jnp.mean(x * x, axis=-1, keepdims=True)
    return (x * lax.rsqrt(ms + EPS)) * g


def _proj_kernel(x_ref, g_ref, w_ref, k_ref, v_ref, qkv_ref, rest_ref, *, sb_width):
    h = _rms(x_ref[...], g_ref[...]).astype(BF16)
    s = sb_width
    q = jnp.dot(h, w_ref[:, 0:s], preferred_element_type=F32)
    qkv_ref[:, 0:s] = (q * (HEAD_DIM ** -0.5)).astype(BF16)
    k = jnp.dot(h, w_ref[:, s:2 * s], preferred_element_type=F32)
    k_ref[...] = k
    qkv_ref[:, s:2 * s] = k.astype(BF16)
    v = jnp.dot(h, w_ref[:, 2 * s:3 * s], preferred_element_type=F32)
    v_ref[...] = v
    qkv_ref[:, 2 * s:3 * s] = v.astype(BF16)
    rest_ref[...] = jnp.dot(h, w_ref[:, 3 * s:], preferred_element_type=F32)


def _project(x2d, g, w_bf16, sb_width, tm):
    m, d = x2d.shape
    n = w_bf16.shape[1]
    n_rest = n - 3 * sb_width
    return pl.pallas_call(
        functools.partial(_proj_kernel, sb_width=sb_width),
        grid=(m // tm,),
        in_specs=[
            pl.BlockSpec((tm, d), lambda i: (i, 0)),
            pl.BlockSpec((1, d), lambda i: (0, 0)),
            pl.BlockSpec((d, n), lambda i: (0, 0)),
        ],
        out_specs=[
            pl.BlockSpec((tm, sb_width), lambda i: (i, 0)),
            pl.BlockSpec((tm, sb_width), lambda i: (i, 0)),
            pl.BlockSpec((tm, 3 * sb_width), lambda i: (i, 0)),
            pl.BlockSpec((tm, n_rest), lambda i: (i, 0)),
        ],
        out_shape=[
            jax.ShapeDtypeStruct((m, sb_width), F32),
            jax.ShapeDtypeStruct((m, sb_width), F32),
            jax.ShapeDtypeStruct((m, 3 * sb_width), BF16),
            jax.ShapeDtypeStruct((m, n_rest), F32),
        ],
        compiler_params=pltpu.CompilerParams(
            dimension_semantics=("arbitrary",), vmem_limit_bytes=VMEM_LIMIT),
        name="proj",
    )(x2d, g, w_bf16)


def _memkv_kernel(x_ref, g_ref, w_ref, mk_ref, mv_ref, *, xa_width):
    h = _rms(x_ref[...], g_ref[...]).astype(BF16)
    kv = jnp.dot(h, w_ref[...], preferred_element_type=F32)
    mk_ref[...] = kv[:, :xa_width]
    mv_ref[...] = kv[:, xa_width:]


def _memory_kv(mem2d, g, w_bf16, tm):
    m, d = mem2d.shape
    xa_width = w_bf16.shape[1] // 2
    return pl.pallas_call(
        functools.partial(_memkv_kernel, xa_width=xa_width),
        grid=(m // tm,),
        in_specs=[
            pl.BlockSpec((tm, d), lambda i: (i, 0)),
            pl.BlockSpec((1, d), lambda i: (0, 0)),
            pl.BlockSpec((d, 2 * xa_width), lambda i: (0, 0)),
        ],
        out_specs=[
            pl.BlockSpec((tm, xa_width), lambda i: (i, 0)),
            pl.BlockSpec((tm, xa_width), lambda i: (i, 0)),
        ],
        out_shape=[
            jax.ShapeDtypeStruct((m, xa_width), F32),
            jax.ShapeDtypeStruct((m, xa_width), F32),
        ],
        compiler_params=pltpu.CompilerParams(
            dimension_semantics=("arbitrary",), vmem_limit_bytes=VMEM_LIMIT),
        name="memkv",
    )(mem2d, g, w_bf16)


def _strict_upper(n):
    r = lax.broadcasted_iota(jnp.int32, (n, n), 0)
    c = lax.broadcasted_iota(jnp.int32, (n, n), 1)
    return jnp.where(r > c, 1.0, 0.0).astype(BF16)


def _sb_block(qh, kblk, vblk, upper, carry, mask):
    z = lax.dot_general(qh, kblk, (((1,), (1,)), ((), ())), preferred_element_type=F32)
    sp = jnp.maximum(z, 0.0) + jnp.log(1.0 + jnp.exp(-jnp.abs(z)))
    if mask is not None:
        sp = jnp.where(mask, sp, 0.0)
    hi = sp.astype(BF16)
    lo = (sp - hi.astype(F32)).astype(BF16)
    later = (jnp.dot(hi, upper, preferred_element_type=F32)
             + jnp.dot(lo, upper, preferred_element_type=F32))
    w = jnp.exp((z - sp) - later - carry)
    if mask is not None:
        w = jnp.where(mask, w, 0.0)
    out = jnp.dot(w.astype(BF16), vblk, preferred_element_type=F32)
    return out, carry + jnp.sum(sp, axis=-1, keepdims=True)


def _sb_kernel(q_ref, kd_ref, vd_ref, kp_ref, vp_ref, o_ref, acc0_ref, acc1_ref,
               *, tq, tk, n_past_static):
    q = q_ref[0]
    lane = lax.broadcasted_iota(jnp.int32, (tq, LANES), 1)
    first = lane < HEAD_DIM
    zero = jnp.zeros_like(q)
    q0 = jnp.where(first, q, zero)
    q1 = jnp.where(first, zero, q)

    r = lax.broadcasted_iota(jnp.int32, (tq, tq), 0)
    c = lax.broadcasted_iota(jnp.int32, (tq, tq), 1)
    causal = c < r
    upper_d = _strict_upper(tq)
    kd = kd_ref[0]
    vd = vd_ref[0]
    zc = jnp.zeros((tq, 1), F32)
    o0, c0 = _sb_block(q0, kd, vd, upper_d, zc, causal)
    o1, c1 = _sb_block(q1, kd, vd, upper_d, zc, causal)
    acc0_ref[...] = o0
    acc1_ref[...] = o1

    if n_past_static is None:
        n_past = pl.program_id(2) * (tq // tk)
    else:
        n_past = n_past_static
    upper = _strict_upper(tk)

    def live(state):
        j, c0, c1 = state
        return jnp.logical_and(j < n_past, jnp.min(jnp.minimum(c0, c1)) <= DEAD_CARRY)

    def body(state):
        j, c0, c1 = state
        start = pl.multiple_of((n_past - 1 - j) * tk, tk)
        kblk = kp_ref[0, pl.ds(start, tk), :].astype(BF16)
        vblk = vp_ref[0, pl.ds(start, tk), :].astype(BF16)
        o0, c0 = _sb_block(q0, kblk, vblk, upper, c0, None)
        o1, c1 = _sb_block(q1, kblk, vblk, upper, c1, None)
        acc0_ref[...] += o0
        acc1_ref[...] += o1
        return j + 1, c0, c1

    lax.while_loop(live, body, (jnp.int32(0), c0, c1))
    o_ref[0] = jnp.where(first, acc0_ref[...], acc1_ref[...])


def _stick_breaking(qkv, k_past, v_past, *, tq, tk, sb_width):
    b, t, _ = qkv.shape
    pairs = sb_width // LANES
    if k_past is None:
        k_past, v_past = qkv, qkv
        k_off, v_off = pairs, 2 * pairs
        n_past_static = None
    else:
        k_off, v_off = 0, 0
        n_past_static = k_past.shape[1] // tk
    p = k_past.shape[1]
    return pl.pallas_call(
        functools.partial(_sb_kernel, tq=tq, tk=tk, n_past_static=n_past_static),
        grid=(b, pairs, t // tq),
        in_specs=[
            pl.BlockSpec((1, tq, LANES), lambda bi, hp, i: (bi, i, hp)),
            pl.BlockSpec((1, tq, LANES), lambda bi, hp, i: (bi, i, pairs + hp)),
            pl.BlockSpec((1, tq, LANES), lambda bi, hp, i: (bi, i, 2 * pairs + hp)),
            pl.BlockSpec((1, p, LANES), lambda bi, hp, i: (bi, 0, k_off + hp)),
            pl.BlockSpec((1, p, LANES), lambda bi, hp, i: (bi, 0, v_off + hp)),
        ],
        out_specs=pl.BlockSpec((1, tq, LANES), lambda bi, hp, i: (bi, i, hp)),
        out_shape=jax.ShapeDtypeStruct((b, t, sb_width), F32),
        scratch_shapes=[pltpu.VMEM((tq, LANES), F32), pltpu.VMEM((tq, LANES), F32)],
        compiler_params=pltpu.CompilerParams(
            dimension_semantics=("arbitrary", "arbitrary", "arbitrary"),
            vmem_limit_bytes=VMEM_LIMIT),
        name="stick_breaking",
    )(qkv, qkv, qkv, k_past, v_past)


def _silu(g):
    return g / (1.0 + jnp.exp(-g))


def _epilogue_kernel(x_ref, osb_ref, rest_ref, prev_ref, hist0_ref, mk_ref, mv_ref,
                     poolw_ref, pscale_ref, wout_ref, gfin_ref, y_ref,
                     *, tt, start, sb_width, pool_width, xa_width, xa_heads, apply_final):
    i = pl.program_id(1)
    s, pw, xw = sb_width, pool_width, xa_width
    g_sb = rest_ref[0, :, 0:s]
    u = rest_ref[0, :, s:s + pw]
    g_pool = rest_ref[0, :, s + pw:s + 2 * pw]
    q_xa = rest_ref[0, :, s + 2 * pw:s + 2 * pw + xw]
    g_xa = rest_ref[0, :, s + 2 * pw + xw:s + 2 * pw + 2 * xw]

    hist = jnp.where(i == 0, hist0_ref[0], prev_ref[0])
    ext = jnp.concatenate([hist, u], axis=0)
    sums = {1: ext}
    width = 1
    while width < max(POOL_WINDOWS):
        prev = sums[width]
        sums[2 * width] = prev + pltpu.roll(prev, width, 0)
        width *= 2
    lane = lax.broadcasted_iota(jnp.int32, (tt, pw), 1)
    group = lane // (pw // len(POOL_WINDOWS))
    pos = start + i * tt + lax.broadcasted_iota(jnp.int32, (tt, pw), 0)
    win_sum = jnp.zeros((tt, pw), F32)
    win = jnp.zeros((tt, pw), jnp.int32)
    for gi, w in enumerate(POOL_WINDOWS):
        sel = group == gi
        win_sum = jnp.where(sel, sums[w][HIST_ROWS:], win_sum)
        win = jnp.where(sel, w, win)
    cnt = jnp.minimum(pos + 1, win).astype(F32)
    pooled = win_sum / cnt - u
    o_pool = jnp.dot(pooled.astype(BF16), poolw_ref[...],
                     preferred_element_type=F32) * pscale_ref[...]

    mk = mk_ref[0].astype(BF16)
    mv = mv_ref[0].astype(BF16)
    xa_dim = xw // xa_heads
    lane_x = lax.broadcasted_iota(jnp.int32, (tt, xw), 1)
    head_x = lane_x // xa_dim
    qs = q_xa * (xa_dim ** -0.5)
    o_xa = jnp.zeros((tt, xw), F32)
    for hh in range(xa_heads):
        sel = head_x == hh
        qh = jnp.where(sel, qs, 0.0).astype(BF16)
        sc = lax.dot_general(qh, mk, (((1,), (1,)), ((), ())), preferred_element_type=F32)
        e = jnp.exp(sc - jnp.max(sc, axis=-1, keepdims=True))
        p = e / jnp.sum(e, axis=-1, keepdims=True)
        oh = jnp.dot(p.astype(BF16), mv, preferred_element_type=F32)
        o_xa = jnp.where(sel, oh, o_xa)

    m_sb = (osb_ref[0] * _silu(g_sb)).astype(BF16)
    m_pool = (o_pool * _silu(g_pool)).astype(BF16)
    m_xa = (o_xa * _silu(g_xa)).astype(BF16)
    y = (x_ref[0]
         + jnp.dot(m_sb, wout_ref[0:s, :], preferred_element_type=F32)
         + jnp.dot(m_pool, wout_ref[s:s + pw, :], preferred_element_type=F32)
         + jnp.dot(m_xa, wout_ref[s + pw:s + pw + xw, :], preferred_element_type=F32))
    if apply_final:
        y = _rms(y, gfin_ref[...])
    y_ref[0] = y


def _epilogue(x, o_sb, rest, hist0, mk, mv, pool_bd, pool_scale, w_out_bf16, g_final,
              *, tt, start, sb_width, pool_width, xa_width, xa_heads, apply_final):
    b, t, d = x.shape
    n_rest = rest.shape[-1]
    n_mem = mk.shape[1]
    hist_blocks = tt // HIST_ROWS
    u_col = sb_width // pool_width
    kern = functools.partial(
        _epilogue_kernel, tt=tt, start=start, sb_width=sb_width, pool_width=pool_width,
        xa_width=xa_width, xa_heads=xa_heads, apply_final=apply_final)
    return pl.pallas_call(
        kern,
        grid=(b, t // tt),
        in_specs=[
            pl.BlockSpec((1, tt, d), lambda bi, i: (bi, i, 0)),
            pl.BlockSpec((1, tt, sb_width), lambda bi, i: (bi, i, 0)),
            pl.BlockSpec((1, tt, n_rest), lambda bi, i: (bi, i, 0)),
            pl.BlockSpec((1, HIST_ROWS, pool_width),
                         lambda bi, i: (bi, jnp.maximum(i * hist_blocks - 1, 0), u_col)),
            pl.BlockSpec((1, HIST_ROWS, pool_width), lambda bi, i: (bi, 0, 0)),
            pl.BlockSpec((1, n_mem, xa_width), lambda bi, i: (bi, 0, 0)),
            pl.BlockSpec((1, n_mem, xa_width), lambda bi, i: (bi, 0, 0)),
            pl.BlockSpec((pool_width, pool_width), lambda bi, i: (0, 0)),
            pl.BlockSpec((1, pool_width), lambda bi, i: (0, 0)),
            pl.BlockSpec((d, d), lambda bi, i: (0, 0)),
            pl.BlockSpec((1, d), lambda bi, i: (0, 0)),
        ],
        out_specs=pl.BlockSpec((1, tt, d), lambda bi, i: (bi, i, 0)),
        out_shape=jax.ShapeDtypeStruct((b, t, d), F32),
        compiler_params=pltpu.CompilerParams(
            dimension_semantics=("arbitrary", "arbitrary"), vmem_limit_bytes=VMEM_LIMIT),
        name="epilogue",
    )(x, o_sb, rest, rest, hist0, mk, mv, pool_bd, pool_scale, w_out_bf16, g_final)


def _block_diag(pool_w):
    g, c, _ = pool_w.shape
    eye = jnp.eye(g, dtype=pool_w.dtype)
    return (eye[:, None, :, None] * pool_w[:, :, None, :]).reshape(g * c, g * c)


def _pad_hist(hist):
    return jnp.pad(hist, ((0, 0), (HIST_ROWS - hist.shape[1], 0), (0, 0)))


def _mixer_layer(x, start, k_past, v_past, pool_hist, mk, mv, g_norm, w_in_bf16, pool_bd,
                 pool_scale, w_out_bf16, g_final, *, dims, tiles, apply_final):
    sb_width, pool_width, xa_width, xa_heads = dims
    tm, tq, tk, tt = tiles
    b, t, d = x.shape
    k2d, v2d, qkv, rest = _project(x.reshape(b * t, d), g_norm, w_in_bf16, sb_width, tm)
    qkv = qkv.reshape(b, t, -1)
    rest = rest.reshape(b, t, -1)
    o_sb = _stick_breaking(qkv, k_past, v_past, tq=tq, tk=tk, sb_width=sb_width)
    y = _epilogue(x, o_sb, rest, _pad_hist(pool_hist), mk, mv, pool_bd, pool_scale,
                  w_out_bf16, g_final, tt=tt, start=start, sb_width=sb_width,
                  pool_width=pool_width, xa_width=xa_width, xa_heads=xa_heads,
                  apply_final=apply_final)
    u_pool = rest[:, :, sb_width:sb_width + pool_width]
    new_hist = jnp.concatenate([pool_hist, u_pool], axis=1)[:, -POOL_STATE:]
    return y, k2d, v2d, new_hist


def kernel(x_prompt, x_sample, cache_sb_k, cache_sb_v, state_pool, cache_mem_k, cache_mem_v,
           mem_prompt, g_norm, w_in, pool_w, pool_scale, g_mem, w_mem_kv, w_out, g_final):
    depth = g_norm.shape[0]
    bp, tp, d = x_prompt.shape
    bs, ts, _ = x_sample.shape
    past = cache_sb_k.shape[2]
    sb_heads, head_dim = cache_sb_k.shape[3], cache_sb_k.shape[4]
    assert head_dim == HEAD_DIM
    sb_width = sb_heads * head_dim
    pool_width = pool_w.shape[1] * pool_w.shape[2]
    n_mem, xa_heads, xa_dim = cache_mem_k.shape[2:]
    xa_width = xa_heads * xa_dim
    dims = (sb_width, pool_width, xa_width, xa_heads)
    g_fin = g_final.reshape(1, d)

    yp, ys = x_prompt, x_sample
    kp_l, vp_l, hp_l, mkp_l, mvp_l, ks_l, vs_l, hs_l = [], [], [], [], [], [], [], []
    for l in range(depth):
        final = l == depth - 1
        g_l = g_norm[l].reshape(1, d)
        w_in_b = w_in[l].astype(BF16)
        w_out_b = w_out[l].astype(BF16)
        pool_bd = _block_diag(pool_w[l]).astype(BF16)
        p_scale = pool_scale[l].reshape(1, pool_width)
        mk, mv = _memory_kv(mem_prompt.reshape(bp * n_mem, d), g_mem[l].reshape(1, d),
                            w_mem_kv[l].astype(BF16), tm=256)
        mk = mk.reshape(bp, n_mem, xa_width)
        mv = mv.reshape(bp, n_mem, xa_width)
        hist0 = jnp.zeros((bp, POOL_STATE, pool_width), x_prompt.dtype)
        yp, kp, vp, hp = _mixer_layer(
            yp, 0, None, None, hist0, mk, mv, g_l, w_in_b, pool_bd, p_scale, w_out_b, g_fin,
            dims=dims, tiles=(512, 256, 256, 256), apply_final=final)
        ys, kn, vn, hn = _mixer_layer(
            ys, past, cache_sb_k[l].reshape(bs, past, sb_width),
            cache_sb_v[l].reshape(bs, past, sb_width), state_pool[l],
            cache_mem_k[l].reshape(bs, n_mem, xa_width),
            cache_mem_v[l].reshape(bs, n_mem, xa_width),
            g_l, w_in_b, pool_bd, p_scale, w_out_b, g_fin,
            dims=dims, tiles=(512, ts, 256, ts), apply_final=final)
        kp_l.append(kp.reshape(bp, tp, sb_heads, head_dim))
        vp_l.append(vp.reshape(bp, tp, sb_heads, head_dim))
        hp_l.append(hp)
        mkp_l.append(mk.reshape(bp, n_mem, xa_heads, xa_dim))
        mvp_l.append(mv.reshape(bp, n_mem, xa_heads, xa_dim))
        ks_l.append(kn.reshape(bs, ts, sb_heads, head_dim))
        vs_l.append(vn.reshape(bs, ts, sb_heads, head_dim))
        hs_l.append(hn)
    stack = lambda xs: jnp.stack(xs, axis=0)
    return (yp, ys, stack(kp_l), stack(vp_l), stack(hp_l), stack(mkp_l), stack(mvp_l),
            stack(ks_l), stack(vs_l), stack(hs_l))
```

```python
import functools

import jax
import jax.numpy as jnp
from jax import lax
from jax.experimental import pallas as pl
from jax.experimental.pallas import tpu as pltpu

F32 = jnp.float32
BF16 = jnp.bfloat16

HEAD_DIM = 64
LANES = 128
POOL_WINDOWS = (2, 4, 8, 16)
POOL_STATE = max(POOL_WINDOWS) - 1
HIST_ROWS = 16
EPS = 1e-6
DEAD_CARRY = 104.0
VMEM_LIMIT = 56 * 1024 * 1024


def _rms(x, g):
    ms = jnp.mean(x * x, axis=-1, keepdims=True)
    return (x * lax.rsqrt(ms + EPS)) * g


def _proj_kernel(x_ref, g_ref, w_ref, k_ref, v_ref, qkv_ref, rest_ref, *, sb_width):
    h = _rms(x_ref[...], g_ref[...]).astype(BF16)
    s = sb_width
    q = jnp.dot(h, w_ref[:, 0:s], preferred_element_type=F32)
    qkv_ref[:, 0:s] = (q * (HEAD_DIM ** -0.5)).astype(BF16)
    k = jnp.dot(h, w_ref[:, s:2 * s], preferred_element_type=F32)
    k_ref[...] = k
    qkv_ref[:, s:2 * s] = k.astype(BF16)
    v = jnp.dot(h, w_ref[:, 2 * s:3 * s], preferred_element_type=F32)
    v_ref[...] = v
    qkv_ref[:, 2 * s:3 * s] = v.astype(BF16)
    rest_ref[...] = jnp.dot(h, w_ref[:, 3 * s:], preferred_element_type=F32)


def _project(x2d, g, w_bf16, sb_width, tm):
    m, d = x2d.shape
    n = w_bf16.shape[1]
    n_rest = n - 3 * sb_width
    return pl.pallas_call(
        functools.partial(_proj_kernel, sb_width=sb_width),
        grid=(m // tm,),
        in_specs=[
            pl.BlockSpec((tm, d), lambda i: (i, 0)),
            pl.BlockSpec((1, d), lambda i: (0, 0)),
            pl.BlockSpec((d, n), lambda i: (0, 0)),
        ],
        out_specs=[
            pl.BlockSpec((tm, sb_width), lambda i: (i, 0)),
            pl.BlockSpec((tm, sb_width), lambda i: (i, 0)),
            pl.BlockSpec((tm, 3 * sb_width), lambda i: (i, 0)),
            pl.BlockSpec((tm, n_rest), lambda i: (i, 0)),
        ],
        out_shape=[
            jax.ShapeDtypeStruct((m, sb_width), F32),
            jax.ShapeDtypeStruct((m, sb_width), F32),
            jax.ShapeDtypeStruct((m, 3 * sb_width), BF16),
            jax.ShapeDtypeStruct((m, n_rest), F32),
        ],
        compiler_params=pltpu.CompilerParams(
            dimension_semantics=("arbitrary",), vmem_limit_bytes=VMEM_LIMIT),
        name="proj",
    )(x2d, g, w_bf16)


def _memkv_kernel(x_ref, g_ref, w_ref, mk_ref, mv_ref, *, xa_width):
    h = _rms(x_ref[...], g_ref[...]).astype(BF16)
    kv = jnp.dot(h, w_ref[...], preferred_element_type=F32)
    mk_ref[...] = kv[:, :xa_width]
    mv_ref[...] = kv[:, xa_width:]


def _memory_kv(mem2d, g, w_bf16, tm):
    m, d = mem2d.shape
    xa_width = w_bf16.shape[1] // 2
    return pl.pallas_call(
        functools.partial(_memkv_kernel, xa_width=xa_width),
        grid=(m // tm,),
        in_specs=[
            pl.BlockSpec((tm, d), lambda i: (i, 0)),
            pl.BlockSpec((1, d), lambda i: (0, 0)),
            pl.BlockSpec((d, 2 * xa_width), lambda i: (0, 0)),
        ],
        out_specs=[
            pl.BlockSpec((tm, xa_width), lambda i: (i, 0)),
            pl.BlockSpec((tm, xa_width), lambda i: (i, 0)),
        ],
        out_shape=[
            jax.ShapeDtypeStruct((m, xa_width), F32),
            jax.ShapeDtypeStruct((m, xa_width), F32),
        ],
        compiler_params=pltpu.CompilerParams(
            dimension_semantics=("arbitrary",), vmem_limit_bytes=VMEM_LIMIT),
        name="memkv",
    )(mem2d, g, w_bf16)


def _strict_upper(n):
    r = lax.broadcasted_iota(jnp.int32, (n, n), 0)
    c = lax.broadcasted_iota(jnp.int32, (n, n), 1)
    return jnp.where(r > c, 1.0, 0.0).astype(BF16)


def _softplus(z):
    sign = jnp.uint32(0x80000000)
    neg_abs = lax.bitcast_convert_type(lax.bitcast_convert_type(z, jnp.uint32) | sign, F32)
    return jnp.maximum(z, 0.0) + jnp.log(1.0 + jnp.exp(neg_abs))


def _sum_later(sp, upper):
    hi = sp.astype(BF16)
    lo = (sp - hi.astype(F32)).astype(BF16)
    return jnp.dot(jnp.concatenate([hi, lo], axis=1), jnp.concatenate([upper, upper], axis=0),
                   preferred_element_type=F32)


def _sb_block(qh, kblk, vblk, upper, carry, mask):
    z = lax.dot_general(qh, kblk, (((1,), (1,)), ((), ())), preferred_element_type=F32)
    sp = _softplus(z)
    if mask is not None:
        sp = jnp.where(mask, sp, 0.0)
    later = _sum_later(sp, upper)
    w = jnp.exp((z - sp) - later - carry)
    if mask is not None:
        w = jnp.where(mask, w, 0.0)
    out = jnp.dot(w.astype(BF16), vblk, preferred_element_type=F32)
    return out, carry + jnp.sum(sp, axis=-1, keepdims=True)


def _sb_kernel(q_ref, kd_ref, vd_ref, kp_ref, vp_ref, o_ref, acc0_ref, acc1_ref,
               *, tq, tk, n_past_static):
    q = q_ref[0]
    lane = lax.broadcasted_iota(jnp.int32, (tq, LANES), 1)
    first = lane < HEAD_DIM
    zero = jnp.zeros_like(q)
    q0 = jnp.where(first, q, zero)
    q1 = jnp.where(first, zero, q)

    r = lax.broadcasted_iota(jnp.int32, (tq, tq), 0)
    c = lax.broadcasted_iota(jnp.int32, (tq, tq), 1)
    causal = c < r
    upper_d = _strict_upper(tq)
    kd = kd_ref[0]
    vd = vd_ref[0]
    zc = jnp.zeros((tq, 1), F32)
    o0, c0 = _sb_block(q0, kd, vd, upper_d, zc, causal)
    o1, c1 = _sb_block(q1, kd, vd, upper_d, zc, causal)
    acc0_ref[...] = o0
    acc1_ref[...] = o1

    if n_past_static is None:
        n_past = pl.program_id(2) * (tq // tk)
    else:
        n_past = n_past_static
    upper = _strict_upper(tk)

    def live(state):
        j, c0, c1 = state
        return jnp.logical_and(j < n_past, jnp.min(jnp.minimum(c0, c1)) <= DEAD_CARRY)

    def body(state):
        j, c0, c1 = state
        start = pl.multiple_of((n_past - 1 - j) * tk, tk)
        kblk = kp_ref[0, pl.ds(start, tk), :].astype(BF16)
        vblk = vp_ref[0, pl.ds(start, tk), :].astype(BF16)
        o0, c0 = _sb_block(q0, kblk, vblk, upper, c0, None)
        o1, c1 = _sb_block(q1, kblk, vblk, upper, c1, None)
        acc0_ref[...] += o0
        acc1_ref[...] += o1
        return j + 1, c0, c1

    lax.while_loop(live, body, (jnp.int32(0), c0, c1))
    o_ref[0] = jnp.where(first, acc0_ref[...], acc1_ref[...])


def _sb_window_kernel(q_ref, k_ref, v_ref, o_ref, acc_ref, *, tq):
    i = pl.program_id(2)
    q = q_ref[0]
    lane = lax.broadcasted_iota(jnp.int32, (tq, LANES), 1)
    first = lane < HEAD_DIM
    zero = jnp.zeros_like(q)
    heads = (jnp.where(first, q, zero), jnp.where(first, zero, q))
    r = lax.broadcasted_iota(jnp.int32, (tq, tq), 0)
    c = lax.broadcasted_iota(jnp.int32, (tq, tq), 1)
    causal = c < r
    upper = _strict_upper(tq)
    nt = (((1,), (1,)), ((), ()))

    @pl.when(i == 0)
    def _():
        kd = k_ref[0, 0:tq, :]
        vd = v_ref[0, 0:tq, :]
        zc = jnp.zeros((tq, 1), F32)
        for h, qh in enumerate(heads):
            o, _ = _sb_block(qh, kd, vd, upper, zc, causal)
            acc_ref[h] = o

    @pl.when(i > 0)
    def _():
        ws = pl.multiple_of((i - 1) * tq, tq)
        kwin = k_ref[0, pl.ds(ws, 2 * tq), :]
        vwin = v_ref[0, pl.ds(ws, 2 * tq), :]
        zs = [lax.dot_general(qh, kwin, nt, preferred_element_type=F32) for qh in heads]
        zp = [z[:, :tq] for z in zs]
        zd = [z[:, tq:] for z in zs]
        sp_p = [_softplus(z) for z in zp]
        sp_d = [jnp.where(causal, _softplus(z), 0.0) for z in zd]
        later = _sum_later(jnp.concatenate(sp_p + sp_d, axis=0), upper)
        n = len(heads)
        carries = []
        for h in range(n):
            rs_d = jnp.sum(sp_d[h], axis=-1, keepdims=True)
            later_p = later[h * tq:(h + 1) * tq]
            later_d = later[(n + h) * tq:(n + h + 1) * tq]
            w_p = jnp.exp((zp[h] - sp_p[h]) - later_p - rs_d)
            w_d = jnp.where(causal, jnp.exp((zd[h] - sp_d[h]) - later_d), 0.0)
            w = jnp.concatenate([w_p.astype(BF16), w_d.astype(BF16)], axis=1)
            acc_ref[h] = jnp.dot(w, vwin, preferred_element_type=F32)
            carries.append(rs_d + jnp.sum(sp_p[h], axis=-1, keepdims=True))

        n_older = i - 1

        def live(state):
            j, c0, c1 = state
            return jnp.logical_and(j < n_older, jnp.min(jnp.minimum(c0, c1)) <= DEAD_CARRY)

        def body(state):
            j, c0, c1 = state
            start = pl.multiple_of((n_older - 1 - j) * tq, tq)
            kblk = k_ref[0, pl.ds(start, tq), :]
            vblk = v_ref[0, pl.ds(start, tq), :]
            o0, c0 = _sb_block(heads[0], kblk, vblk, upper, c0, None)
            o1, c1 = _sb_block(heads[1], kblk, vblk, upper, c1, None)
            acc_ref[0] += o0
            acc_ref[1] += o1
            return j + 1, c0, c1

        lax.while_loop(live, body, (jnp.int32(0), carries[0], carries[1]))

    o_ref[0] = jnp.where(first, acc_ref[0], acc_ref[1])


def _stick_breaking_self(qkv, *, tq, sb_width):
    b, t, _ = qkv.shape
    pairs = sb_width // LANES
    return pl.pallas_call(
        functools.partial(_sb_window_kernel, tq=tq),
        grid=(b, pairs, t // tq),
        in_specs=[
            pl.BlockSpec((1, tq, LANES), lambda bi, hp, i: (bi, i, hp)),
            pl.BlockSpec((1, t, LANES), lambda bi, hp, i: (bi, 0, pairs + hp)),
            pl.BlockSpec((1, t, LANES), lambda bi, hp, i: (bi, 0, 2 * pairs + hp)),
        ],
        out_specs=pl.BlockSpec((1, tq, LANES), lambda bi, hp, i: (bi, i, hp)),
        out_shape=jax.ShapeDtypeStruct((b, t, sb_width), F32),
        scratch_shapes=[pltpu.VMEM((2, tq, LANES), F32)],
        compiler_params=pltpu.CompilerParams(
            dimension_semantics=("arbitrary", "arbitrary", "arbitrary"),
            vmem_limit_bytes=VMEM_LIMIT),
        name="stick_breaking_self",
    )(qkv, qkv, qkv)


def _stick_breaking(qkv, k_past, v_past, *, tq, tk, sb_width):
    b, t, _ = qkv.shape
    pairs = sb_width // LANES
    if k_past is None:
        k_past, v_past = qkv, qkv
        k_off, v_off = pairs, 2 * pairs
        n_past_static = None
    else:
        k_off, v_off = 0, 0
        n_past_static = k_past.shape[1] // tk
    p = k_past.shape[1]
    return pl.pallas_call(
        functools.partial(_sb_kernel, tq=tq, tk=tk, n_past_static=n_past_static),
        grid=(b, pairs, t // tq),
        in_specs=[
            pl.BlockSpec((1, tq, LANES), lambda bi, hp, i: (bi, i, hp)),
            pl.BlockSpec((1, tq, LANES), lambda bi, hp, i: (bi, i, pairs + hp)),
            pl.BlockSpec((1, tq, LANES), lambda bi, hp, i: (bi, i, 2 * pairs + hp)),
            pl.BlockSpec((1, p, LANES), lambda bi, hp, i: (bi, 0, k_off + hp)),
            pl.BlockSpec((1, p, LANES), lambda bi, hp, i: (bi, 0, v_off + hp)),
        ],
        out_specs=pl.BlockSpec((1, tq, LANES), lambda bi, hp, i: (bi, i, hp)),
        out_shape=jax.ShapeDtypeStruct((b, t, sb_width), F32),
        scratch_shapes=[pltpu.VMEM((tq, LANES), F32), pltpu.VMEM((tq, LANES), F32)],
        compiler_params=pltpu.CompilerParams(
            dimension_semantics=("arbitrary", "arbitrary", "arbitrary"),
            vmem_limit_bytes=VMEM_LIMIT),
        name="stick_breaking",
    )(qkv, qkv, qkv, k_past, v_past)


def _silu(g):
    return g / (1.0 + jnp.exp(-g))


def _epilogue_kernel(x_ref, osb_ref, rest_ref, prev_ref, hist0_ref, mk_ref, mv_ref,
                     poolw_ref, pscale_ref, wout_ref, gfin_ref, y_ref,
                     *, tt, start, sb_width, pool_width, xa_width, xa_heads, apply_final):
    i = pl.program_id(1)
    s, pw, xw = sb_width, pool_width, xa_width
    g_sb = rest_ref[0, :, 0:s]
    u = rest_ref[0, :, s:s + pw]
    g_pool = rest_ref[0, :, s + pw:s + 2 * pw]
    q_xa = rest_ref[0, :, s + 2 * pw:s + 2 * pw + xw]
    g_xa = rest_ref[0, :, s + 2 * pw + xw:s + 2 * pw + 2 * xw]

    hist = jnp.where(i == 0, hist0_ref[0], prev_ref[0])
    ext = jnp.concatenate([hist, u], axis=0)
    sums = {1: ext}
    width = 1
    while width < max(POOL_WINDOWS):
        prev = sums[width]
        sums[2 * width] = prev + pltpu.roll(prev, width, 0)
        width *= 2
    lane = lax.broadcasted_iota(jnp.int32, (tt, pw), 1)
    group = lane // (pw // len(POOL_WINDOWS))
    pos = start + i * tt + lax.broadcasted_iota(jnp.int32, (tt, pw), 0)
    win_sum = jnp.zeros((tt, pw), F32)
    win = jnp.zeros((tt, pw), jnp.int32)
    for gi, w in enumerate(POOL_WINDOWS):
        sel = group == gi
        win_sum = jnp.where(sel, sums[w][HIST_ROWS:], win_sum)
        win = jnp.where(sel, w, win)
    cnt = jnp.minimum(pos + 1, win).astype(F32)
    pooled = win_sum / cnt - u
    o_pool = jnp.dot(pooled.astype(BF16), poolw_ref[...],
                     preferred_element_type=F32) * pscale_ref[...]

    mk = mk_ref[0].astype(BF16)
    mv = mv_ref[0].astype(BF16)
    xa_dim = xw // xa_heads
    lane_x = lax.broadcasted_iota(jnp.int32, (tt, xw), 1)
    head_x = lane_x // xa_dim
    qs = q_xa * (xa_dim ** -0.5)
    o_xa = jnp.zeros((tt, xw), F32)
    for hh in range(xa_heads):
        sel = head_x == hh
        qh = jnp.where(sel, qs, 0.0).astype(BF16)
        sc = lax.dot_general(qh, mk, (((1,), (1,)), ((), ())), preferred_element_type=F32)
        e = jnp.exp(sc - jnp.max(sc, axis=-1, keepdims=True))
        p = e / jnp.sum(e, axis=-1, keepdims=True)
        oh = jnp.dot(p.astype(BF16), mv, preferred_element_type=F32)
        o_xa = jnp.where(sel, oh, o_xa)

    m_sb = (osb_ref[0] * _silu(g_sb)).astype(BF16)
    m_pool = (o_pool * _silu(g_pool)).astype(BF16)
    m_xa = (o_xa * _silu(g_xa)).astype(BF16)
    y = (x_ref[0]
         + jnp.dot(m_sb, wout_ref[0:s, :], preferred_element_type=F32)
         + jnp.dot(m_pool, wout_ref[s:s + pw, :], preferred_element_type=F32)
         + jnp.dot(m_xa, wout_ref[s + pw:s + pw + xw, :], preferred_element_type=F32))
    if apply_final:
        y = _rms(y, gfin_ref[...])
    y_ref[0] = y


def _epilogue(x, o_sb, rest, hist0, mk, mv, pool_bd, pool_scale, w_out_bf16, g_final,
              *, tt, start, sb_width, pool_width, xa_width, xa_heads, apply_final):
    b, t, d = x.shape
    n_rest = rest.shape[-1]
    n_mem = mk.shape[1]
    hist_blocks = tt // HIST_ROWS
    u_col = sb_width // pool_width
    kern = functools.partial(
        _epilogue_kernel, tt=tt, start=start, sb_width=sb_width, pool_width=pool_width,
        xa_width=xa_width, xa_heads=xa_heads, apply_final=apply_final)
    return pl.pallas_call(
        kern,
        grid=(b, t // tt),
        in_specs=[
            pl.BlockSpec((1, tt, d), lambda bi, i: (bi, i, 0)),
            pl.BlockSpec((1, tt, sb_width), lambda bi, i: (bi, i, 0)),
            pl.BlockSpec((1, tt, n_rest), lambda bi, i: (bi, i, 0)),
            pl.BlockSpec((1, HIST_ROWS, pool_width),
                         lambda bi, i: (bi, jnp.maximum(i * hist_blocks - 1, 0), u_col)),
            pl.BlockSpec((1, HIST_ROWS, pool_width), lambda bi, i: (bi, 0, 0)),
            pl.BlockSpec((1, n_mem, xa_width), lambda bi, i: (bi, 0, 0)),
            pl.BlockSpec((1, n_mem, xa_width), lambda bi, i: (bi, 0, 0)),
            pl.BlockSpec((pool_width, pool_width), lambda bi, i: (0, 0)),
            pl.BlockSpec((1, pool_width), lambda bi, i: (0, 0)),
            pl.BlockSpec((d, d), lambda bi, i: (0, 0)),
            pl.BlockSpec((1, d), lambda bi, i: (0, 0)),
        ],
        out_specs=pl.BlockSpec((1, tt, d), lambda bi, i: (bi, i, 0)),
        out_shape=jax.ShapeDtypeStruct((b, t, d), F32),
        compiler_params=pltpu.CompilerParams(
            dimension_semantics=("arbitrary", "arbitrary"), vmem_limit_bytes=VMEM_LIMIT),
        name="epilogue",
    )(x, o_sb, rest, rest, hist0, mk, mv, pool_bd, pool_scale, w_out_bf16, g_final)


def _block_diag(pool_w):
    g, c, _ = pool_w.shape
    eye = jnp.eye(g, dtype=pool_w.dtype)
    return (eye[:, None, :, None] * pool_w[:, :, None, :]).reshape(g * c, g * c)


def _pad_hist(hist):
    return jnp.pad(hist, ((0, 0), (HIST_ROWS - hist.shape[1], 0), (0, 0)))


def _mixer_layer(x, start, k_past, v_past, pool_hist, mk, mv, g_norm, w_in_bf16, pool_bd,
                 pool_scale, w_out_bf16, g_final, *, dims, tiles, apply_final):
    sb_width, pool_width, xa_width, xa_heads = dims
    tm, tq, tk, tt = tiles
    b, t, d = x.shape
    k2d, v2d, qkv, rest = _project(x.reshape(b * t, d), g_norm, w_in_bf16, sb_width, tm)
    qkv = qkv.reshape(b, t, -1)
    rest = rest.reshape(b, t, -1)
    if k_past is None:
        o_sb = _stick_breaking_self(qkv, tq=tq, sb_width=sb_width)
    else:
        o_sb = _stick_breaking(qkv, k_past, v_past, tq=tq, tk=tk, sb_width=sb_width)
    y = _epilogue(x, o_sb, rest, _pad_hist(pool_hist), mk, mv, pool_bd, pool_scale,
                  w_out_bf16, g_final, tt=tt, start=start, sb_width=sb_width,
                  pool_width=pool_width, xa_width=xa_width, xa_heads=xa_heads,
                  apply_final=apply_final)
    u_pool = rest[:, :, sb_width:sb_width + pool_width]
    new_hist = jnp.concatenate([pool_hist, u_pool], axis=1)[:, -POOL_STATE:]
    return y, k2d, v2d, new_hist


def kernel(x_prompt, x_sample, cache_sb_k, cache_sb_v, state_pool, cache_mem_k, cache_mem_v,
           mem_prompt, g_norm, w_in, pool_w, pool_scale, g_mem, w_mem_kv, w_out, g_final):
    depth = g_norm.shape[0]
    bp, tp, d = x_prompt.shape
    bs, ts, _ = x_sample.shape
    past = cache_sb_k.shape[2]
    sb_heads, head_dim = cache_sb_k.shape[3], cache_sb_k.shape[4]
    assert head_dim == HEAD_DIM
    sb_width = sb_heads * head_dim
    pool_width = pool_w.shape[1] * pool_w.shape[2]
    n_mem, xa_heads, xa_dim = cache_mem_k.shape[2:]
    xa_width = xa_heads * xa_dim
    dims = (sb_width, pool_width, xa_width, xa_heads)
    g_fin = g_final.reshape(1, d)

    yp, ys = x_prompt, x_sample
    kp_l, vp_l, hp_l, mkp_l, mvp_l, ks_l, vs_l, hs_l = [], [], [], [], [], [], [], []
    for l in range(depth):
        final = l == depth - 1
        g_l = g_norm[l].reshape(1, d)
        w_in_b = w_in[l].astype(BF16)
        w_out_b = w_out[l].astype(BF16)
        pool_bd = _block_diag(pool_w[l]).astype(BF16)
        p_scale = pool_scale[l].reshape(1, pool_width)
        mk, mv = _memory_kv(mem_prompt.reshape(bp * n_mem, d), g_mem[l].reshape(1, d),
                            w_mem_kv[l].astype(BF16), tm=256)
        mk = mk.reshape(bp, n_mem, xa_width)
        mv = mv.reshape(bp, n_mem, xa_width)
        hist0 = jnp.zeros((bp, POOL_STATE, pool_width), x_prompt.dtype)
        yp, kp, vp, hp = _mixer_layer(
            yp, 0, None, None, hist0, mk, mv, g_l, w_in_b, pool_bd, p_scale, w_out_b, g_fin,
            dims=dims, tiles=(512, 256, 256, 256), apply_final=final)
        ys, kn, vn, hn = _mixer_layer(
            ys, past, cache_sb_k[l].reshape(bs, past, sb_width),
            cache_sb_v[l].reshape(bs, past, sb_width), state_pool[l],
            cache_mem_k[l].reshape(bs, n_mem, xa_width),
            cache_mem_v[l].reshape(bs, n_mem, xa_width),
            g_l, w_in_b, pool_bd, p_scale, w_out_b, g_fin,
            dims=dims, tiles=(512, ts, 256, ts), apply_final=final)
        kp_l.append(kp.reshape(bp, tp, sb_heads, head_dim))
        vp_l.append(vp.reshape(bp, tp, sb_heads, head_dim))
        hp_l.append(hp)
        mkp_l.append(mk.reshape(bp, n_mem, xa_heads, xa_dim))
        mvp_l.append(mv.reshape(bp, n_mem, xa_heads, xa_dim))
        ks_l.append(kn.reshape(bs, ts, sb_heads, head_dim))
        vs_l.append(vn.reshape(bs, ts, sb_heads, head_dim))
        hs_l.append(hn)
    stack = lambda xs: jnp.stack(xs, axis=0)
    return (yp, ys, stack(kp_l), stack(vp_l), stack(hp_l), stack(mkp_l), stack(mvp_l),
            stack(ks_l), stack(vs_l), stack(hs_l))
```

```python
import functools

import jax
import jax.numpy as jnp
from jax import lax
from jax.experimental import pallas as pl
from jax.experimental.pallas import tpu as pltpu

F32 = jnp.float32
BF16 = jnp.bfloat16

HEAD_DIM = 64
LANES = 128
POOL_WINDOWS = (2, 4, 8, 16)
POOL_STATE = max(POOL_WINDOWS) - 1
HIST_ROWS = 16
EPS = 1e-6
DEAD_CARRY = 104.0
VMEM_LIMIT = 56 * 1024 * 1024


def _rms(x, g):
    ms = jnp.mean(x * x, axis=-1, keepdims=True)
    return (x * lax.rsqrt(ms + EPS)) * g


def _proj_kernel(x_ref, g_ref, w_ref, k_ref, v_ref, qkv_ref, rest_ref, *, sb_width):
    h = _rms(x_ref[...], g_ref[...]).astype(BF16)
    s = sb_width
    q = jnp.dot(h, w_ref[:, 0:s], preferred_element_type=F32)
    qkv_ref[:, 0:s] = (q * (HEAD_DIM ** -0.5)).astype(BF16)
    k = jnp.dot(h, w_ref[:, s:2 * s], preferred_element_type=F32)
    k_ref[...] = k
    qkv_ref[:, s:2 * s] = k.astype(BF16)
    v = jnp.dot(h, w_ref[:, 2 * s:3 * s], preferred_element_type=F32)
    v_ref[...] = v
    qkv_ref[:, 2 * s:3 * s] = v.astype(BF16)
    rest_ref[...] = jnp.dot(h, w_ref[:, 3 * s:], preferred_element_type=F32)


def _project(x2d, g, w_bf16, sb_width, tm):
    m, d = x2d.shape
    n = w_bf16.shape[1]
    n_rest = n - 3 * sb_width
    return pl.pallas_call(
        functools.partial(_proj_kernel, sb_width=sb_width),
        grid=(m // tm,),
        in_specs=[
            pl.BlockSpec((tm, d), lambda i: (i, 0)),
            pl.BlockSpec((1, d), lambda i: (0, 0)),
            pl.BlockSpec((d, n), lambda i: (0, 0)),
        ],
        out_specs=[
            pl.BlockSpec((tm, sb_width), lambda i: (i, 0)),
            pl.BlockSpec((tm, sb_width), lambda i: (i, 0)),
            pl.BlockSpec((tm, 3 * sb_width), lambda i: (i, 0)),
            pl.BlockSpec((tm, n_rest), lambda i: (i, 0)),
        ],
        out_shape=[
            jax.ShapeDtypeStruct((m, sb_width), F32),
            jax.ShapeDtypeStruct((m, sb_width), F32),
            jax.ShapeDtypeStruct((m, 3 * sb_width), BF16),
            jax.ShapeDtypeStruct((m, n_rest), F32),
        ],
        compiler_params=pltpu.CompilerParams(
            dimension_semantics=("arbitrary",), vmem_limit_bytes=VMEM_LIMIT),
        name="proj",
    )(x2d, g, w_bf16)


def _memkv_kernel(x_ref, g_ref, w_ref, mk_ref, mv_ref, *, xa_width):
    h = _rms(x_ref[...], g_ref[...]).astype(BF16)
    kv = jnp.dot(h, w_ref[...], preferred_element_type=F32)
    mk_ref[...] = kv[:, :xa_width]
    mv_ref[...] = kv[:, xa_width:]


def _memory_kv(mem2d, g, w_bf16, tm):
    m, d = mem2d.shape
    xa_width = w_bf16.shape[1] // 2
    return pl.pallas_call(
        functools.partial(_memkv_kernel, xa_width=xa_width),
        grid=(m // tm,),
        in_specs=[
            pl.BlockSpec((tm, d), lambda i: (i, 0)),
            pl.BlockSpec((1, d), lambda i: (0, 0)),
            pl.BlockSpec((d, 2 * xa_width), lambda i: (0, 0)),
        ],
        out_specs=[
            pl.BlockSpec((tm, xa_width), lambda i: (i, 0)),
            pl.BlockSpec((tm, xa_width), lambda i: (i, 0)),
        ],
        out_shape=[
            jax.ShapeDtypeStruct((m, xa_width), F32),
            jax.ShapeDtypeStruct((m, xa_width), F32),
        ],
        compiler_params=pltpu.CompilerParams(
            dimension_semantics=("arbitrary",), vmem_limit_bytes=VMEM_LIMIT),
        name="memkv",
    )(mem2d, g, w_bf16)


def _strict_upper(n):
    r = lax.broadcasted_iota(jnp.int32, (n, n), 0)
    c = lax.broadcasted_iota(jnp.int32, (n, n), 1)
    return jnp.where(r > c, 1.0, 0.0).astype(BF16)


def _softplus(z):
    return jnp.maximum(z, 0.0) + jnp.log(1.0 + jnp.exp(-jnp.abs(z)))


def _sum_later(sp, upper):
    hi = sp.astype(BF16)
    lo = (sp - hi.astype(F32)).astype(BF16)
    return jnp.dot(jnp.concatenate([hi, lo], axis=1), jnp.concatenate([upper, upper], axis=0),
                   preferred_element_type=F32)


def _sb_block(qh, kblk, vblk, upper, carry, mask):
    z = lax.dot_general(qh, kblk, (((1,), (1,)), ((), ())), preferred_element_type=F32)
    sp = _softplus(z)
    if mask is not None:
        sp = jnp.where(mask, sp, 0.0)
    later = _sum_later(sp, upper)
    w = jnp.exp((z - sp) - later - carry)
    if mask is not None:
        w = jnp.where(mask, w, 0.0)
    out = jnp.dot(w.astype(BF16), vblk, preferred_element_type=F32)
    return out, carry + jnp.sum(sp, axis=-1, keepdims=True)


def _sb_kernel(q_ref, kd_ref, vd_ref, kp_ref, vp_ref, o_ref, acc0_ref, acc1_ref,
               *, tq, tk, n_past_static):
    q = q_ref[0]
    lane = lax.broadcasted_iota(jnp.int32, (tq, LANES), 1)
    first = lane < HEAD_DIM
    zero = jnp.zeros_like(q)
    q0 = jnp.where(first, q, zero)
    q1 = jnp.where(first, zero, q)

    r = lax.broadcasted_iota(jnp.int32, (tq, tq), 0)
    c = lax.broadcasted_iota(jnp.int32, (tq, tq), 1)
    causal = c < r
    upper_d = _strict_upper(tq)
    kd = kd_ref[0]
    vd = vd_ref[0]
    zc = jnp.zeros((tq, 1), F32)
    o0, c0 = _sb_block(q0, kd, vd, upper_d, zc, causal)
    o1, c1 = _sb_block(q1, kd, vd, upper_d, zc, causal)
    acc0_ref[...] = o0
    acc1_ref[...] = o1

    if n_past_static is None:
        n_past = pl.program_id(2) * (tq // tk)
    else:
        n_past = n_past_static
    upper = _strict_upper(tk)

    def live(state):
        j, c0, c1 = state
        return jnp.logical_and(j < n_past, jnp.min(jnp.minimum(c0, c1)) <= DEAD_CARRY)

    def body(state):
        j, c0, c1 = state
        start = pl.multiple_of((n_past - 1 - j) * tk, tk)
        kblk = kp_ref[0, pl.ds(start, tk), :].astype(BF16)
        vblk = vp_ref[0, pl.ds(start, tk), :].astype(BF16)
        o0, c0 = _sb_block(q0, kblk, vblk, upper, c0, None)
        o1, c1 = _sb_block(q1, kblk, vblk, upper, c1, None)
        acc0_ref[...] += o0
        acc1_ref[...] += o1
        return j + 1, c0, c1

    lax.while_loop(live, body, (jnp.int32(0), c0, c1))
    o_ref[0] = jnp.where(first, acc0_ref[...], acc1_ref[...])


def _sb_window_kernel(q_ref, k_ref, v_ref, o_ref, acc_ref, *, tq):
    i = pl.program_id(2)
    q = q_ref[0]
    lane = lax.broadcasted_iota(jnp.int32, (tq, LANES), 1)
    first = lane < HEAD_DIM
    zero = jnp.zeros_like(q)
    heads = (jnp.where(first, q, zero), jnp.where(first, zero, q))
    r = lax.broadcasted_iota(jnp.int32, (tq, tq), 0)
    c = lax.broadcasted_iota(jnp.int32, (tq, tq), 1)
    causal = c < r
    upper = _strict_upper(tq)
    nt = (((1,), (1,)), ((), ()))

    @pl.when(i == 0)
    def _():
        kd = k_ref[0, 0:tq, :]
        vd = v_ref[0, 0:tq, :]
        zc = jnp.zeros((tq, 1), F32)
        for h, qh in enumerate(heads):
            o, _ = _sb_block(qh, kd, vd, upper, zc, causal)
            acc_ref[h] = o

    @pl.when(i > 0)
    def _():
        ws = pl.multiple_of((i - 1) * tq, tq)
        kwin = k_ref[0, pl.ds(ws, 2 * tq), :]
        vwin = v_ref[0, pl.ds(ws, 2 * tq), :]
        zs = [lax.dot_general(qh, kwin, nt, preferred_element_type=F32) for qh in heads]
        zp = [z[:, :tq] for z in zs]
        zd = [z[:, tq:] for z in zs]
        sp_p = [_softplus(z) for z in zp]
        sp_d = [jnp.where(causal, _softplus(z), 0.0) for z in zd]
        later = _sum_later(jnp.concatenate(sp_p + sp_d, axis=0), upper)
        n = len(heads)
        carries = []
        for h in range(n):
            rs_d = jnp.sum(sp_d[h], axis=-1, keepdims=True)
            later_p = later[h * tq:(h + 1) * tq]
            later_d = later[(n + h) * tq:(n + h + 1) * tq]
            w_p = jnp.exp((zp[h] - sp_p[h]) - later_p - rs_d)
            w_d = jnp.where(causal, jnp.exp((zd[h] - sp_d[h]) - later_d), 0.0)
            w = jnp.concatenate([w_p.astype(BF16), w_d.astype(BF16)], axis=1)
            acc_ref[h] = jnp.dot(w, vwin, preferred_element_type=F32)
            carries.append(rs_d + jnp.sum(sp_p[h], axis=-1, keepdims=True))

        n_older = i - 1

        def live(state):
            j, c0, c1 = state
            return jnp.logical_and(j < n_older, jnp.min(jnp.minimum(c0, c1)) <= DEAD_CARRY)

        def body(state):
            j, c0, c1 = state
            start = pl.multiple_of((n_older - 1 - j) * tq, tq)
            kblk = k_ref[0, pl.ds(start, tq), :]
            vblk = v_ref[0, pl.ds(start, tq), :]
            o0, c0 = _sb_block(heads[0], kblk, vblk, upper, c0, None)
            o1, c1 = _sb_block(heads[1], kblk, vblk, upper, c1, None)
            acc_ref[0] += o0
            acc_ref[1] += o1
            return j + 1, c0, c1

        lax.while_loop(live, body, (jnp.int32(0), carries[0], carries[1]))

    o_ref[0] = jnp.where(first, acc_ref[0], acc_ref[1])


def _stick_breaking_self(qkv, *, tq, sb_width):
    b, t, _ = qkv.shape
    pairs = sb_width // LANES
    return pl.pallas_call(
        functools.partial(_sb_window_kernel, tq=tq),
        grid=(b, pairs, t // tq),
        in_specs=[
            pl.BlockSpec((1, tq, LANES), lambda bi, hp, i: (bi, i, hp)),
            pl.BlockSpec((1, t, LANES), lambda bi, hp, i: (bi, 0, pairs + hp)),
            pl.BlockSpec((1, t, LANES), lambda bi, hp, i: (bi, 0, 2 * pairs + hp)),
        ],
        out_specs=pl.BlockSpec((1, tq, LANES), lambda bi, hp, i: (bi, i, hp)),
        out_shape=jax.ShapeDtypeStruct((b, t, sb_width), F32),
        scratch_shapes=[pltpu.VMEM((2, tq, LANES), F32)],
        compiler_params=pltpu.CompilerParams(
            dimension_semantics=("arbitrary", "arbitrary", "arbitrary"),
            vmem_limit_bytes=VMEM_LIMIT),
        name="stick_breaking_self",
    )(qkv, qkv, qkv)


def _stick_breaking(qkv, k_past, v_past, *, tq, tk, sb_width):
    b, t, _ = qkv.shape
    pairs = sb_width // LANES
    if k_past is None:
        k_past, v_past = qkv, qkv
        k_off, v_off = pairs, 2 * pairs
        n_past_static = None
    else:
        k_off, v_off = 0, 0
        n_past_static = k_past.shape[1] // tk
    p = k_past.shape[1]
    return pl.pallas_call(
        functools.partial(_sb_kernel, tq=tq, tk=tk, n_past_static=n_past_static),
        grid=(b, pairs, t // tq),
        in_specs=[
            pl.BlockSpec((1, tq, LANES), lambda bi, hp, i: (bi, i, hp)),
            pl.BlockSpec((1, tq, LANES), lambda bi, hp, i: (bi, i, pairs + hp)),
            pl.BlockSpec((1, tq, LANES), lambda bi, hp, i: (bi, i, 2 * pairs + hp)),
            pl.BlockSpec((1, p, LANES), lambda bi, hp, i: (bi, 0, k_off + hp)),
            pl.BlockSpec((1, p, LANES), lambda bi, hp, i: (bi, 0, v_off + hp)),
        ],
        out_specs=pl.BlockSpec((1, tq, LANES), lambda bi, hp, i: (bi, i, hp)),
        out_shape=jax.ShapeDtypeStruct((b, t, sb_width), F32),
        scratch_shapes=[pltpu.VMEM((tq, LANES), F32), pltpu.VMEM((tq, LANES), F32)],
        compiler_params=pltpu.CompilerParams(
            dimension_semantics=("arbitrary", "arbitrary", "arbitrary"),
            vmem_limit_bytes=VMEM_LIMIT),
        name="stick_breaking",
    )(qkv, qkv, qkv, k_past, v_past)


def _silu(g):
    return g / (1.0 + jnp.exp(-g))


def _epilogue_kernel(x_ref, osb_ref, rest_ref, prev_ref, hist0_ref, mk_ref, mv_ref,
                     poolw_ref, pscale_ref, wout_ref, gfin_ref, y_ref,
                     *, tt, start, sb_width, pool_width, xa_width, xa_heads, apply_final):
    i = pl.program_id(1)
    s, pw, xw = sb_width, pool_width, xa_width
    g_sb = rest_ref[0, :, 0:s]
    u = rest_ref[0, :, s:s + pw]
    g_pool = rest_ref[0, :, s + pw:s + 2 * pw]
    q_xa = rest_ref[0, :, s + 2 * pw:s + 2 * pw + xw]
    g_xa = rest_ref[0, :, s + 2 * pw + xw:s + 2 * pw + 2 * xw]

    hist = jnp.where(i == 0, hist0_ref[0], prev_ref[0])
    ext = jnp.concatenate([hist, u], axis=0)
    sums = {1: ext}
    width = 1
    while width < max(POOL_WINDOWS):
        prev = sums[width]
        sums[2 * width] = prev + pltpu.roll(prev, width, 0)
        width *= 2
    lane = lax.broadcasted_iota(jnp.int32, (tt, pw), 1)
    group = lane // (pw // len(POOL_WINDOWS))
    pos = start + i * tt + lax.broadcasted_iota(jnp.int32, (tt, pw), 0)
    win_sum = jnp.zeros((tt, pw), F32)
    win = jnp.zeros((tt, pw), jnp.int32)
    for gi, w in enumerate(POOL_WINDOWS):
        sel = group == gi
        win_sum = jnp.where(sel, sums[w][HIST_ROWS:], win_sum)
        win = jnp.where(sel, w, win)
    cnt = jnp.minimum(pos + 1, win).astype(F32)
    pooled = win_sum / cnt - u
    o_pool = jnp.dot(pooled.astype(BF16), poolw_ref[...],
                     preferred_element_type=F32) * pscale_ref[...]

    mk = mk_ref[0].astype(BF16)
    mv = mv_ref[0].astype(BF16)
    xa_dim = xw // xa_heads
    lane_x = lax.broadcasted_iota(jnp.int32, (tt, xw), 1)
    head_x = lane_x // xa_dim
    qs = q_xa * (xa_dim ** -0.5)
    o_xa = jnp.zeros((tt, xw), F32)
    for hh in range(xa_heads):
        sel = head_x == hh
        qh = jnp.where(sel, qs, 0.0).astype(BF16)
        sc = lax.dot_general(qh, mk, (((1,), (1,)), ((), ())), preferred_element_type=F32)
        e = jnp.exp(sc - jnp.max(sc, axis=-1, keepdims=True))
        p = e / jnp.sum(e, axis=-1, keepdims=True)
        oh = jnp.dot(p.astype(BF16), mv, preferred_element_type=F32)
        o_xa = jnp.where(sel, oh, o_xa)

    m_sb = (osb_ref[0] * _silu(g_sb)).astype(BF16)
    m_pool = (o_pool * _silu(g_pool)).astype(BF16)
    m_xa = (o_xa * _silu(g_xa)).astype(BF16)
    y = (x_ref[0]
         + jnp.dot(m_sb, wout_ref[0:s, :], preferred_element_type=F32)
         + jnp.dot(m_pool, wout_ref[s:s + pw, :], preferred_element_type=F32)
         + jnp.dot(m_xa, wout_ref[s + pw:s + pw + xw, :], preferred_element_type=F32))
    if apply_final:
        y = _rms(y, gfin_ref[...])
    y_ref[0] = y


def _epilogue(x, o_sb, rest, hist0, mk, mv, pool_bd, pool_scale, w_out_bf16, g_final,
              *, tt, start, sb_width, pool_width, xa_width, xa_heads, apply_final):
    b, t, d = x.shape
    n_rest = rest.shape[-1]
    n_mem = mk.shape[1]
    hist_blocks = tt // HIST_ROWS
    u_col = sb_width // pool_width
    kern = functools.partial(
        _epilogue_kernel, tt=tt, start=start, sb_width=sb_width, pool_width=pool_width,
        xa_width=xa_width, xa_heads=xa_heads, apply_final=apply_final)
    return pl.pallas_call(
        kern,
        grid=(b, t // tt),
        in_specs=[
            pl.BlockSpec((1, tt, d), lambda bi, i: (bi, i, 0)),
            pl.BlockSpec((1, tt, sb_width), lambda bi, i: (bi, i, 0)),
            pl.BlockSpec((1, tt, n_rest), lambda bi, i: (bi, i, 0)),
            pl.BlockSpec((1, HIST_ROWS, pool_width),
                         lambda bi, i: (bi, jnp.maximum(i * hist_blocks - 1, 0), u_col)),
            pl.BlockSpec((1, HIST_ROWS, pool_width), lambda bi, i: (bi, 0, 0)),
            pl.BlockSpec((1, n_mem, xa_width), lambda bi, i: (bi, 0, 0)),
            pl.BlockSpec((1, n_mem, xa_width), lambda bi, i: (bi, 0, 0)),
            pl.BlockSpec((pool_width, pool_width), lambda bi, i: (0, 0)),
            pl.BlockSpec((1, pool_width), lambda bi, i: (0, 0)),
            pl.BlockSpec((d, d), lambda bi, i: (0, 0)),
            pl.BlockSpec((1, d), lambda bi, i: (0, 0)),
        ],
        out_specs=pl.BlockSpec((1, tt, d), lambda bi, i: (bi, i, 0)),
        out_shape=jax.ShapeDtypeStruct((b, t, d), F32),
        compiler_params=pltpu.CompilerParams(
            dimension_semantics=("arbitrary", "arbitrary"), vmem_limit_bytes=VMEM_LIMIT),
        name="epilogue",
    )(x, o_sb, rest, rest, hist0, mk, mv, pool_bd, pool_scale, w_out_bf16, g_final)


def _block_diag(pool_w):
    g, c, _ = pool_w.shape
    eye = jnp.eye(g, dtype=pool_w.dtype)
    return (eye[:, None, :, None] * pool_w[:, :, None, :]).reshape(g * c, g * c)


def _pad_hist(hist):
    return jnp.pad(hist, ((0, 0), (HIST_ROWS - hist.shape[1], 0), (0, 0)))


def _mixer_layer(x, start, k_past, v_past, pool_hist, mk, mv, g_norm, w_in_bf16, pool_bd,
                 pool_scale, w_out_bf16, g_final, *, dims, tiles, apply_final):
    sb_width, pool_width, xa_width, xa_heads = dims
    tm, tq, tk, tt = tiles
    b, t, d = x.shape
    k2d, v2d, qkv, rest = _project(x.reshape(b * t, d), g_norm, w_in_bf16, sb_width, tm)
    qkv = qkv.reshape(b, t, -1)
    rest = rest.reshape(b, t, -1)
    if k_past is None:
        o_sb = _stick_breaking_self(qkv, tq=tq, sb_width=sb_width)
    else:
        o_sb = _stick_breaking(qkv, k_past, v_past, tq=tq, tk=tk, sb_width=sb_width)
    y = _epilogue(x, o_sb, rest, _pad_hist(pool_hist), mk, mv, pool_bd, pool_scale,
                  w_out_bf16, g_final, tt=tt, start=start, sb_width=sb_width,
                  pool_width=pool_width, xa_width=xa_width, xa_heads=xa_heads,
                  apply_final=apply_final)
    u_pool = rest[:, :, sb_width:sb_width + pool_width]
    new_hist = jnp.concatenate([pool_hist, u_pool], axis=1)[:, -POOL_STATE:]
    return y, k2d, v2d, new_hist


def kernel(x_prompt, x_sample, cache_sb_k, cache_sb_v, state_pool, cache_mem_k, cache_mem_v,
           mem_prompt, g_norm, w_in, pool_w, pool_scale, g_mem, w_mem_kv, w_out, g_final):
    depth = g_norm.shape[0]
    bp, tp, d = x_prompt.shape
    bs, ts, _ = x_sample.shape
    past = cache_sb_k.shape[2]
    sb_heads, head_dim = cache_sb_k.shape[3], cache_sb_k.shape[4]
    assert head_dim == HEAD_DIM
    sb_width = sb_heads * head_dim
    pool_width = pool_w.shape[1] * pool_w.shape[2]
    n_mem, xa_heads, xa_dim = cache_mem_k.shape[2:]
    xa_width = xa_heads * xa_dim
    dims = (sb_width, pool_width, xa_width, xa_heads)
    g_fin = g_final.reshape(1, d)

    yp, ys = x_prompt, x_sample
    kp_l, vp_l, hp_l, mkp_l, mvp_l, ks_l, vs_l, hs_l = [], [], [], [], [], [], [], []
    for l in range(depth):
        final = l == depth - 1
        g_l = g_norm[l].reshape(1, d)
        w_in_b = w_in[l].astype(BF16)
        w_out_b = w_out[l].astype(BF16)
        pool_bd = _block_diag(pool_w[l]).astype(BF16)
        p_scale = pool_scale[l].reshape(1, pool_width)
        mk, mv = _memory_kv(mem_prompt.reshape(bp * n_mem, d), g_mem[l].reshape(1, d),
                            w_mem_kv[l].astype(BF16), tm=256)
        mk = mk.reshape(bp, n_mem, xa_width)
        mv = mv.reshape(bp, n_mem, xa_width)
        hist0 = jnp.zeros((bp, POOL_STATE, pool_width), x_prompt.dtype)
        yp, kp, vp, hp = _mixer_layer(
            yp, 0, None, None, hist0, mk, mv, g_l, w_in_b, pool_bd, p_scale, w_out_b, g_fin,
            dims=dims, tiles=(512, 256, 256, 256), apply_final=final)
        ys, kn, vn, hn = _mixer_layer(
            ys, past, cache_sb_k[l].reshape(bs, past, sb_width),
            cache_sb_v[l].reshape(bs, past, sb_width), state_pool[l],
            cache_mem_k[l].reshape(bs, n_mem, xa_width),
            cache_mem_v[l].reshape(bs, n_mem, xa_width),
            g_l, w_in_b, pool_bd, p_scale, w_out_b, g_fin,
            dims=dims, tiles=(512, ts, 256, ts), apply_final=final)
        kp_l.append(kp.reshape(bp, tp, sb_heads, head_dim))
        vp_l.append(vp.reshape(bp, tp, sb_heads, head_dim))
        hp_l.append(hp)
        mkp_l.append(mk.reshape(bp, n_mem, xa_heads, xa_dim))
        mvp_l.append(mv.reshape(bp, n_mem, xa_heads, xa_dim))
        ks_l.append(kn.reshape(bs, ts, sb_heads, head_dim))
        vs_l.append(vn.reshape(bs, ts, sb_heads, head_dim))
        hs_l.append(hn)
    stack = lambda xs: jnp.stack(xs, axis=0)
    return (yp, ys, stack(kp_l), stack(vp_l), stack(hp_l), stack(mkp_l), stack(mvp_l),
            stack(ks_l), stack(vs_l), stack(hs_l))
```

```python
import functools

import jax
import jax.numpy as jnp
from jax import lax
from jax.experimental import pallas as pl
from jax.experimental.pallas import tpu as pltpu

F32 = jnp.float32
BF16 = jnp.bfloat16

HEAD_DIM = 64
LANES = 128
POOL_WINDOWS = (2, 4, 8, 16)
POOL_STATE = max(POOL_WINDOWS) - 1
HIST_ROWS = 16
EPS = 1e-6
DEAD_CARRY = 104.0
VMEM_LIMIT = 56 * 1024 * 1024


def _rms(x, g):
    ms = jnp.mean(x * x, axis=-1, keepdims=True)
    return (x * lax.rsqrt(ms + EPS)) * g


def _proj_kernel(x_ref, g_ref, w_ref, k_ref, v_ref, qkv_ref, rest_ref, *, sb_width):
    h = _rms(x_ref[...], g_ref[...]).astype(BF16)
    s = sb_width
    q = jnp.dot(h, w_ref[:, 0:s], preferred_element_type=F32)
    qkv_ref[:, 0:s] = (q * (HEAD_DIM ** -0.5)).astype(BF16)
    k = jnp.dot(h, w_ref[:, s:2 * s], preferred_element_type=F32)
    k_ref[...] = k
    qkv_ref[:, s:2 * s] = k.astype(BF16)
    v = jnp.dot(h, w_ref[:, 2 * s:3 * s], preferred_element_type=F32)
    v_ref[...] = v
    qkv_ref[:, 2 * s:3 * s] = v.astype(BF16)
    rest_ref[...] = jnp.dot(h, w_ref[:, 3 * s:], preferred_element_type=F32)


def _project(x2d, g, w_bf16, sb_width, tm):
    m, d = x2d.shape
    n = w_bf16.shape[1]
    n_rest = n - 3 * sb_width
    return pl.pallas_call(
        functools.partial(_proj_kernel, sb_width=sb_width),
        grid=(m // tm,),
        in_specs=[
            pl.BlockSpec((tm, d), lambda i: (i, 0)),
            pl.BlockSpec((1, d), lambda i: (0, 0)),
            pl.BlockSpec((d, n), lambda i: (0, 0)),
        ],
        out_specs=[
            pl.BlockSpec((tm, sb_width), lambda i: (i, 0)),
            pl.BlockSpec((tm, sb_width), lambda i: (i, 0)),
            pl.BlockSpec((tm, 3 * sb_width), lambda i: (i, 0)),
            pl.BlockSpec((tm, n_rest), lambda i: (i, 0)),
        ],
        out_shape=[
            jax.ShapeDtypeStruct((m, sb_width), F32),
            jax.ShapeDtypeStruct((m, sb_width), F32),
            jax.ShapeDtypeStruct((m, 3 * sb_width), BF16),
            jax.ShapeDtypeStruct((m, n_rest), F32),
        ],
        compiler_params=pltpu.CompilerParams(
            dimension_semantics=("arbitrary",), vmem_limit_bytes=VMEM_LIMIT),
        name="proj",
    )(x2d, g, w_bf16)


def _memkv_kernel(x_ref, g_ref, w_ref, mk_ref, mv_ref, *, xa_width):
    h = _rms(x_ref[...], g_ref[...]).astype(BF16)
    kv = jnp.dot(h, w_ref[...], preferred_element_type=F32)
    mk_ref[...] = kv[:, :xa_width]
    mv_ref[...] = kv[:, xa_width:]


def _memory_kv(mem2d, g, w_bf16, tm):
    m, d = mem2d.shape
    xa_width = w_bf16.shape[1] // 2
    return pl.pallas_call(
        functools.partial(_memkv_kernel, xa_width=xa_width),
        grid=(m // tm,),
        in_specs=[
            pl.BlockSpec((tm, d), lambda i: (i, 0)),
            pl.BlockSpec((1, d), lambda i: (0, 0)),
            pl.BlockSpec((d, 2 * xa_width), lambda i: (0, 0)),
        ],
        out_specs=[
            pl.BlockSpec((tm, xa_width), lambda i: (i, 0)),
            pl.BlockSpec((tm, xa_width), lambda i: (i, 0)),
        ],
        out_shape=[
            jax.ShapeDtypeStruct((m, xa_width), F32),
            jax.ShapeDtypeStruct((m, xa_width), F32),
        ],
        compiler_params=pltpu.CompilerParams(
            dimension_semantics=("arbitrary",), vmem_limit_bytes=VMEM_LIMIT),
        name="memkv",
    )(mem2d, g, w_bf16)


def _strict_upper(n):
    r = lax.broadcasted_iota(jnp.int32, (n, n), 0)
    c = lax.broadcasted_iota(jnp.int32, (n, n), 1)
    return jnp.where(r > c, 1.0, 0.0).astype(BF16)


def _softplus(z):
    return jnp.maximum(z, 0.0) + jnp.log(1.0 + jnp.exp(-jnp.abs(z)))


def _sum_later(sp, upper):
    hi = sp.astype(BF16)
    lo = (sp - hi.astype(F32)).astype(BF16)
    return jnp.dot(jnp.concatenate([hi, lo], axis=1), jnp.concatenate([upper, upper], axis=0),
                   preferred_element_type=F32)


def _sb_block(qh, kblk, vblk, upper, carry, mask):
    z = lax.dot_general(qh, kblk, (((1,), (1,)), ((), ())), preferred_element_type=F32)
    sp = _softplus(z)
    if mask is not None:
        sp = jnp.where(mask, sp, 0.0)
    later = _sum_later(sp, upper)
    w = jnp.exp((z - sp) - later - carry)
    if mask is not None:
        w = jnp.where(mask, w, 0.0)
    out = jnp.dot(w.astype(BF16), vblk, preferred_element_type=F32)
    return out, carry + jnp.sum(sp, axis=-1, keepdims=True)


def _sb_kernel(q_ref, kd_ref, vd_ref, kp_ref, vp_ref, o_ref, acc0_ref, acc1_ref,
               *, tq, tk, n_past_static):
    q = q_ref[0]
    lane = lax.broadcasted_iota(jnp.int32, (tq, LANES), 1)
    first = lane < HEAD_DIM
    zero = jnp.zeros_like(q)
    q0 = jnp.where(first, q, zero)
    q1 = jnp.where(first, zero, q)

    r = lax.broadcasted_iota(jnp.int32, (tq, tq), 0)
    c = lax.broadcasted_iota(jnp.int32, (tq, tq), 1)
    causal = c < r
    upper_d = _strict_upper(tq)
    kd = kd_ref[0]
    vd = vd_ref[0]
    zc = jnp.zeros((tq, 1), F32)
    o0, c0 = _sb_block(q0, kd, vd, upper_d, zc, causal)
    o1, c1 = _sb_block(q1, kd, vd, upper_d, zc, causal)
    acc0_ref[...] = o0
    acc1_ref[...] = o1

    if n_past_static is None:
        n_past = pl.program_id(2) * (tq // tk)
    else:
        n_past = n_past_static
    upper = _strict_upper(tk)

    def live(state):
        j, c0, c1 = state
        return jnp.logical_and(j < n_past, jnp.min(jnp.minimum(c0, c1)) <= DEAD_CARRY)

    def body(state):
        j, c0, c1 = state
        start = pl.multiple_of((n_past - 1 - j) * tk, tk)
        kblk = kp_ref[0, pl.ds(start, tk), :].astype(BF16)
        vblk = vp_ref[0, pl.ds(start, tk), :].astype(BF16)
        o0, c0 = _sb_block(q0, kblk, vblk, upper, c0, None)
        o1, c1 = _sb_block(q1, kblk, vblk, upper, c1, None)
        acc0_ref[...] += o0
        acc1_ref[...] += o1
        return j + 1, c0, c1

    lax.while_loop(live, body, (jnp.int32(0), c0, c1))
    o_ref[0] = jnp.where(first, acc0_ref[...], acc1_ref[...])


def _sb_window_kernel(q_ref, k_ref, v_ref, o_ref, acc_ref, *, tq):
    i = pl.program_id(2)
    q = q_ref[0]
    lane = lax.broadcasted_iota(jnp.int32, (tq, LANES), 1)
    first = lane < HEAD_DIM
    zero = jnp.zeros_like(q)
    heads = (jnp.where(first, q, zero), jnp.where(first, zero, q))
    r = lax.broadcasted_iota(jnp.int32, (tq, tq), 0)
    c = lax.broadcasted_iota(jnp.int32, (tq, tq), 1)
    causal = c < r
    upper = _strict_upper(tq)
    nt = (((1,), (1,)), ((), ()))

    @pl.when(i == 0)
    def _():
        kd = k_ref[0, 0:tq, :]
        vd = v_ref[0, 0:tq, :]
        zc = jnp.zeros((tq, 1), F32)
        for h, qh in enumerate(heads):
            o, _ = _sb_block(qh, kd, vd, upper, zc, causal)
            acc_ref[h] = o

    @pl.when(i > 0)
    def _():
        ws = pl.multiple_of((i - 1) * tq, tq)
        kwin = k_ref[0, pl.ds(ws, 2 * tq), :]
        vwin = v_ref[0, pl.ds(ws, 2 * tq), :]
        zs = [lax.dot_general(qh, kwin, nt, preferred_element_type=F32) for qh in heads]
        zp = [z[:, :tq] for z in zs]
        zd = [z[:, tq:] for z in zs]
        sp_p = [_softplus(z) for z in zp]
        sp_d = [jnp.where(causal, _softplus(z), 0.0) for z in zd]
        later = _sum_later(jnp.concatenate(sp_p + sp_d, axis=0), upper)
        n = len(heads)
        carries = []
        for h in range(n):
            rs_d = jnp.sum(sp_d[h], axis=-1, keepdims=True)
            later_p = later[h * tq:(h + 1) * tq]
            later_d = later[(n + h) * tq:(n + h + 1) * tq]
            w_p = jnp.exp((zp[h] - sp_p[h]) - later_p - rs_d)
            w_d = jnp.where(causal, jnp.exp((zd[h] - sp_d[h]) - later_d), 0.0)
            w = jnp.concatenate([w_p.astype(BF16), w_d.astype(BF16)], axis=1)
            acc_ref[h] = jnp.dot(w, vwin, preferred_element_type=F32)
            carries.append(rs_d + jnp.sum(sp_p[h], axis=-1, keepdims=True))

        n_older = i - 1

        def live(state):
            j, c0, c1 = state
            return jnp.logical_and(j < n_older, jnp.min(jnp.minimum(c0, c1)) <= DEAD_CARRY)

        def body(state):
            j, c0, c1 = state
            start = pl.multiple_of((n_older - 1 - j) * tq, tq)
            kblk = k_ref[0, pl.ds(start, tq), :]
            vblk = v_ref[0, pl.ds(start, tq), :]
            o0, c0 = _sb_block(heads[0], kblk, vblk, upper, c0, None)
            o1, c1 = _sb_block(heads[1], kblk, vblk, upper, c1, None)
            acc_ref[0] += o0
            acc_ref[1] += o1
            return j + 1, c0, c1

        lax.while_loop(live, body, (jnp.int32(0), carries[0], carries[1]))

    o_ref[0] = jnp.where(first, acc_ref[0], acc_ref[1])


def _stick_breaking_self(qkv, *, tq, sb_width):
    b, t, _ = qkv.shape
    pairs = sb_width // LANES
    return pl.pallas_call(
        functools.partial(_sb_window_kernel, tq=tq),
        grid=(b, pairs, t // tq),
        in_specs=[
            pl.BlockSpec((1, tq, LANES), lambda bi, hp, i: (bi, i, hp)),
            pl.BlockSpec((1, t, LANES), lambda bi, hp, i: (bi, 0, pairs + hp)),
            pl.BlockSpec((1, t, LANES), lambda bi, hp, i: (bi, 0, 2 * pairs + hp)),
        ],
        out_specs=pl.BlockSpec((1, tq, LANES), lambda bi, hp, i: (bi, i, hp)),
        out_shape=jax.ShapeDtypeStruct((b, t, sb_width), F32),
        scratch_shapes=[pltpu.VMEM((2, tq, LANES), F32)],
        compiler_params=pltpu.CompilerParams(
            dimension_semantics=("arbitrary", "arbitrary", "arbitrary"),
            vmem_limit_bytes=VMEM_LIMIT),
        name="stick_breaking_self",
    )(qkv, qkv, qkv)


def _sb_cached_kernel(qkv_ref, kwin_ref, vwin_ref, kc_ref, vc_ref, o_ref,
                      kbuf, vbuf, acc_ref, sem, *, heads, tk, n_older):
    b = pl.program_id(0)
    tq = qkv_ref.shape[1]
    s = heads * HEAD_DIM
    nt = (((1,), (1,)), ((), ()))
    r = lax.broadcasted_iota(jnp.int32, (tq, tq), 0)
    c = lax.broadcasted_iota(jnp.int32, (tq, tq), 1)
    causal = c < r
    upper_d = _strict_upper(tq)
    upper = _strict_upper(tk)
    q = qkv_ref[0, :, 0:s]
    kn = qkv_ref[0, :, s:2 * s]
    vn = qkv_ref[0, :, 2 * s:3 * s]
    head = lambda x, h: x[:, h * HEAD_DIM:(h + 1) * HEAD_DIM]
    qs = [head(q, h) for h in range(heads)]

    zd = [lax.dot_general(qs[h], head(kn, h), nt, preferred_element_type=F32)
          for h in range(heads)]
    zw = [lax.dot_general(qs[h], kwin_ref[0, :, h, :].astype(BF16), nt,
                          preferred_element_type=F32) for h in range(heads)]
    sp_d = [jnp.where(causal, _softplus(z), 0.0) for z in zd]
    sp_w = [_softplus(z) for z in zw]
    later_d = _sum_later(jnp.concatenate(sp_d, axis=0), upper_d)
    later_w = _sum_later(jnp.concatenate(sp_w, axis=0), upper)
    carries = []
    for h in range(heads):
        rows = slice(h * tq, (h + 1) * tq)
        rs_d = jnp.sum(sp_d[h], axis=-1, keepdims=True)
        w_d = jnp.where(causal, jnp.exp((zd[h] - sp_d[h]) - later_d[rows]), 0.0)
        w_w = jnp.exp((zw[h] - sp_w[h]) - later_w[rows] - rs_d)
        acc_ref[h] = (jnp.dot(w_d.astype(BF16), head(vn, h), preferred_element_type=F32)
                      + jnp.dot(w_w.astype(BF16), vwin_ref[0, :, h, :].astype(BF16),
                                preferred_element_type=F32))
        carries.append(rs_d + jnp.sum(sp_w[h], axis=-1, keepdims=True))

    def block_copies(j):
        start = pl.multiple_of((n_older - 1 - j) * tk, tk)
        return (pltpu.make_async_copy(kc_ref.at[b, pl.ds(start, tk)], kbuf, sem.at[0]),
                pltpu.make_async_copy(vc_ref.at[b, pl.ds(start, tk)], vbuf, sem.at[1]))

    def live(state):
        lowest = functools.reduce(jnp.minimum, state[1:])
        return jnp.logical_and(state[0] < n_older, jnp.min(lowest) <= DEAD_CARRY)

    def body(state):
        j = state[0]
        copies = block_copies(j)
        for cp in copies:
            cp.start()
        for cp in copies:
            cp.wait()
        new = []
        for h in range(heads):
            o, ch = _sb_block(qs[h], kbuf[:, h, :].astype(BF16), vbuf[:, h, :].astype(BF16),
                              upper, state[1 + h], None)
            acc_ref[h] += o
            new.append(ch)
        return (j + 1, *new)

    lax.while_loop(live, body, (jnp.int32(0), *carries))
    o_ref[0] = jnp.concatenate([acc_ref[h] for h in range(heads)], axis=1)


def _stick_breaking_cached(qkv, cache_k, cache_v, *, tk):
    b, t, _ = qkv.shape
    _, p, heads, dh = cache_k.shape
    s = heads * dh
    last = p // tk - 1
    win = pl.BlockSpec((1, tk, heads, dh), lambda bi: (bi, last, 0, 0))
    return pl.pallas_call(
        functools.partial(_sb_cached_kernel, heads=heads, tk=tk, n_older=last),
        grid=(b,),
        in_specs=[
            pl.BlockSpec((1, t, 3 * s), lambda bi: (bi, 0, 0)),
            win, win,
            pl.BlockSpec(memory_space=pl.ANY),
            pl.BlockSpec(memory_space=pl.ANY),
        ],
        out_specs=pl.BlockSpec((1, t, s), lambda bi: (bi, 0, 0)),
        out_shape=jax.ShapeDtypeStruct((b, t, s), F32),
        scratch_shapes=[
            pltpu.VMEM((tk, heads, dh), F32),
            pltpu.VMEM((tk, heads, dh), F32),
            pltpu.VMEM((heads, t, dh), F32),
            pltpu.SemaphoreType.DMA((2,)),
        ],
        compiler_params=pltpu.CompilerParams(
            dimension_semantics=("arbitrary",), vmem_limit_bytes=VMEM_LIMIT),
        name="stick_breaking_cached",
    )(qkv, cache_k, cache_v, cache_k, cache_v)


def _stick_breaking(qkv, k_past, v_past, *, tq, tk, sb_width):
    b, t, _ = qkv.shape
    pairs = sb_width // LANES
    if k_past is None:
        k_past, v_past = qkv, qkv
        k_off, v_off = pairs, 2 * pairs
        n_past_static = None
    else:
        k_off, v_off = 0, 0
        n_past_static = k_past.shape[1] // tk
    p = k_past.shape[1]
    return pl.pallas_call(
        functools.partial(_sb_kernel, tq=tq, tk=tk, n_past_static=n_past_static),
        grid=(b, pairs, t // tq),
        in_specs=[
            pl.BlockSpec((1, tq, LANES), lambda bi, hp, i: (bi, i, hp)),
            pl.BlockSpec((1, tq, LANES), lambda bi, hp, i: (bi, i, pairs + hp)),
            pl.BlockSpec((1, tq, LANES), lambda bi, hp, i: (bi, i, 2 * pairs + hp)),
            pl.BlockSpec((1, p, LANES), lambda bi, hp, i: (bi, 0, k_off + hp)),
            pl.BlockSpec((1, p, LANES), lambda bi, hp, i: (bi, 0, v_off + hp)),
        ],
        out_specs=pl.BlockSpec((1, tq, LANES), lambda bi, hp, i: (bi, i, hp)),
        out_shape=jax.ShapeDtypeStruct((b, t, sb_width), F32),
        scratch_shapes=[pltpu.VMEM((tq, LANES), F32), pltpu.VMEM((tq, LANES), F32)],
        compiler_params=pltpu.CompilerParams(
            dimension_semantics=("arbitrary", "arbitrary", "arbitrary"),
            vmem_limit_bytes=VMEM_LIMIT),
        name="stick_breaking",
    )(qkv, qkv, qkv, k_past, v_past)


def _silu(g):
    return g / (1.0 + jnp.exp(-g))


def _epilogue_kernel(x_ref, osb_ref, rest_ref, prev_ref, hist0_ref, mk_ref, mv_ref,
                     poolw_ref, pscale_ref, wout_ref, gfin_ref, y_ref,
                     *, tt, start, sb_width, pool_width, xa_width, xa_heads, apply_final):
    i = pl.program_id(1)
    s, pw, xw = sb_width, pool_width, xa_width
    g_sb = rest_ref[0, :, 0:s]
    u = rest_ref[0, :, s:s + pw]
    g_pool = rest_ref[0, :, s + pw:s + 2 * pw]
    q_xa = rest_ref[0, :, s + 2 * pw:s + 2 * pw + xw]
    g_xa = rest_ref[0, :, s + 2 * pw + xw:s + 2 * pw + 2 * xw]

    hist = jnp.where(i == 0, hist0_ref[0], prev_ref[0])
    ext = jnp.concatenate([hist, u], axis=0)
    sums = {1: ext}
    width = 1
    while width < max(POOL_WINDOWS):
        prev = sums[width]
        sums[2 * width] = prev + pltpu.roll(prev, width, 0)
        width *= 2
    lane = lax.broadcasted_iota(jnp.int32, (tt, pw), 1)
    group = lane // (pw // len(POOL_WINDOWS))
    pos = start + i * tt + lax.broadcasted_iota(jnp.int32, (tt, pw), 0)
    win_sum = jnp.zeros((tt, pw), F32)
    win = jnp.zeros((tt, pw), jnp.int32)
    for gi, w in enumerate(POOL_WINDOWS):
        sel = group == gi
        win_sum = jnp.where(sel, sums[w][HIST_ROWS:], win_sum)
        win = jnp.where(sel, w, win)
    cnt = jnp.minimum(pos + 1, win).astype(F32)
    pooled = win_sum / cnt - u
    o_pool = jnp.dot(pooled.astype(BF16), poolw_ref[...],
                     preferred_element_type=F32) * pscale_ref[...]

    mk = mk_ref[0].astype(BF16)
    mv = mv_ref[0].astype(BF16)
    xa_dim = xw // xa_heads
    lane_x = lax.broadcasted_iota(jnp.int32, (tt, xw), 1)
    head_x = lane_x // xa_dim
    qs = q_xa * (xa_dim ** -0.5)
    o_xa = jnp.zeros((tt, xw), F32)
    for hh in range(xa_heads):
        sel = head_x == hh
        qh = jnp.where(sel, qs, 0.0).astype(BF16)
        sc = lax.dot_general(qh, mk, (((1,), (1,)), ((), ())), preferred_element_type=F32)
        e = jnp.exp(sc - jnp.max(sc, axis=-1, keepdims=True))
        p = e / jnp.sum(e, axis=-1, keepdims=True)
        oh = jnp.dot(p.astype(BF16), mv, preferred_element_type=F32)
        o_xa = jnp.where(sel, oh, o_xa)

    m_sb = (osb_ref[0] * _silu(g_sb)).astype(BF16)
    m_pool = (o_pool * _silu(g_pool)).astype(BF16)
    m_xa = (o_xa * _silu(g_xa)).astype(BF16)
    y = (x_ref[0]
         + jnp.dot(m_sb, wout_ref[0:s, :], preferred_element_type=F32)
         + jnp.dot(m_pool, wout_ref[s:s + pw, :], preferred_element_type=F32)
         + jnp.dot(m_xa, wout_ref[s + pw:s + pw + xw, :], preferred_element_type=F32))
    if apply_final:
        y = _rms(y, gfin_ref[...])
    y_ref[0] = y


def _epilogue(x, o_sb, rest, hist0, mk, mv, pool_bd, pool_scale, w_out_bf16, g_final,
              *, tt, start, sb_width, pool_width, xa_width, xa_heads, apply_final):
    b, t, d = x.shape
    n_rest = rest.shape[-1]
    n_mem = mk.shape[1]
    hist_blocks = tt // HIST_ROWS
    u_col = sb_width // pool_width
    kern = functools.partial(
        _epilogue_kernel, tt=tt, start=start, sb_width=sb_width, pool_width=pool_width,
        xa_width=xa_width, xa_heads=xa_heads, apply_final=apply_final)
    return pl.pallas_call(
        kern,
        grid=(b, t // tt),
        in_specs=[
            pl.BlockSpec((1, tt, d), lambda bi, i: (bi, i, 0)),
            pl.BlockSpec((1, tt, sb_width), lambda bi, i: (bi, i, 0)),
            pl.BlockSpec((1, tt, n_rest), lambda bi, i: (bi, i, 0)),
            pl.BlockSpec((1, HIST_ROWS, pool_width),
                         lambda bi, i: (bi, jnp.maximum(i * hist_blocks - 1, 0), u_col)),
            pl.BlockSpec((1, HIST_ROWS, pool_width), lambda bi, i: (bi, 0, 0)),
            pl.BlockSpec((1, n_mem, xa_width), lambda bi, i: (bi, 0, 0)),
            pl.BlockSpec((1, n_mem, xa_width), lambda bi, i: (bi, 0, 0)),
            pl.BlockSpec((pool_width, pool_width), lambda bi, i: (0, 0)),
            pl.BlockSpec((1, pool_width), lambda bi, i: (0, 0)),
            pl.BlockSpec((d, d), lambda bi, i: (0, 0)),
            pl.BlockSpec((1, d), lambda bi, i: (0, 0)),
        ],
        out_specs=pl.BlockSpec((1, tt, d), lambda bi, i: (bi, i, 0)),
        out_shape=jax.ShapeDtypeStruct((b, t, d), F32),
        compiler_params=pltpu.CompilerParams(
            dimension_semantics=("arbitrary", "arbitrary"), vmem_limit_bytes=VMEM_LIMIT),
        name="epilogue",
    )(x, o_sb, rest, rest, hist0, mk, mv, pool_bd, pool_scale, w_out_bf16, g_final)


def _block_diag(pool_w):
    g, c, _ = pool_w.shape
    eye = jnp.eye(g, dtype=pool_w.dtype)
    return (eye[:, None, :, None] * pool_w[:, :, None, :]).reshape(g * c, g * c)


def _pad_hist(hist):
    return jnp.pad(hist, ((0, 0), (HIST_ROWS - hist.shape[1], 0), (0, 0)))


def _mixer_layer(x, start, k_past, v_past, pool_hist, mk, mv, g_norm, w_in_bf16, pool_bd,
                 pool_scale, w_out_bf16, g_final, *, dims, tiles, apply_final):
    sb_width, pool_width, xa_width, xa_heads = dims
    tm, tq, tk, tt = tiles
    b, t, d = x.shape
    k2d, v2d, qkv, rest = _project(x.reshape(b * t, d), g_norm, w_in_bf16, sb_width, tm)
    qkv = qkv.reshape(b, t, -1)
    rest = rest.reshape(b, t, -1)
    if k_past is None:
        o_sb = _stick_breaking_self(qkv, tq=tq, sb_width=sb_width)
    else:
        o_sb = _stick_breaking_cached(qkv, k_past, v_past, tk=tk)
    y = _epilogue(x, o_sb, rest, _pad_hist(pool_hist), mk, mv, pool_bd, pool_scale,
                  w_out_bf16, g_final, tt=tt, start=start, sb_width=sb_width,
                  pool_width=pool_width, xa_width=xa_width, xa_heads=xa_heads,
                  apply_final=apply_final)
    u_pool = rest[:, :, sb_width:sb_width + pool_width]
    new_hist = jnp.concatenate([pool_hist, u_pool], axis=1)[:, -POOL_STATE:]
    return y, k2d, v2d, new_hist


def kernel(x_prompt, x_sample, cache_sb_k, cache_sb_v, state_pool, cache_mem_k, cache_mem_v,
           mem_prompt, g_norm, w_in, pool_w, pool_scale, g_mem, w_mem_kv, w_out, g_final):
    depth = g_norm.shape[0]
    bp, tp, d = x_prompt.shape
    bs, ts, _ = x_sample.shape
    past = cache_sb_k.shape[2]
    sb_heads, head_dim = cache_sb_k.shape[3], cache_sb_k.shape[4]
    assert head_dim == HEAD_DIM
    sb_width = sb_heads * head_dim
    pool_width = pool_w.shape[1] * pool_w.shape[2]
    n_mem, xa_heads, xa_dim = cache_mem_k.shape[2:]
    xa_width = xa_heads * xa_dim
    dims = (sb_width, pool_width, xa_width, xa_heads)
    g_fin = g_final.reshape(1, d)

    yp, ys = x_prompt, x_sample
    kp_l, vp_l, hp_l, mkp_l, mvp_l, ks_l, vs_l, hs_l = [], [], [], [], [], [], [], []
    for l in range(depth):
        final = l == depth - 1
        g_l = g_norm[l].reshape(1, d)
        w_in_b = w_in[l].astype(BF16)
        w_out_b = w_out[l].astype(BF16)
        pool_bd = _block_diag(pool_w[l]).astype(BF16)
        p_scale = pool_scale[l].reshape(1, pool_width)
        mk, mv = _memory_kv(mem_prompt.reshape(bp * n_mem, d), g_mem[l].reshape(1, d),
                            w_mem_kv[l].astype(BF16), tm=256)
        mk = mk.reshape(bp, n_mem, xa_width)
        mv = mv.reshape(bp, n_mem, xa_width)
        hist0 = jnp.zeros((bp, POOL_STATE, pool_width), x_prompt.dtype)
        yp, kp, vp, hp = _mixer_layer(
            yp, 0, None, None, hist0, mk, mv, g_l, w_in_b, pool_bd, p_scale, w_out_b, g_fin,
            dims=dims, tiles=(512, 256, 256, 256), apply_final=final)
        ys, kn, vn, hn = _mixer_layer(
            ys, past, cache_sb_k[l], cache_sb_v[l], state_pool[l],
            cache_mem_k[l].reshape(bs, n_mem, xa_width),
            cache_mem_v[l].reshape(bs, n_mem, xa_width),
            g_l, w_in_b, pool_bd, p_scale, w_out_b, g_fin,
            dims=dims, tiles=(512, ts, 256, ts), apply_final=final)
        kp_l.append(kp.reshape(bp, tp, sb_heads, head_dim))
        vp_l.append(vp.reshape(bp, tp, sb_heads, head_dim))
        hp_l.append(hp)
        mkp_l.append(mk.reshape(bp, n_mem, xa_heads, xa_dim))
        mvp_l.append(mv.reshape(bp, n_mem, xa_heads, xa_dim))
        ks_l.append(kn.reshape(bs, ts, sb_heads, head_dim))
        vs_l.append(vn.reshape(bs, ts, sb_heads, head_dim))
        hs_l.append(hn)
    stack = lambda xs: jnp.stack(xs, axis=0)
    return (yp, ys, stack(kp_l), stack(vp_l), stack(hp_l), stack(mkp_l), stack(mvp_l),
            stack(ks_l), stack(vs_l), stack(hs_l))
```

```python
import functools

import jax
import jax.numpy as jnp
from jax import lax
from jax.experimental import pallas as pl
from jax.experimental.pallas import tpu as pltpu

F32 = jnp.float32
BF16 = jnp.bfloat16

HEAD_DIM = 64
LANES = 128
POOL_WINDOWS = (2, 4, 8, 16)
POOL_STATE = max(POOL_WINDOWS) - 1
HIST_ROWS = 16
EPS = 1e-6
DEAD_CARRY = 104.0
VMEM_LIMIT = 56 * 1024 * 1024


def _rms(x, g):
    ms = jnp.mean(x * x, axis=-1, keepdims=True)
    return (x * lax.rsqrt(ms + EPS)) * g


def _proj_kernel(x_ref, g_ref, w_ref, k_ref, v_ref, qkv_ref, rest_ref, *, sb_width):
    h = _rms(x_ref[...], g_ref[...]).astype(BF16)
    s = sb_width
    q = jnp.dot(h, w_ref[:, 0:s], preferred_element_type=F32)
    qkv_ref[:, 0:s] = (q * (HEAD_DIM ** -0.5)).astype(BF16)
    k = jnp.dot(h, w_ref[:, s:2 * s], preferred_element_type=F32)
    k_ref[...] = k
    qkv_ref[:, s:2 * s] = k.astype(BF16)
    v = jnp.dot(h, w_ref[:, 2 * s:3 * s], preferred_element_type=F32)
    v_ref[...] = v
    qkv_ref[:, 2 * s:3 * s] = v.astype(BF16)
    rest_ref[...] = jnp.dot(h, w_ref[:, 3 * s:], preferred_element_type=F32)


def _project(x2d, g, w_bf16, sb_width, tm):
    m, d = x2d.shape
    n = w_bf16.shape[1]
    n_rest = n - 3 * sb_width
    return pl.pallas_call(
        functools.partial(_proj_kernel, sb_width=sb_width),
        grid=(m // tm,),
        in_specs=[
            pl.BlockSpec((tm, d), lambda i: (i, 0)),
            pl.BlockSpec((1, d), lambda i: (0, 0)),
            pl.BlockSpec((d, n), lambda i: (0, 0)),
        ],
        out_specs=[
            pl.BlockSpec((tm, sb_width), lambda i: (i, 0)),
            pl.BlockSpec((tm, sb_width), lambda i: (i, 0)),
            pl.BlockSpec((tm, 3 * sb_width), lambda i: (i, 0)),
            pl.BlockSpec((tm, n_rest), lambda i: (i, 0)),
        ],
        out_shape=[
            jax.ShapeDtypeStruct((m, sb_width), F32),
            jax.ShapeDtypeStruct((m, sb_width), F32),
            jax.ShapeDtypeStruct((m, 3 * sb_width), BF16),
            jax.ShapeDtypeStruct((m, n_rest), F32),
        ],
        compiler_params=pltpu.CompilerParams(
            dimension_semantics=("arbitrary",), vmem_limit_bytes=VMEM_LIMIT),
        name="proj",
    )(x2d, g, w_bf16)


def _memkv_kernel(x_ref, g_ref, w_ref, mk_ref, mv_ref, *, xa_width):
    h = _rms(x_ref[...], g_ref[...]).astype(BF16)
    kv = jnp.dot(h, w_ref[...], preferred_element_type=F32)
    mk_ref[...] = kv[:, :xa_width]
    mv_ref[...] = kv[:, xa_width:]


def _memory_kv(mem2d, g, w_bf16, tm):
    m, d = mem2d.shape
    xa_width = w_bf16.shape[1] // 2
    return pl.pallas_call(
        functools.partial(_memkv_kernel, xa_width=xa_width),
        grid=(m // tm,),
        in_specs=[
            pl.BlockSpec((tm, d), lambda i: (i, 0)),
            pl.BlockSpec((1, d), lambda i: (0, 0)),
            pl.BlockSpec((d, 2 * xa_width), lambda i: (0, 0)),
        ],
        out_specs=[
            pl.BlockSpec((tm, xa_width), lambda i: (i, 0)),
            pl.BlockSpec((tm, xa_width), lambda i: (i, 0)),
        ],
        out_shape=[
            jax.ShapeDtypeStruct((m, xa_width), F32),
            jax.ShapeDtypeStruct((m, xa_width), F32),
        ],
        compiler_params=pltpu.CompilerParams(
            dimension_semantics=("arbitrary",), vmem_limit_bytes=VMEM_LIMIT),
        name="memkv",
    )(mem2d, g, w_bf16)


def _strict_upper(n):
    r = lax.broadcasted_iota(jnp.int32, (n, n), 0)
    c = lax.broadcasted_iota(jnp.int32, (n, n), 1)
    return jnp.where(r > c, 1.0, 0.0).astype(BF16)


def _softplus(z):
    return jnp.maximum(z, 0.0) + jnp.log(1.0 + jnp.exp(-jnp.abs(z)))


def _sum_later(sp, upper):
    hi = sp.astype(BF16)
    lo = (sp - hi.astype(F32)).astype(BF16)
    return jnp.dot(jnp.concatenate([hi, lo], axis=1), jnp.concatenate([upper, upper], axis=0),
                   preferred_element_type=F32)


NT_DIMS = (((1,), (1,)), ((), ()))


def _sb_block(qh, kblk, vblk, upper, carry, mask, transposed=False):
    if transposed:
        z = jnp.dot(qh, kblk, preferred_element_type=F32)
    else:
        z = lax.dot_general(qh, kblk, NT_DIMS, preferred_element_type=F32)
    sp = _softplus(z)
    if mask is not None:
        sp = jnp.where(mask, sp, 0.0)
    later = _sum_later(sp, upper)
    w = jnp.exp((z - sp) - later - carry)
    if mask is not None:
        w = jnp.where(mask, w, 0.0)
    if transposed:
        out = lax.dot_general(w.astype(BF16), vblk, NT_DIMS, preferred_element_type=F32)
    else:
        out = jnp.dot(w.astype(BF16), vblk, preferred_element_type=F32)
    return out, carry + jnp.sum(sp, axis=-1, keepdims=True)


def _sb_kernel(q_ref, kd_ref, vd_ref, kp_ref, vp_ref, o_ref, acc0_ref, acc1_ref,
               *, tq, tk, n_past_static):
    q = q_ref[0]
    lane = lax.broadcasted_iota(jnp.int32, (tq, LANES), 1)
    first = lane < HEAD_DIM
    zero = jnp.zeros_like(q)
    q0 = jnp.where(first, q, zero)
    q1 = jnp.where(first, zero, q)

    r = lax.broadcasted_iota(jnp.int32, (tq, tq), 0)
    c = lax.broadcasted_iota(jnp.int32, (tq, tq), 1)
    causal = c < r
    upper_d = _strict_upper(tq)
    kd = kd_ref[0]
    vd = vd_ref[0]
    zc = jnp.zeros((tq, 1), F32)
    o0, c0 = _sb_block(q0, kd, vd, upper_d, zc, causal)
    o1, c1 = _sb_block(q1, kd, vd, upper_d, zc, causal)
    acc0_ref[...] = o0
    acc1_ref[...] = o1

    if n_past_static is None:
        n_past = pl.program_id(2) * (tq // tk)
    else:
        n_past = n_past_static
    upper = _strict_upper(tk)

    def live(state):
        j, c0, c1 = state
        return jnp.logical_and(j < n_past, jnp.min(jnp.minimum(c0, c1)) <= DEAD_CARRY)

    def body(state):
        j, c0, c1 = state
        start = pl.multiple_of((n_past - 1 - j) * tk, tk)
        kblk = kp_ref[0, pl.ds(start, tk), :].astype(BF16)
        vblk = vp_ref[0, pl.ds(start, tk), :].astype(BF16)
        o0, c0 = _sb_block(q0, kblk, vblk, upper, c0, None)
        o1, c1 = _sb_block(q1, kblk, vblk, upper, c1, None)
        acc0_ref[...] += o0
        acc1_ref[...] += o1
        return j + 1, c0, c1

    lax.while_loop(live, body, (jnp.int32(0), c0, c1))
    o_ref[0] = jnp.where(first, acc0_ref[...], acc1_ref[...])


def _sb_window_kernel(q_ref, k_ref, v_ref, o_ref, acc_ref, *, tq):
    i = pl.program_id(2)
    q = q_ref[0]
    lane = lax.broadcasted_iota(jnp.int32, (tq, LANES), 1)
    first = lane < HEAD_DIM
    zero = jnp.zeros_like(q)
    heads = (jnp.where(first, q, zero), jnp.where(first, zero, q))
    r = lax.broadcasted_iota(jnp.int32, (tq, tq), 0)
    c = lax.broadcasted_iota(jnp.int32, (tq, tq), 1)
    causal = c < r
    upper = _strict_upper(tq)
    nt = (((1,), (1,)), ((), ()))

    @pl.when(i == 0)
    def _():
        kd = k_ref[0, 0:tq, :]
        vd = v_ref[0, 0:tq, :]
        zc = jnp.zeros((tq, 1), F32)
        for h, qh in enumerate(heads):
            o, _ = _sb_block(qh, kd, vd, upper, zc, causal)
            acc_ref[h] = o

    @pl.when(i > 0)
    def _():
        ws = pl.multiple_of((i - 1) * tq, tq)
        kwin = k_ref[0, pl.ds(ws, 2 * tq), :]
        vwin = v_ref[0, pl.ds(ws, 2 * tq), :]
        zs = [lax.dot_general(qh, kwin, nt, preferred_element_type=F32) for qh in heads]
        zp = [z[:, :tq] for z in zs]
        zd = [z[:, tq:] for z in zs]
        sp_p = [_softplus(z) for z in zp]
        sp_d = [jnp.where(causal, _softplus(z), 0.0) for z in zd]
        later = _sum_later(jnp.concatenate(sp_p + sp_d, axis=0), upper)
        n = len(heads)
        carries = []
        for h in range(n):
            rs_d = jnp.sum(sp_d[h], axis=-1, keepdims=True)
            later_p = later[h * tq:(h + 1) * tq]
            later_d = later[(n + h) * tq:(n + h + 1) * tq]
            w_p = jnp.exp((zp[h] - sp_p[h]) - later_p - rs_d)
            w_d = jnp.where(causal, jnp.exp((zd[h] - sp_d[h]) - later_d), 0.0)
            w = jnp.concatenate([w_p.astype(BF16), w_d.astype(BF16)], axis=1)
            acc_ref[h] = jnp.dot(w, vwin, preferred_element_type=F32)
            carries.append(rs_d + jnp.sum(sp_p[h], axis=-1, keepdims=True))

        n_older = i - 1

        def live(state):
            j, c0, c1 = state
            return jnp.logical_and(j < n_older, jnp.min(jnp.minimum(c0, c1)) <= DEAD_CARRY)

        def body(state):
            j, c0, c1 = state
            start = pl.multiple_of((n_older - 1 - j) * tq, tq)
            kblk = k_ref[0, pl.ds(start, tq), :]
            vblk = v_ref[0, pl.ds(start, tq), :]
            o0, c0 = _sb_block(heads[0], kblk, vblk, upper, c0, None)
            o1, c1 = _sb_block(heads[1], kblk, vblk, upper, c1, None)
            acc_ref[0] += o0
            acc_ref[1] += o1
            return j + 1, c0, c1

        lax.while_loop(live, body, (jnp.int32(0), carries[0], carries[1]))

    o_ref[0] = jnp.where(first, acc_ref[0], acc_ref[1])


def _stick_breaking_self(qkv, *, tq, sb_width):
    b, t, _ = qkv.shape
    pairs = sb_width // LANES
    return pl.pallas_call(
        functools.partial(_sb_window_kernel, tq=tq),
        grid=(b, pairs, t // tq),
        in_specs=[
            pl.BlockSpec((1, tq, LANES), lambda bi, hp, i: (bi, i, hp)),
            pl.BlockSpec((1, t, LANES), lambda bi, hp, i: (bi, 0, pairs + hp)),
            pl.BlockSpec((1, t, LANES), lambda bi, hp, i: (bi, 0, 2 * pairs + hp)),
        ],
        out_specs=pl.BlockSpec((1, tq, LANES), lambda bi, hp, i: (bi, i, hp)),
        out_shape=jax.ShapeDtypeStruct((b, t, sb_width), F32),
        scratch_shapes=[pltpu.VMEM((2, tq, LANES), F32)],
        compiler_params=pltpu.CompilerParams(
            dimension_semantics=("arbitrary", "arbitrary", "arbitrary"),
            vmem_limit_bytes=VMEM_LIMIT),
        name="stick_breaking_self",
    )(qkv, qkv, qkv)


def _sb_cached_kernel(qkv_ref, kwin_ref, vwin_ref, kc_ref, vc_ref, o_ref,
                      kbuf, vbuf, acc_ref, sem, *, heads, tk, n_older):
    b = pl.program_id(0)
    tq = qkv_ref.shape[1]
    s = heads * HEAD_DIM
    nt = (((1,), (1,)), ((), ()))
    r = lax.broadcasted_iota(jnp.int32, (tq, tq), 0)
    c = lax.broadcasted_iota(jnp.int32, (tq, tq), 1)
    causal = c < r
    upper_d = _strict_upper(tq)
    upper = _strict_upper(tk)
    q = qkv_ref[0, :, 0:s]
    kn = qkv_ref[0, :, s:2 * s]
    vn = qkv_ref[0, :, 2 * s:3 * s]
    head = lambda x, h: x[:, h * HEAD_DIM:(h + 1) * HEAD_DIM]
    qs = [head(q, h) for h in range(heads)]

    zd = [lax.dot_general(qs[h], head(kn, h), nt, preferred_element_type=F32)
          for h in range(heads)]
    zw = [jnp.dot(qs[h], kwin_ref[0, h].astype(BF16), preferred_element_type=F32)
          for h in range(heads)]
    sp_d = [jnp.where(causal, _softplus(z), 0.0) for z in zd]
    sp_w = [_softplus(z) for z in zw]
    later_d = _sum_later(jnp.concatenate(sp_d, axis=0), upper_d)
    later_w = _sum_later(jnp.concatenate(sp_w, axis=0), upper)
    carries = []
    for h in range(heads):
        rows = slice(h * tq, (h + 1) * tq)
        rs_d = jnp.sum(sp_d[h], axis=-1, keepdims=True)
        w_d = jnp.where(causal, jnp.exp((zd[h] - sp_d[h]) - later_d[rows]), 0.0)
        w_w = jnp.exp((zw[h] - sp_w[h]) - later_w[rows] - rs_d)
        acc_ref[h] = (jnp.dot(w_d.astype(BF16), head(vn, h), preferred_element_type=F32)
                      + lax.dot_general(w_w.astype(BF16), vwin_ref[0, h].astype(BF16), NT_DIMS,
                                        preferred_element_type=F32))
        carries.append(rs_d + jnp.sum(sp_w[h], axis=-1, keepdims=True))

    def block_copies(j):
        start = pl.multiple_of((n_older - 1 - j) * tk, tk)
        return (pltpu.make_async_copy(kc_ref.at[b, :, :, pl.ds(start, tk)], kbuf, sem.at[0]),
                pltpu.make_async_copy(vc_ref.at[b, :, :, pl.ds(start, tk)], vbuf, sem.at[1]))

    def live(state):
        lowest = functools.reduce(jnp.minimum, state[1:])
        return jnp.logical_and(state[0] < n_older, jnp.min(lowest) <= DEAD_CARRY)

    def body(state):
        j = state[0]
        copies = block_copies(j)
        for cp in copies:
            cp.start()
        for cp in copies:
            cp.wait()
        new = []
        for h in range(heads):
            o, ch = _sb_block(qs[h], kbuf[h].astype(BF16), vbuf[h].astype(BF16),
                              upper, state[1 + h], None, transposed=True)
            acc_ref[h] += o
            new.append(ch)
        return (j + 1, *new)

    lax.while_loop(live, body, (jnp.int32(0), *carries))
    o_ref[0] = jnp.concatenate([acc_ref[h] for h in range(heads)], axis=1)


def _stick_breaking_cached(qkv, cache_k, cache_v, *, tk):
    b, t, _ = qkv.shape
    _, heads, dh, p = cache_k.shape
    s = heads * dh
    last = p // tk - 1
    win = pl.BlockSpec((1, heads, dh, tk), lambda bi: (bi, 0, 0, last))
    return pl.pallas_call(
        functools.partial(_sb_cached_kernel, heads=heads, tk=tk, n_older=last),
        grid=(b,),
        in_specs=[
            pl.BlockSpec((1, t, 3 * s), lambda bi: (bi, 0, 0)),
            win, win,
            pl.BlockSpec(memory_space=pl.ANY),
            pl.BlockSpec(memory_space=pl.ANY),
        ],
        out_specs=pl.BlockSpec((1, t, s), lambda bi: (bi, 0, 0)),
        out_shape=jax.ShapeDtypeStruct((b, t, s), F32),
        scratch_shapes=[
            pltpu.VMEM((heads, dh, tk), F32),
            pltpu.VMEM((heads, dh, tk), F32),
            pltpu.VMEM((heads, t, dh), F32),
            pltpu.SemaphoreType.DMA((2,)),
        ],
        compiler_params=pltpu.CompilerParams(
            dimension_semantics=("arbitrary",), vmem_limit_bytes=VMEM_LIMIT),
        name="stick_breaking_cached",
    )(qkv, cache_k, cache_v, cache_k, cache_v)


def _stick_breaking(qkv, k_past, v_past, *, tq, tk, sb_width):
    b, t, _ = qkv.shape
    pairs = sb_width // LANES
    if k_past is None:
        k_past, v_past = qkv, qkv
        k_off, v_off = pairs, 2 * pairs
        n_past_static = None
    else:
        k_off, v_off = 0, 0
        n_past_static = k_past.shape[1] // tk
    p = k_past.shape[1]
    return pl.pallas_call(
        functools.partial(_sb_kernel, tq=tq, tk=tk, n_past_static=n_past_static),
        grid=(b, pairs, t // tq),
        in_specs=[
            pl.BlockSpec((1, tq, LANES), lambda bi, hp, i: (bi, i, hp)),
            pl.BlockSpec((1, tq, LANES), lambda bi, hp, i: (bi, i, pairs + hp)),
            pl.BlockSpec((1, tq, LANES), lambda bi, hp, i: (bi, i, 2 * pairs + hp)),
            pl.BlockSpec((1, p, LANES), lambda bi, hp, i: (bi, 0, k_off + hp)),
            pl.BlockSpec((1, p, LANES), lambda bi, hp, i: (bi, 0, v_off + hp)),
        ],
        out_specs=pl.BlockSpec((1, tq, LANES), lambda bi, hp, i: (bi, i, hp)),
        out_shape=jax.ShapeDtypeStruct((b, t, sb_width), F32),
        scratch_shapes=[pltpu.VMEM((tq, LANES), F32), pltpu.VMEM((tq, LANES), F32)],
        compiler_params=pltpu.CompilerParams(
            dimension_semantics=("arbitrary", "arbitrary", "arbitrary"),
            vmem_limit_bytes=VMEM_LIMIT),
        name="stick_breaking",
    )(qkv, qkv, qkv, k_past, v_past)


def _silu(g):
    return g / (1.0 + jnp.exp(-g))


def _epilogue_kernel(x_ref, osb_ref, rest_ref, prev_ref, hist0_ref, mk_ref, mv_ref,
                     poolw_ref, pscale_ref, wout_ref, gfin_ref, y_ref,
                     *, tt, start, sb_width, pool_width, xa_width, xa_heads, apply_final):
    i = pl.program_id(1)
    s, pw, xw = sb_width, pool_width, xa_width
    g_sb = rest_ref[0, :, 0:s]
    u = rest_ref[0, :, s:s + pw]
    g_pool = rest_ref[0, :, s + pw:s + 2 * pw]
    q_xa = rest_ref[0, :, s + 2 * pw:s + 2 * pw + xw]
    g_xa = rest_ref[0, :, s + 2 * pw + xw:s + 2 * pw + 2 * xw]

    hist = jnp.where(i == 0, hist0_ref[0], prev_ref[0])
    ext = jnp.concatenate([hist, u], axis=0)
    sums = {1: ext}
    width = 1
    while width < max(POOL_WINDOWS):
        prev = sums[width]
        sums[2 * width] = prev + pltpu.roll(prev, width, 0)
        width *= 2
    lane = lax.broadcasted_iota(jnp.int32, (tt, pw), 1)
    group = lane // (pw // len(POOL_WINDOWS))
    pos = start + i * tt + lax.broadcasted_iota(jnp.int32, (tt, pw), 0)
    win_sum = jnp.zeros((tt, pw), F32)
    win = jnp.zeros((tt, pw), jnp.int32)
    for gi, w in enumerate(POOL_WINDOWS):
        sel = group == gi
        win_sum = jnp.where(sel, sums[w][HIST_ROWS:], win_sum)
        win = jnp.where(sel, w, win)
    cnt = jnp.minimum(pos + 1, win).astype(F32)
    pooled = win_sum / cnt - u
    o_pool = jnp.dot(pooled.astype(BF16), poolw_ref[...],
                     preferred_element_type=F32) * pscale_ref[...]

    mk = mk_ref[0].astype(BF16)
    mv = mv_ref[0].astype(BF16)
    xa_dim = xw // xa_heads
    lane_x = lax.broadcasted_iota(jnp.int32, (tt, xw), 1)
    head_x = lane_x // xa_dim
    qs = q_xa * (xa_dim ** -0.5)
    o_xa = jnp.zeros((tt, xw), F32)
    for hh in range(xa_heads):
        sel = head_x == hh
        qh = jnp.where(sel, qs, 0.0).astype(BF16)
        sc = lax.dot_general(qh, mk, (((1,), (1,)), ((), ())), preferred_element_type=F32)
        e = jnp.exp(sc - jnp.max(sc, axis=-1, keepdims=True))
        p = e / jnp.sum(e, axis=-1, keepdims=True)
        oh = jnp.dot(p.astype(BF16), mv, preferred_element_type=F32)
        o_xa = jnp.where(sel, oh, o_xa)

    m_sb = (osb_ref[0] * _silu(g_sb)).astype(BF16)
    m_pool = (o_pool * _silu(g_pool)).astype(BF16)
    m_xa = (o_xa * _silu(g_xa)).astype(BF16)
    y = (x_ref[0]
         + jnp.dot(m_sb, wout_ref[0:s, :], preferred_element_type=F32)
         + jnp.dot(m_pool, wout_ref[s:s + pw, :], preferred_element_type=F32)
         + jnp.dot(m_xa, wout_ref[s + pw:s + pw + xw, :], preferred_element_type=F32))
    if apply_final:
        y = _rms(y, gfin_ref[...])
    y_ref[0] = y


def _epilogue(x, o_sb, rest, hist0, mk, mv, pool_bd, pool_scale, w_out_bf16, g_final,
              *, tt, start, sb_width, pool_width, xa_width, xa_heads, apply_final):
    b, t, d = x.shape
    n_rest = rest.shape[-1]
    n_mem = mk.shape[1]
    hist_blocks = tt // HIST_ROWS
    u_col = sb_width // pool_width
    kern = functools.partial(
        _epilogue_kernel, tt=tt, start=start, sb_width=sb_width, pool_width=pool_width,
        xa_width=xa_width, xa_heads=xa_heads, apply_final=apply_final)
    return pl.pallas_call(
        kern,
        grid=(b, t // tt),
        in_specs=[
            pl.BlockSpec((1, tt, d), lambda bi, i: (bi, i, 0)),
            pl.BlockSpec((1, tt, sb_width), lambda bi, i: (bi, i, 0)),
            pl.BlockSpec((1, tt, n_rest), lambda bi, i: (bi, i, 0)),
            pl.BlockSpec((1, HIST_ROWS, pool_width),
                         lambda bi, i: (bi, jnp.maximum(i * hist_blocks - 1, 0), u_col)),
            pl.BlockSpec((1, HIST_ROWS, pool_width), lambda bi, i: (bi, 0, 0)),
            pl.BlockSpec((1, n_mem, xa_width), lambda bi, i: (bi, 0, 0)),
            pl.BlockSpec((1, n_mem, xa_width), lambda bi, i: (bi, 0, 0)),
            pl.BlockSpec((pool_width, pool_width), lambda bi, i: (0, 0)),
            pl.BlockSpec((1, pool_width), lambda bi, i: (0, 0)),
            pl.BlockSpec((d, d), lambda bi, i: (0, 0)),
            pl.BlockSpec((1, d), lambda bi, i: (0, 0)),
        ],
        out_specs=pl.BlockSpec((1, tt, d), lambda bi, i: (bi, i, 0)),
        out_shape=jax.ShapeDtypeStruct((b, t, d), F32),
        compiler_params=pltpu.CompilerParams(
            dimension_semantics=("arbitrary", "arbitrary"), vmem_limit_bytes=VMEM_LIMIT),
        name="epilogue",
    )(x, o_sb, rest, rest, hist0, mk, mv, pool_bd, pool_scale, w_out_bf16, g_final)


def _block_diag(pool_w):
    g, c, _ = pool_w.shape
    eye = jnp.eye(g, dtype=pool_w.dtype)
    return (eye[:, None, :, None] * pool_w[:, :, None, :]).reshape(g * c, g * c)


def _pad_hist(hist):
    return jnp.pad(hist, ((0, 0), (HIST_ROWS - hist.shape[1], 0), (0, 0)))


def _mixer_layer(x, start, k_past, v_past, pool_hist, mk, mv, g_norm, w_in_bf16, pool_bd,
                 pool_scale, w_out_bf16, g_final, *, dims, tiles, apply_final):
    sb_width, pool_width, xa_width, xa_heads = dims
    tm, tq, tk, tt = tiles
    b, t, d = x.shape
    k2d, v2d, qkv, rest = _project(x.reshape(b * t, d), g_norm, w_in_bf16, sb_width, tm)
    qkv = qkv.reshape(b, t, -1)
    rest = rest.reshape(b, t, -1)
    if k_past is None:
        o_sb = _stick_breaking_self(qkv, tq=tq, sb_width=sb_width)
    else:
        o_sb = _stick_breaking_cached(qkv, k_past, v_past, tk=tk)
    y = _epilogue(x, o_sb, rest, _pad_hist(pool_hist), mk, mv, pool_bd, pool_scale,
                  w_out_bf16, g_final, tt=tt, start=start, sb_width=sb_width,
                  pool_width=pool_width, xa_width=xa_width, xa_heads=xa_heads,
                  apply_final=apply_final)
    u_pool = rest[:, :, sb_width:sb_width + pool_width]
    new_hist = jnp.concatenate([pool_hist, u_pool], axis=1)[:, -POOL_STATE:]
    return y, k2d, v2d, new_hist


def kernel(x_prompt, x_sample, cache_sb_k, cache_sb_v, state_pool, cache_mem_k, cache_mem_v,
           mem_prompt, g_norm, w_in, pool_w, pool_scale, g_mem, w_mem_kv, w_out, g_final):
    depth = g_norm.shape[0]
    bp, tp, d = x_prompt.shape
    bs, ts, _ = x_sample.shape
    past = cache_sb_k.shape[2]
    sb_heads, head_dim = cache_sb_k.shape[3], cache_sb_k.shape[4]
    assert head_dim == HEAD_DIM
    sb_width = sb_heads * head_dim
    pool_width = pool_w.shape[1] * pool_w.shape[2]
    n_mem, xa_heads, xa_dim = cache_mem_k.shape[2:]
    xa_width = xa_heads * xa_dim
    dims = (sb_width, pool_width, xa_width, xa_heads)
    g_fin = g_final.reshape(1, d)

    yp, ys = x_prompt, x_sample
    kp_l, vp_l, hp_l, mkp_l, mvp_l, ks_l, vs_l, hs_l = [], [], [], [], [], [], [], []
    for l in range(depth):
        final = l == depth - 1
        g_l = g_norm[l].reshape(1, d)
        w_in_b = w_in[l].astype(BF16)
        w_out_b = w_out[l].astype(BF16)
        pool_bd = _block_diag(pool_w[l]).astype(BF16)
        p_scale = pool_scale[l].reshape(1, pool_width)
        mk, mv = _memory_kv(mem_prompt.reshape(bp * n_mem, d), g_mem[l].reshape(1, d),
                            w_mem_kv[l].astype(BF16), tm=256)
        mk = mk.reshape(bp, n_mem, xa_width)
        mv = mv.reshape(bp, n_mem, xa_width)
        hist0 = jnp.zeros((bp, POOL_STATE, pool_width), x_prompt.dtype)
        yp, kp, vp, hp = _mixer_layer(
            yp, 0, None, None, hist0, mk, mv, g_l, w_in_b, pool_bd, p_scale, w_out_b, g_fin,
            dims=dims, tiles=(512, 256, 256, 256), apply_final=final)
        ys, kn, vn, hn = _mixer_layer(
            ys, past, jnp.transpose(cache_sb_k[l], (0, 2, 3, 1)),
            jnp.transpose(cache_sb_v[l], (0, 2, 3, 1)), state_pool[l],
            cache_mem_k[l].reshape(bs, n_mem, xa_width),
            cache_mem_v[l].reshape(bs, n_mem, xa_width),
            g_l, w_in_b, pool_bd, p_scale, w_out_b, g_fin,
            dims=dims, tiles=(512, ts, 256, ts), apply_final=final)
        kp_l.append(kp.reshape(bp, tp, sb_heads, head_dim))
        vp_l.append(vp.reshape(bp, tp, sb_heads, head_dim))
        hp_l.append(hp)
        mkp_l.append(mk.reshape(bp, n_mem, xa_heads, xa_dim))
        mvp_l.append(mv.reshape(bp, n_mem, xa_heads, xa_dim))
        ks_l.append(kn.reshape(bs, ts, sb_heads, head_dim))
        vs_l.append(vn.reshape(bs, ts, sb_heads, head_dim))
        hs_l.append(hn)
    stack = lambda xs: jnp.stack(xs, axis=0)
    return (yp, ys, stack(kp_l), stack(vp_l), stack(hp_l), stack(mkp_l), stack(mvp_l),
            stack(ks_l), stack(vs_l), stack(hs_l))
```

```python
import functools

import jax
import jax.numpy as jnp
from jax import lax
from jax.experimental import pallas as pl
from jax.experimental.pallas import tpu as pltpu

F32 = jnp.float32
BF16 = jnp.bfloat16

HEAD_DIM = 64
LANES = 128
POOL_WINDOWS = (2, 4, 8, 16)
POOL_STATE = max(POOL_WINDOWS) - 1
HIST_ROWS = 16
EPS = 1e-6
DEAD_CARRY = 104.0
VMEM_LIMIT = 56 * 1024 * 1024
NT_DIMS = (((1,), (1,)), ((), ()))


def _rms(x, g):
    ms = jnp.mean(x * x, axis=-1, keepdims=True)
    return (x * lax.rsqrt(ms + EPS)) * g


def _proj_kernel(x_ref, g_ref, w_ref, wkv_t_ref, k_ref, v_ref, qkv_ref, rest_ref,
                 *, sb_width, time_minor):
    h = _rms(x_ref[...], g_ref[...]).astype(BF16)
    s = sb_width
    q = jnp.dot(h, w_ref[:, 0:s], preferred_element_type=F32)
    qkv_ref[:, 0:s] = (q * (HEAD_DIM ** -0.5)).astype(BF16)
    k = jnp.dot(h, w_ref[:, s:2 * s], preferred_element_type=F32)
    qkv_ref[:, s:2 * s] = k.astype(BF16)
    v = jnp.dot(h, w_ref[:, 2 * s:3 * s], preferred_element_type=F32)
    qkv_ref[:, 2 * s:3 * s] = v.astype(BF16)
    if time_minor:
        k_ref[0] = lax.dot_general(wkv_t_ref[0:s, :], h, NT_DIMS, preferred_element_type=F32)
        v_ref[0] = lax.dot_general(wkv_t_ref[s:2 * s, :], h, NT_DIMS,
                                   preferred_element_type=F32)
    else:
        k_ref[...] = k
        v_ref[...] = v
    rest_ref[...] = jnp.dot(h, w_ref[:, 3 * s:], preferred_element_type=F32)


def _project(x2d, g, w_bf16, wkv_t, sb_width, tm, rows_per_stream):
    m, d = x2d.shape
    n = w_bf16.shape[1]
    n_rest = n - 3 * sb_width
    time_minor = rows_per_stream % tm == 0
    if time_minor:
        nt = rows_per_stream // tm
        kv_spec = pl.BlockSpec((1, sb_width, tm), lambda i: (i // nt, 0, i % nt))
        kv_shape = jax.ShapeDtypeStruct((m // rows_per_stream, sb_width, rows_per_stream), F32)
    else:
        kv_spec = pl.BlockSpec((tm, sb_width), lambda i: (i, 0))
        kv_shape = jax.ShapeDtypeStruct((m, sb_width), F32)
    return pl.pallas_call(
        functools.partial(_proj_kernel, sb_width=sb_width, time_minor=time_minor),
        grid=(m // tm,),
        in_specs=[
            pl.BlockSpec((tm, d), lambda i: (i, 0)),
            pl.BlockSpec((1, d), lambda i: (0, 0)),
            pl.BlockSpec((d, n), lambda i: (0, 0)),
            pl.BlockSpec((2 * sb_width, d), lambda i: (0, 0)),
        ],
        out_specs=[
            kv_spec,
            kv_spec,
            pl.BlockSpec((tm, 3 * sb_width), lambda i: (i, 0)),
            pl.BlockSpec((tm, n_rest), lambda i: (i, 0)),
        ],
        out_shape=[
            kv_shape,
            kv_shape,
            jax.ShapeDtypeStruct((m, 3 * sb_width), BF16),
            jax.ShapeDtypeStruct((m, n_rest), F32),
        ],
        compiler_params=pltpu.CompilerParams(
            dimension_semantics=("arbitrary",), vmem_limit_bytes=VMEM_LIMIT),
        name="proj",
    )(x2d, g, w_bf16, wkv_t)


def _memkv_kernel(x_ref, g_ref, w_ref, mk_ref, mv_ref, *, xa_width):
    h = _rms(x_ref[...], g_ref[...]).astype(BF16)
    kv = jnp.dot(h, w_ref[...], preferred_element_type=F32)
    mk_ref[...] = kv[:, :xa_width]
    mv_ref[...] = kv[:, xa_width:]


def _memory_kv(mem2d, g, w_bf16, tm):
    m, d = mem2d.shape
    xa_width = w_bf16.shape[1] // 2
    return pl.pallas_call(
        functools.partial(_memkv_kernel, xa_width=xa_width),
        grid=(m // tm,),
        in_specs=[
            pl.BlockSpec((tm, d), lambda i: (i, 0)),
            pl.BlockSpec((1, d), lambda i: (0, 0)),
            pl.BlockSpec((d, 2 * xa_width), lambda i: (0, 0)),
        ],
        out_specs=[
            pl.BlockSpec((tm, xa_width), lambda i: (i, 0)),
            pl.BlockSpec((tm, xa_width), lambda i: (i, 0)),
        ],
        out_shape=[
            jax.ShapeDtypeStruct((m, xa_width), F32),
            jax.ShapeDtypeStruct((m, xa_width), F32),
        ],
        compiler_params=pltpu.CompilerParams(
            dimension_semantics=("arbitrary",), vmem_limit_bytes=VMEM_LIMIT),
        name="memkv",
    )(mem2d, g, w_bf16)


def _strict_upper(n):
    r = lax.broadcasted_iota(jnp.int32, (n, n), 0)
    c = lax.broadcasted_iota(jnp.int32, (n, n), 1)
    return jnp.where(r > c, 1.0, 0.0).astype(BF16)


def _softplus(z):
    return jnp.maximum(z, 0.0) + jnp.log(1.0 + jnp.exp(-jnp.abs(z)))


def _sum_later(sp, upper):
    hi = sp.astype(BF16)
    lo = (sp - hi.astype(F32)).astype(BF16)
    return jnp.dot(jnp.concatenate([hi, lo], axis=1), jnp.concatenate([upper, upper], axis=0),
                   preferred_element_type=F32)


def _sb_block(qh, kblk, vblk, upper, carry, mask, transposed=False):
    if transposed:
        z = jnp.dot(qh, kblk, preferred_element_type=F32)
    else:
        z = lax.dot_general(qh, kblk, NT_DIMS, preferred_element_type=F32)
    sp = _softplus(z)
    if mask is not None:
        sp = jnp.where(mask, sp, 0.0)
    later = _sum_later(sp, upper)
    w = jnp.exp((z - sp) - later - carry)
    if mask is not None:
        w = jnp.where(mask, w, 0.0)
    if transposed:
        out = lax.dot_general(w.astype(BF16), vblk, NT_DIMS, preferred_element_type=F32)
    else:
        out = jnp.dot(w.astype(BF16), vblk, preferred_element_type=F32)
    return out, carry + jnp.sum(sp, axis=-1, keepdims=True)


def _sb_kernel(q_ref, kd_ref, vd_ref, kp_ref, vp_ref, o_ref, acc0_ref, acc1_ref,
               *, tq, tk, n_past_static):
    q = q_ref[0]
    lane = lax.broadcasted_iota(jnp.int32, (tq, LANES), 1)
    first = lane < HEAD_DIM
    zero = jnp.zeros_like(q)
    q0 = jnp.where(first, q, zero)
    q1 = jnp.where(first, zero, q)

    r = lax.broadcasted_iota(jnp.int32, (tq, tq), 0)
    c = lax.broadcasted_iota(jnp.int32, (tq, tq), 1)
    causal = c < r
    upper_d = _strict_upper(tq)
    kd = kd_ref[0]
    vd = vd_ref[0]
    zc = jnp.zeros((tq, 1), F32)
    o0, c0 = _sb_block(q0, kd, vd, upper_d, zc, causal)
    o1, c1 = _sb_block(q1, kd, vd, upper_d, zc, causal)
    acc0_ref[...] = o0
    acc1_ref[...] = o1

    if n_past_static is None:
        n_past = pl.program_id(2) * (tq // tk)
    else:
        n_past = n_past_static
    upper = _strict_upper(tk)

    def live(state):
        j, c0, c1 = state
        return jnp.logical_and(j < n_past, jnp.min(jnp.minimum(c0, c1)) <= DEAD_CARRY)

    def body(state):
        j, c0, c1 = state
        start = pl.multiple_of((n_past - 1 - j) * tk, tk)
        kblk = kp_ref[0, pl.ds(start, tk), :].astype(BF16)
        vblk = vp_ref[0, pl.ds(start, tk), :].astype(BF16)
        o0, c0 = _sb_block(q0, kblk, vblk, upper, c0, None)
        o1, c1 = _sb_block(q1, kblk, vblk, upper, c1, None)
        acc0_ref[...] += o0
        acc1_ref[...] += o1
        return j + 1, c0, c1

    lax.while_loop(live, body, (jnp.int32(0), c0, c1))
    o_ref[0] = jnp.where(first, acc0_ref[...], acc1_ref[...])


def _sb_window_kernel(q_ref, k_ref, v_ref, o_ref, acc_ref, *, tq):
    i = pl.program_id(2)
    q = q_ref[0]
    lane = lax.broadcasted_iota(jnp.int32, (tq, LANES), 1)
    first = lane < HEAD_DIM
    zero = jnp.zeros_like(q)
    heads = (jnp.where(first, q, zero), jnp.where(first, zero, q))
    r = lax.broadcasted_iota(jnp.int32, (tq, tq), 0)
    c = lax.broadcasted_iota(jnp.int32, (tq, tq), 1)
    causal = c < r
    upper = _strict_upper(tq)
    nt = (((1,), (1,)), ((), ()))

    @pl.when(i == 0)
    def _():
        kd = k_ref[0, 0:tq, :]
        vd = v_ref[0, 0:tq, :]
        zc = jnp.zeros((tq, 1), F32)
        for h, qh in enumerate(heads):
            o, _ = _sb_block(qh, kd, vd, upper, zc, causal)
            acc_ref[h] = o

    @pl.when(i > 0)
    def _():
        ws = pl.multiple_of((i - 1) * tq, tq)
        kwin = k_ref[0, pl.ds(ws, 2 * tq), :]
        vwin = v_ref[0, pl.ds(ws, 2 * tq), :]
        zs = [lax.dot_general(qh, kwin, nt, preferred_element_type=F32) for qh in heads]
        zp = [z[:, :tq] for z in zs]
        zd = [z[:, tq:] for z in zs]
        sp_p = [_softplus(z) for z in zp]
        sp_d = [jnp.where(causal, _softplus(z), 0.0) for z in zd]
        later = _sum_later(jnp.concatenate(sp_p + sp_d, axis=0), upper)
        n = len(heads)
        carries = []
        for h in range(n):
            rs_d = jnp.sum(sp_d[h], axis=-1, keepdims=True)
            later_p = later[h * tq:(h + 1) * tq]
            later_d = later[(n + h) * tq:(n + h + 1) * tq]
            w_p = jnp.exp((zp[h] - sp_p[h]) - later_p - rs_d)
            w_d = jnp.where(causal, jnp.exp((zd[h] - sp_d[h]) - later_d), 0.0)
            w = jnp.concatenate([w_p.astype(BF16), w_d.astype(BF16)], axis=1)
            acc_ref[h] = jnp.dot(w, vwin, preferred_element_type=F32)
            carries.append(rs_d + jnp.sum(sp_p[h], axis=-1, keepdims=True))

        n_older = i - 1

        def live(state):
            j, c0, c1 = state
            return jnp.logical_and(j < n_older, jnp.min(jnp.minimum(c0, c1)) <= DEAD_CARRY)

        def body(state):
            j, c0, c1 = state
            start = pl.multiple_of((n_older - 1 - j) * tq, tq)
            kblk = k_ref[0, pl.ds(start, tq), :]
            vblk = v_ref[0, pl.ds(start, tq), :]
            o0, c0 = _sb_block(heads[0], kblk, vblk, upper, c0, None)
            o1, c1 = _sb_block(heads[1], kblk, vblk, upper, c1, None)
            acc_ref[0] += o0
            acc_ref[1] += o1
            return j + 1, c0, c1

        lax.while_loop(live, body, (jnp.int32(0), carries[0], carries[1]))

    o_ref[0] = jnp.where(first, acc_ref[0], acc_ref[1])


def _stick_breaking_self(qkv, *, tq, sb_width):
    b, t, _ = qkv.shape
    pairs = sb_width // LANES
    return pl.pallas_call(
        functools.partial(_sb_window_kernel, tq=tq),
        grid=(b, pairs, t // tq),
        in_specs=[
            pl.BlockSpec((1, tq, LANES), lambda bi, hp, i: (bi, i, hp)),
            pl.BlockSpec((1, t, LANES), lambda bi, hp, i: (bi, 0, pairs + hp)),
            pl.BlockSpec((1, t, LANES), lambda bi, hp, i: (bi, 0, 2 * pairs + hp)),
        ],
        out_specs=pl.BlockSpec((1, tq, LANES), lambda bi, hp, i: (bi, i, hp)),
        out_shape=jax.ShapeDtypeStruct((b, t, sb_width), F32),
        scratch_shapes=[pltpu.VMEM((2, tq, LANES), F32)],
        compiler_params=pltpu.CompilerParams(
            dimension_semantics=("arbitrary", "arbitrary", "arbitrary"),
            vmem_limit_bytes=VMEM_LIMIT),
        name="stick_breaking_self",
    )(qkv, qkv, qkv)


def _sb_cached_kernel(qkv_ref, kwin_ref, vwin_ref, kc_ref, vc_ref, o_ref,
                      kbuf, vbuf, acc_ref, sem, *, heads, tk, n_older):
    b = pl.program_id(0)
    tq = qkv_ref.shape[1]
    s = heads * HEAD_DIM
    nt = (((1,), (1,)), ((), ()))
    r = lax.broadcasted_iota(jnp.int32, (tq, tq), 0)
    c = lax.broadcasted_iota(jnp.int32, (tq, tq), 1)
    causal = c < r
    upper_d = _strict_upper(tq)
    upper = _strict_upper(tk)
    q = qkv_ref[0, :, 0:s]
    kn = qkv_ref[0, :, s:2 * s]
    vn = qkv_ref[0, :, 2 * s:3 * s]
    head = lambda x, h: x[:, h * HEAD_DIM:(h + 1) * HEAD_DIM]
    qs = [head(q, h) for h in range(heads)]

    zd = [lax.dot_general(qs[h], head(kn, h), nt, preferred_element_type=F32)
          for h in range(heads)]
    zw = [jnp.dot(qs[h], kwin_ref[0, h].astype(BF16), preferred_element_type=F32)
          for h in range(heads)]
    sp_d = [jnp.where(causal, _softplus(z), 0.0) for z in zd]
    sp_w = [_softplus(z) for z in zw]
    later_d = _sum_later(jnp.concatenate(sp_d, axis=0), upper_d)
    later_w = _sum_later(jnp.concatenate(sp_w, axis=0), upper)
    carries = []
    for h in range(heads):
        rows = slice(h * tq, (h + 1) * tq)
        rs_d = jnp.sum(sp_d[h], axis=-1, keepdims=True)
        w_d = jnp.where(causal, jnp.exp((zd[h] - sp_d[h]) - later_d[rows]), 0.0)
        w_w = jnp.exp((zw[h] - sp_w[h]) - later_w[rows] - rs_d)
        acc_ref[h] = (jnp.dot(w_d.astype(BF16), head(vn, h), preferred_element_type=F32)
                      + lax.dot_general(w_w.astype(BF16), vwin_ref[0, h].astype(BF16), NT_DIMS,
                                        preferred_element_type=F32))
        carries.append(rs_d + jnp.sum(sp_w[h], axis=-1, keepdims=True))

    def block_copies(j):
        start = pl.multiple_of((n_older - 1 - j) * tk, tk)
        return (pltpu.make_async_copy(kc_ref.at[b, :, :, pl.ds(start, tk)], kbuf, sem.at[0]),
                pltpu.make_async_copy(vc_ref.at[b, :, :, pl.ds(start, tk)], vbuf, sem.at[1]))

    def live(state):
        lowest = functools.reduce(jnp.minimum, state[1:])
        return jnp.logical_and(state[0] < n_older, jnp.min(lowest) <= DEAD_CARRY)

    def body(state):
        j = state[0]
        copies = block_copies(j)
        for cp in copies:
            cp.start()
        for cp in copies:
            cp.wait()
        new = []
        for h in range(heads):
            o, ch = _sb_block(qs[h], kbuf[h].astype(BF16), vbuf[h].astype(BF16),
                              upper, state[1 + h], None, transposed=True)
            acc_ref[h] += o
            new.append(ch)
        return (j + 1, *new)

    lax.while_loop(live, body, (jnp.int32(0), *carries))
    o_ref[0] = jnp.concatenate([acc_ref[h] for h in range(heads)], axis=1)


def _stick_breaking_cached(qkv, cache_k, cache_v, *, tk):
    b, t, _ = qkv.shape
    _, heads, dh, p = cache_k.shape
    s = heads * dh
    last = p // tk - 1
    win = pl.BlockSpec((1, heads, dh, tk), lambda bi: (bi, 0, 0, last))
    return pl.pallas_call(
        functools.partial(_sb_cached_kernel, heads=heads, tk=tk, n_older=last),
        grid=(b,),
        in_specs=[
            pl.BlockSpec((1, t, 3 * s), lambda bi: (bi, 0, 0)),
            win, win,
            pl.BlockSpec(memory_space=pl.ANY),
            pl.BlockSpec(memory_space=pl.ANY),
        ],
        out_specs=pl.BlockSpec((1, t, s), lambda bi: (bi, 0, 0)),
        out_shape=jax.ShapeDtypeStruct((b, t, s), F32),
        scratch_shapes=[
            pltpu.VMEM((heads, dh, tk), F32),
            pltpu.VMEM((heads, dh, tk), F32),
            pltpu.VMEM((heads, t, dh), F32),
            pltpu.SemaphoreType.DMA((2,)),
        ],
        compiler_params=pltpu.CompilerParams(
            dimension_semantics=("arbitrary",), vmem_limit_bytes=VMEM_LIMIT),
        name="stick_breaking_cached",
    )(qkv, cache_k, cache_v, cache_k, cache_v)


def _stick_breaking(qkv, k_past, v_past, *, tq, tk, sb_width):
    b, t, _ = qkv.shape
    pairs = sb_width // LANES
    if k_past is None:
        k_past, v_past = qkv, qkv
        k_off, v_off = pairs, 2 * pairs
        n_past_static = None
    else:
        k_off, v_off = 0, 0
        n_past_static = k_past.shape[1] // tk
    p = k_past.shape[1]
    return pl.pallas_call(
        functools.partial(_sb_kernel, tq=tq, tk=tk, n_past_static=n_past_static),
        grid=(b, pairs, t // tq),
        in_specs=[
            pl.BlockSpec((1, tq, LANES), lambda bi, hp, i: (bi, i, hp)),
            pl.BlockSpec((1, tq, LANES), lambda bi, hp, i: (bi, i, pairs + hp)),
            pl.BlockSpec((1, tq, LANES), lambda bi, hp, i: (bi, i, 2 * pairs + hp)),
            pl.BlockSpec((1, p, LANES), lambda bi, hp, i: (bi, 0, k_off + hp)),
            pl.BlockSpec((1, p, LANES), lambda bi, hp, i: (bi, 0, v_off + hp)),
        ],
        out_specs=pl.BlockSpec((1, tq, LANES), lambda bi, hp, i: (bi, i, hp)),
        out_shape=jax.ShapeDtypeStruct((b, t, sb_width), F32),
        scratch_shapes=[pltpu.VMEM((tq, LANES), F32), pltpu.VMEM((tq, LANES), F32)],
        compiler_params=pltpu.CompilerParams(
            dimension_semantics=("arbitrary", "arbitrary", "arbitrary"),
            vmem_limit_bytes=VMEM_LIMIT),
        name="stick_breaking",
    )(qkv, qkv, qkv, k_past, v_past)


def _silu(g):
    return g / (1.0 + jnp.exp(-g))


def _epilogue_kernel(x_ref, osb_ref, rest_ref, prev_ref, hist0_ref, mk_ref, mv_ref,
                     poolw_ref, pscale_ref, wout_ref, gfin_ref, y_ref,
                     *, tt, start, sb_width, pool_width, xa_width, xa_heads, apply_final):
    i = pl.program_id(1)
    s, pw, xw = sb_width, pool_width, xa_width
    g_sb = rest_ref[0, :, 0:s]
    u = rest_ref[0, :, s:s + pw]
    g_pool = rest_ref[0, :, s + pw:s + 2 * pw]
    q_xa = rest_ref[0, :, s + 2 * pw:s + 2 * pw + xw]
    g_xa = rest_ref[0, :, s + 2 * pw + xw:s + 2 * pw + 2 * xw]

    hist = jnp.where(i == 0, hist0_ref[0], prev_ref[0])
    ext = jnp.concatenate([hist, u], axis=0)
    sums = {1: ext}
    width = 1
    while width < max(POOL_WINDOWS):
        prev = sums[width]
        sums[2 * width] = prev + pltpu.roll(prev, width, 0)
        width *= 2
    lane = lax.broadcasted_iota(jnp.int32, (tt, pw), 1)
    group = lane // (pw // len(POOL_WINDOWS))
    pos = start + i * tt + lax.broadcasted_iota(jnp.int32, (tt, pw), 0)
    win_sum = jnp.zeros((tt, pw), F32)
    win = jnp.zeros((tt, pw), jnp.int32)
    for gi, w in enumerate(POOL_WINDOWS):
        sel = group == gi
        win_sum = jnp.where(sel, sums[w][HIST_ROWS:], win_sum)
        win = jnp.where(sel, w, win)
    cnt = jnp.minimum(pos + 1, win).astype(F32)
    pooled = win_sum / cnt - u
    o_pool = jnp.dot(pooled.astype(BF16), poolw_ref[...],
                     preferred_element_type=F32) * pscale_ref[...]

    mk = mk_ref[0].astype(BF16)
    mv = mv_ref[0].astype(BF16)
    xa_dim = xw // xa_heads
    lane_x = lax.broadcasted_iota(jnp.int32, (tt, xw), 1)
    head_x = lane_x // xa_dim
    qs = q_xa * (xa_dim ** -0.5)
    o_xa = jnp.zeros((tt, xw), F32)
    for hh in range(xa_heads):
        sel = head_x == hh
        qh = jnp.where(sel, qs, 0.0).astype(BF16)
        sc = lax.dot_general(qh, mk, (((1,), (1,)), ((), ())), preferred_element_type=F32)
        e = jnp.exp(sc - jnp.max(sc, axis=-1, keepdims=True))
        p = e / jnp.sum(e, axis=-1, keepdims=True)
        oh = jnp.dot(p.astype(BF16), mv, preferred_element_type=F32)
        o_xa = jnp.where(sel, oh, o_xa)

    m_sb = (osb_ref[0] * _silu(g_sb)).astype(BF16)
    m_pool = (o_pool * _silu(g_pool)).astype(BF16)
    m_xa = (o_xa * _silu(g_xa)).astype(BF16)
    y = (x_ref[0]
         + jnp.dot(m_sb, wout_ref[0:s, :], preferred_element_type=F32)
         + jnp.dot(m_pool, wout_ref[s:s + pw, :], preferred_element_type=F32)
         + jnp.dot(m_xa, wout_ref[s + pw:s + pw + xw, :], preferred_element_type=F32))
    if apply_final:
        y = _rms(y, gfin_ref[...])
    y_ref[0] = y


def _epilogue(x, o_sb, rest, hist0, mk, mv, pool_bd, pool_scale, w_out_bf16, g_final,
              *, tt, start, sb_width, pool_width, xa_width, xa_heads, apply_final):
    b, t, d = x.shape
    n_rest = rest.shape[-1]
    n_mem = mk.shape[1]
    hist_blocks = tt // HIST_ROWS
    u_col = sb_width // pool_width
    kern = functools.partial(
        _epilogue_kernel, tt=tt, start=start, sb_width=sb_width, pool_width=pool_width,
        xa_width=xa_width, xa_heads=xa_heads, apply_final=apply_final)
    return pl.pallas_call(
        kern,
        grid=(b, t // tt),
        in_specs=[
            pl.BlockSpec((1, tt, d), lambda bi, i: (bi, i, 0)),
            pl.BlockSpec((1, tt, sb_width), lambda bi, i: (bi, i, 0)),
            pl.BlockSpec((1, tt, n_rest), lambda bi, i: (bi, i, 0)),
            pl.BlockSpec((1, HIST_ROWS, pool_width),
                         lambda bi, i: (bi, jnp.maximum(i * hist_blocks - 1, 0), u_col)),
            pl.BlockSpec((1, HIST_ROWS, pool_width), lambda bi, i: (bi, 0, 0)),
            pl.BlockSpec((1, n_mem, xa_width), lambda bi, i: (bi, 0, 0)),
            pl.BlockSpec((1, n_mem, xa_width), lambda bi, i: (bi, 0, 0)),
            pl.BlockSpec((pool_width, pool_width), lambda bi, i: (0, 0)),
            pl.BlockSpec((1, pool_width), lambda bi, i: (0, 0)),
            pl.BlockSpec((d, d), lambda bi, i: (0, 0)),
            pl.BlockSpec((1, d), lambda bi, i: (0, 0)),
        ],
        out_specs=pl.BlockSpec((1, tt, d), lambda bi, i: (bi, i, 0)),
        out_shape=jax.ShapeDtypeStruct((b, t, d), F32),
        compiler_params=pltpu.CompilerParams(
            dimension_semantics=("arbitrary", "arbitrary"), vmem_limit_bytes=VMEM_LIMIT),
        name="epilogue",
    )(x, o_sb, rest, rest, hist0, mk, mv, pool_bd, pool_scale, w_out_bf16, g_final)


def _block_diag(pool_w):
    g, c, _ = pool_w.shape
    eye = jnp.eye(g, dtype=pool_w.dtype)
    return (eye[:, None, :, None] * pool_w[:, :, None, :]).reshape(g * c, g * c)


def _pad_hist(hist):
    return jnp.pad(hist, ((0, 0), (HIST_ROWS - hist.shape[1], 0), (0, 0)))


def _mixer_layer(x, start, k_past, v_past, pool_hist, mk, mv, g_norm, w_in_bf16, wkv_t, pool_bd,
                 pool_scale, w_out_bf16, g_final, *, dims, tiles, apply_final):
    sb_width, pool_width, xa_width, xa_heads = dims
    tm, tq, tk, tt = tiles
    b, t, d = x.shape
    heads = sb_width // HEAD_DIM
    k_new, v_new, qkv, rest = _project(x.reshape(b * t, d), g_norm, w_in_bf16, wkv_t,
                                       sb_width, tm, t)
    if k_new.ndim == 3:
        to_cache = lambda a: jnp.transpose(a.reshape(b, heads, HEAD_DIM, t), (0, 3, 1, 2))
    else:
        to_cache = lambda a: a.reshape(b, t, heads, HEAD_DIM)
    qkv = qkv.reshape(b, t, -1)
    rest = rest.reshape(b, t, -1)
    if k_past is None:
        o_sb = _stick_breaking_self(qkv, tq=tq, sb_width=sb_width)
    else:
        o_sb = _stick_breaking_cached(qkv, k_past, v_past, tk=tk)
    y = _epilogue(x, o_sb, rest, _pad_hist(pool_hist), mk, mv, pool_bd, pool_scale,
                  w_out_bf16, g_final, tt=tt, start=start, sb_width=sb_width,
                  pool_width=pool_width, xa_width=xa_width, xa_heads=xa_heads,
                  apply_final=apply_final)
    u_pool = rest[:, :, sb_width:sb_width + pool_width]
    new_hist = jnp.concatenate([pool_hist, u_pool], axis=1)[:, -POOL_STATE:]
    return y, to_cache(k_new), to_cache(v_new), new_hist


def kernel(x_prompt, x_sample, cache_sb_k, cache_sb_v, state_pool, cache_mem_k, cache_mem_v,
           mem_prompt, g_norm, w_in, pool_w, pool_scale, g_mem, w_mem_kv, w_out, g_final):
    depth = g_norm.shape[0]
    bp, tp, d = x_prompt.shape
    bs, ts, _ = x_sample.shape
    past = cache_sb_k.shape[2]
    sb_heads, head_dim = cache_sb_k.shape[3], cache_sb_k.shape[4]
    assert head_dim == HEAD_DIM
    sb_width = sb_heads * head_dim
    pool_width = pool_w.shape[1] * pool_w.shape[2]
    n_mem, xa_heads, xa_dim = cache_mem_k.shape[2:]
    xa_width = xa_heads * xa_dim
    dims = (sb_width, pool_width, xa_width, xa_heads)
    g_fin = g_final.reshape(1, d)

    yp, ys = x_prompt, x_sample
    kp_l, vp_l, hp_l, mkp_l, mvp_l, ks_l, vs_l, hs_l = [], [], [], [], [], [], [], []
    for l in range(depth):
        final = l == depth - 1
        g_l = g_norm[l].reshape(1, d)
        w_in_b = w_in[l].astype(BF16)
        w_out_b = w_out[l].astype(BF16)
        pool_bd = _block_diag(pool_w[l]).astype(BF16)
        p_scale = pool_scale[l].reshape(1, pool_width)
        mk, mv = _memory_kv(mem_prompt.reshape(bp * n_mem, d), g_mem[l].reshape(1, d),
                            w_mem_kv[l].astype(BF16), tm=256)
        mk = mk.reshape(bp, n_mem, xa_width)
        mv = mv.reshape(bp, n_mem, xa_width)
        hist0 = jnp.zeros((bp, POOL_STATE, pool_width), x_prompt.dtype)
        wkv_t = jnp.transpose(w_in[l][:, sb_width:3 * sb_width]).astype(BF16)
        yp, kp, vp, hp = _mixer_layer(
            yp, 0, None, None, hist0, mk, mv, g_l, w_in_b, wkv_t, pool_bd, p_scale, w_out_b,
            g_fin, dims=dims, tiles=(512, 256, 256, 256), apply_final=final)
        ys, kn, vn, hn = _mixer_layer(
            ys, past, jnp.transpose(cache_sb_k[l], (0, 2, 3, 1)),
            jnp.transpose(cache_sb_v[l], (0, 2, 3, 1)), state_pool[l],
            cache_mem_k[l].reshape(bs, n_mem, xa_width),
            cache_mem_v[l].reshape(bs, n_mem, xa_width),
            g_l, w_in_b, wkv_t, pool_bd, p_scale, w_out_b, g_fin,
            dims=dims, tiles=(512, ts, 256, ts), apply_final=final)
        kp_l.append(kp)
        vp_l.append(vp)
        hp_l.append(hp)
        mkp_l.append(mk.reshape(bp, n_mem, xa_heads, xa_dim))
        mvp_l.append(mv.reshape(bp, n_mem, xa_heads, xa_dim))
        ks_l.append(kn)
        vs_l.append(vn)
        hs_l.append(hn)
    stack = lambda xs: jnp.stack(xs, axis=0)
    return (yp, ys, stack(kp_l), stack(vp_l), stack(hp_l), stack(mkp_l), stack(mvp_l),
            stack(ks_l), stack(vs_l), stack(hs_l))
```

```python
import functools

import jax
import jax.numpy as jnp
from jax import lax
from jax.experimental import pallas as pl
from jax.experimental.pallas import tpu as pltpu

F32 = jnp.float32
BF16 = jnp.bfloat16

HEAD_DIM = 64
LANES = 128
POOL_WINDOWS = (2, 4, 8, 16)
POOL_STATE = max(POOL_WINDOWS) - 1
HIST_ROWS = 16
EPS = 1e-6
DEAD_CARRY = 104.0
VMEM_LIMIT = 56 * 1024 * 1024
NT_DIMS = (((1,), (1,)), ((), ()))
PAIRS_PER_STEP = 2


def _rms(x, g):
    ms = jnp.mean(x * x, axis=-1, keepdims=True)
    return (x * lax.rsqrt(ms + EPS)) * g


def _proj_kernel(x_ref, g_ref, w_ref, wkv_t_ref, k_ref, v_ref, qkv_ref, rest_ref,
                 *, sb_width, time_minor):
    h = _rms(x_ref[...], g_ref[...]).astype(BF16)
    s = sb_width
    q = jnp.dot(h, w_ref[:, 0:s], preferred_element_type=F32)
    qkv_ref[:, 0:s] = (q * (HEAD_DIM ** -0.5)).astype(BF16)
    k = jnp.dot(h, w_ref[:, s:2 * s], preferred_element_type=F32)
    qkv_ref[:, s:2 * s] = k.astype(BF16)
    v = jnp.dot(h, w_ref[:, 2 * s:3 * s], preferred_element_type=F32)
    qkv_ref[:, 2 * s:3 * s] = v.astype(BF16)
    if time_minor:
        k_ref[0] = lax.dot_general(wkv_t_ref[0:s, :], h, NT_DIMS, preferred_element_type=F32)
        v_ref[0] = lax.dot_general(wkv_t_ref[s:2 * s, :], h, NT_DIMS,
                                   preferred_element_type=F32)
    else:
        k_ref[...] = k
        v_ref[...] = v
    rest_ref[...] = jnp.dot(h, w_ref[:, 3 * s:], preferred_element_type=F32)


def _project(x2d, g, w_bf16, wkv_t, sb_width, tm, rows_per_stream):
    m, d = x2d.shape
    n = w_bf16.shape[1]
    n_rest = n - 3 * sb_width
    time_minor = rows_per_stream % tm == 0
    if time_minor:
        nt = rows_per_stream // tm
        kv_spec = pl.BlockSpec((1, sb_width, tm), lambda i: (i // nt, 0, i % nt))
        kv_shape = jax.ShapeDtypeStruct((m // rows_per_stream, sb_width, rows_per_stream), F32)
    else:
        kv_spec = pl.BlockSpec((tm, sb_width), lambda i: (i, 0))
        kv_shape = jax.ShapeDtypeStruct((m, sb_width), F32)
    return pl.pallas_call(
        functools.partial(_proj_kernel, sb_width=sb_width, time_minor=time_minor),
        grid=(m // tm,),
        in_specs=[
            pl.BlockSpec((tm, d), lambda i: (i, 0)),
            pl.BlockSpec((1, d), lambda i: (0, 0)),
            pl.BlockSpec((d, n), lambda i: (0, 0)),
            pl.BlockSpec((2 * sb_width, d), lambda i: (0, 0)),
        ],
        out_specs=[
            kv_spec,
            kv_spec,
            pl.BlockSpec((tm, 3 * sb_width), lambda i: (i, 0)),
            pl.BlockSpec((tm, n_rest), lambda i: (i, 0)),
        ],
        out_shape=[
            kv_shape,
            kv_shape,
            jax.ShapeDtypeStruct((m, 3 * sb_width), BF16),
            jax.ShapeDtypeStruct((m, n_rest), F32),
        ],
        compiler_params=pltpu.CompilerParams(
            dimension_semantics=("arbitrary",), vmem_limit_bytes=VMEM_LIMIT),
        name="proj",
    )(x2d, g, w_bf16, wkv_t)


def _memkv_kernel(x_ref, g_ref, w_ref, mk_ref, mv_ref, *, xa_width):
    h = _rms(x_ref[...], g_ref[...]).astype(BF16)
    kv = jnp.dot(h, w_ref[...], preferred_element_type=F32)
    mk_ref[...] = kv[:, :xa_width]
    mv_ref[...] = kv[:, xa_width:]


def _memory_kv(mem2d, g, w_bf16, tm):
    m, d = mem2d.shape
    xa_width = w_bf16.shape[1] // 2
    return pl.pallas_call(
        functools.partial(_memkv_kernel, xa_width=xa_width),
        grid=(m // tm,),
        in_specs=[
            pl.BlockSpec((tm, d), lambda i: (i, 0)),
            pl.BlockSpec((1, d), lambda i: (0, 0)),
            pl.BlockSpec((d, 2 * xa_width), lambda i: (0, 0)),
        ],
        out_specs=[
            pl.BlockSpec((tm, xa_width), lambda i: (i, 0)),
            pl.BlockSpec((tm, xa_width), lambda i: (i, 0)),
        ],
        out_shape=[
            jax.ShapeDtypeStruct((m, xa_width), F32),
            jax.ShapeDtypeStruct((m, xa_width), F32),
        ],
        compiler_params=pltpu.CompilerParams(
            dimension_semantics=("arbitrary",), vmem_limit_bytes=VMEM_LIMIT),
        name="memkv",
    )(mem2d, g, w_bf16)


def _strict_upper(n):
    r = lax.broadcasted_iota(jnp.int32, (n, n), 0)
    c = lax.broadcasted_iota(jnp.int32, (n, n), 1)
    return jnp.where(r > c, 1.0, 0.0).astype(BF16)


def _softplus(z):
    return jnp.maximum(z, 0.0) + jnp.log(1.0 + jnp.exp(-jnp.abs(z)))


def _sum_later(sp, upper):
    hi = sp.astype(BF16)
    lo = (sp - hi.astype(F32)).astype(BF16)
    return jnp.dot(jnp.concatenate([hi, lo], axis=1), jnp.concatenate([upper, upper], axis=0),
                   preferred_element_type=F32)


def _sb_block(qh, kblk, vblk, upper, carry, mask, transposed=False):
    if transposed:
        z = jnp.dot(qh, kblk, preferred_element_type=F32)
    else:
        z = lax.dot_general(qh, kblk, NT_DIMS, preferred_element_type=F32)
    sp = _softplus(z)
    if mask is not None:
        sp = jnp.where(mask, sp, 0.0)
    later = _sum_later(sp, upper)
    w = jnp.exp((z - sp) - later - carry)
    if mask is not None:
        w = jnp.where(mask, w, 0.0)
    if transposed:
        out = lax.dot_general(w.astype(BF16), vblk, NT_DIMS, preferred_element_type=F32)
    else:
        out = jnp.dot(w.astype(BF16), vblk, preferred_element_type=F32)
    return out, carry + jnp.sum(sp, axis=-1, keepdims=True)


def _still_live(state, n_blocks):
    lowest = functools.reduce(jnp.minimum, state[1:])
    return jnp.logical_and(state[0] < n_blocks, jnp.min(lowest) <= DEAD_CARRY)


def _sb_window_kernel(q_ref, k_ref, v_ref, o_ref, acc_ref, *, tq, pairs):
    i = pl.program_id(2)
    lane = lax.broadcasted_iota(jnp.int32, (tq, LANES), 1)
    first = lane < HEAD_DIM
    qs, cols = [], []
    for p in range(pairs):
        col = slice(p * LANES, (p + 1) * LANES)
        q = q_ref[0, :, col]
        zero = jnp.zeros_like(q)
        qs += [jnp.where(first, q, zero), jnp.where(first, zero, q)]
        cols += [col, col]
    n = len(qs)
    r = lax.broadcasted_iota(jnp.int32, (tq, tq), 0)
    c = lax.broadcasted_iota(jnp.int32, (tq, tq), 1)
    causal = c < r
    upper = _strict_upper(tq)

    @pl.when(i == 0)
    def _():
        zc = jnp.zeros((tq, 1), F32)
        for h in range(n):
            o, _ = _sb_block(qs[h], k_ref[0, 0:tq, cols[h]], v_ref[0, 0:tq, cols[h]],
                             upper, zc, causal)
            acc_ref[h] = o

    @pl.when(i > 0)
    def _():
        ws = pl.multiple_of((i - 1) * tq, tq)
        zs = [lax.dot_general(qs[h], k_ref[0, pl.ds(ws, 2 * tq), cols[h]], NT_DIMS,
                              preferred_element_type=F32) for h in range(n)]
        zp = [z[:, :tq] for z in zs]
        zd = [z[:, tq:] for z in zs]
        sp_p = [_softplus(z) for z in zp]
        sp_d = [jnp.where(causal, _softplus(z), 0.0) for z in zd]
        later = _sum_later(jnp.concatenate(sp_p + sp_d, axis=0), upper)
        carries = []
        for h in range(n):
            rs_d = jnp.sum(sp_d[h], axis=-1, keepdims=True)
            later_p = later[h * tq:(h + 1) * tq]
            later_d = later[(n + h) * tq:(n + h + 1) * tq]
            w_p = jnp.exp((zp[h] - sp_p[h]) - later_p - rs_d)
            w_d = jnp.where(causal, jnp.exp((zd[h] - sp_d[h]) - later_d), 0.0)
            w = jnp.concatenate([w_p.astype(BF16), w_d.astype(BF16)], axis=1)
            acc_ref[h] = jnp.dot(w, v_ref[0, pl.ds(ws, 2 * tq), cols[h]],
                                 preferred_element_type=F32)
            carries.append(rs_d + jnp.sum(sp_p[h], axis=-1, keepdims=True))

        n_older = i - 1

        def body(state):
            j = state[0]
            start = pl.multiple_of((n_older - 1 - j) * tq, tq)
            new = []
            for h in range(n):
                o, ch = _sb_block(qs[h], k_ref[0, pl.ds(start, tq), cols[h]],
                                  v_ref[0, pl.ds(start, tq), cols[h]], upper, state[1 + h], None)
                acc_ref[h] += o
                new.append(ch)
            return (j + 1, *new)

        lax.while_loop(functools.partial(_still_live, n_blocks=n_older), body,
                       (jnp.int32(0), *carries))

    for p in range(pairs):
        o_ref[0, :, p * LANES:(p + 1) * LANES] = jnp.where(
            first, acc_ref[2 * p], acc_ref[2 * p + 1])


def _stick_breaking_self(qkv, *, tq, sb_width, pairs_per_step):
    b, t, _ = qkv.shape
    width = pairs_per_step * LANES
    groups = sb_width // width
    return pl.pallas_call(
        functools.partial(_sb_window_kernel, tq=tq, pairs=pairs_per_step),
        grid=(b, groups, t // tq),
        in_specs=[
            pl.BlockSpec((1, tq, width), lambda bi, g, i: (bi, i, g)),
            pl.BlockSpec((1, t, width), lambda bi, g, i: (bi, 0, groups + g)),
            pl.BlockSpec((1, t, width), lambda bi, g, i: (bi, 0, 2 * groups + g)),
        ],
        out_specs=pl.BlockSpec((1, tq, width), lambda bi, g, i: (bi, i, g)),
        out_shape=jax.ShapeDtypeStruct((b, t, sb_width), F32),
        scratch_shapes=[pltpu.VMEM((2 * pairs_per_step, tq, LANES), F32)],
        compiler_params=pltpu.CompilerParams(
            dimension_semantics=("arbitrary", "arbitrary", "arbitrary"),
            vmem_limit_bytes=VMEM_LIMIT),
        name="stick_breaking_self",
    )(qkv, qkv, qkv)


def _sb_cached_kernel(qkv_ref, kwin_ref, vwin_ref, kc_ref, vc_ref, o_ref,
                      kbuf, vbuf, acc_ref, sem, *, heads, tk, n_older):
    b = pl.program_id(0)
    tq = qkv_ref.shape[1]
    s = heads * HEAD_DIM
    r = lax.broadcasted_iota(jnp.int32, (tq, tq), 0)
    c = lax.broadcasted_iota(jnp.int32, (tq, tq), 1)
    causal = c < r
    upper_d = _strict_upper(tq)
    upper = _strict_upper(tk)
    q = qkv_ref[0, :, 0:s]
    kn = qkv_ref[0, :, s:2 * s]
    vn = qkv_ref[0, :, 2 * s:3 * s]
    head = lambda x, h: x[:, h * HEAD_DIM:(h + 1) * HEAD_DIM]
    qs = [head(q, h) for h in range(heads)]

    zd = [lax.dot_general(qs[h], head(kn, h), NT_DIMS, preferred_element_type=F32)
          for h in range(heads)]
    zw = [jnp.dot(qs[h], kwin_ref[0, h].astype(BF16), preferred_element_type=F32)
          for h in range(heads)]
    sp_d = [jnp.where(causal, _softplus(z), 0.0) for z in zd]
    sp_w = [_softplus(z) for z in zw]
    later_d = _sum_later(jnp.concatenate(sp_d, axis=0), upper_d)
    later_w = _sum_later(jnp.concatenate(sp_w, axis=0), upper)
    carries = []
    for h in range(heads):
        rows = slice(h * tq, (h + 1) * tq)
        rs_d = jnp.sum(sp_d[h], axis=-1, keepdims=True)
        w_d = jnp.where(causal, jnp.exp((zd[h] - sp_d[h]) - later_d[rows]), 0.0)
        w_w = jnp.exp((zw[h] - sp_w[h]) - later_w[rows] - rs_d)
        acc_ref[h] = (jnp.dot(w_d.astype(BF16), head(vn, h), preferred_element_type=F32)
                      + lax.dot_general(w_w.astype(BF16), vwin_ref[0, h].astype(BF16), NT_DIMS,
                                        preferred_element_type=F32))
        carries.append(rs_d + jnp.sum(sp_w[h], axis=-1, keepdims=True))

    def block_copies(j):
        start = pl.multiple_of((n_older - 1 - j) * tk, tk)
        return (pltpu.make_async_copy(kc_ref.at[b, :, :, pl.ds(start, tk)], kbuf, sem.at[0]),
                pltpu.make_async_copy(vc_ref.at[b, :, :, pl.ds(start, tk)], vbuf, sem.at[1]))

    def body(state):
        j = state[0]
        copies = block_copies(j)
        for cp in copies:
            cp.start()
        for cp in copies:
            cp.wait()
        new = []
        for h in range(heads):
            o, ch = _sb_block(qs[h], kbuf[h].astype(BF16), vbuf[h].astype(BF16),
                              upper, state[1 + h], None, transposed=True)
            acc_ref[h] += o
            new.append(ch)
        return (j + 1, *new)

    lax.while_loop(functools.partial(_still_live, n_blocks=n_older), body,
                   (jnp.int32(0), *carries))
    o_ref[0] = jnp.concatenate([acc_ref[h] for h in range(heads)], axis=1)


def _stick_breaking_cached(qkv, cache_k, cache_v, *, tk):
    b, t, _ = qkv.shape
    _, heads, dh, p = cache_k.shape
    s = heads * dh
    last = p // tk - 1
    win = pl.BlockSpec((1, heads, dh, tk), lambda bi: (bi, 0, 0, last))
    return pl.pallas_call(
        functools.partial(_sb_cached_kernel, heads=heads, tk=tk, n_older=last),
        grid=(b,),
        in_specs=[
            pl.BlockSpec((1, t, 3 * s), lambda bi: (bi, 0, 0)),
            win, win,
            pl.BlockSpec(memory_space=pl.ANY),
            pl.BlockSpec(memory_space=pl.ANY),
        ],
        out_specs=pl.BlockSpec((1, t, s), lambda bi: (bi, 0, 0)),
        out_shape=jax.ShapeDtypeStruct((b, t, s), F32),
        scratch_shapes=[
            pltpu.VMEM((heads, dh, tk), F32),
            pltpu.VMEM((heads, dh, tk), F32),
            pltpu.VMEM((heads, t, dh), F32),
            pltpu.SemaphoreType.DMA((2,)),
        ],
        compiler_params=pltpu.CompilerParams(
            dimension_semantics=("arbitrary",), vmem_limit_bytes=VMEM_LIMIT),
        name="stick_breaking_cached",
    )(qkv, cache_k, cache_v, cache_k, cache_v)


def _silu(g):
    return g / (1.0 + jnp.exp(-g))


def _epilogue_kernel(x_ref, osb_ref, rest_ref, prev_ref, hist0_ref, mk_ref, mv_ref,
                     poolw_ref, pscale_ref, wout_ref, gfin_ref, y_ref,
                     *, tt, start, sb_width, pool_width, xa_width, xa_heads, apply_final):
    i = pl.program_id(1)
    s, pw, xw = sb_width, pool_width, xa_width
    g_sb = rest_ref[0, :, 0:s]
    u = rest_ref[0, :, s:s + pw]
    g_pool = rest_ref[0, :, s + pw:s + 2 * pw]
    q_xa = rest_ref[0, :, s + 2 * pw:s + 2 * pw + xw]
    g_xa = rest_ref[0, :, s + 2 * pw + xw:s + 2 * pw + 2 * xw]

    hist = jnp.where(i == 0, hist0_ref[0], prev_ref[0])
    ext = jnp.concatenate([hist, u], axis=0)
    sums = {1: ext}
    width = 1
    while width < max(POOL_WINDOWS):
        prev = sums[width]
        sums[2 * width] = prev + pltpu.roll(prev, width, 0)
        width *= 2
    lane = lax.broadcasted_iota(jnp.int32, (tt, pw), 1)
    group = lane // (pw // len(POOL_WINDOWS))
    pos = start + i * tt + lax.broadcasted_iota(jnp.int32, (tt, pw), 0)
    win_sum = jnp.zeros((tt, pw), F32)
    win = jnp.zeros((tt, pw), jnp.int32)
    for gi, w in enumerate(POOL_WINDOWS):
        sel = group == gi
        win_sum = jnp.where(sel, sums[w][HIST_ROWS:], win_sum)
        win = jnp.where(sel, w, win)
    cnt = jnp.minimum(pos + 1, win).astype(F32)
    pooled = win_sum / cnt - u
    o_pool = jnp.dot(pooled.astype(BF16), poolw_ref[...],
                     preferred_element_type=F32) * pscale_ref[...]

    mk = mk_ref[0].astype(BF16)
    mv = mv_ref[0].astype(BF16)
    xa_dim = xw // xa_heads
    lane_x = lax.broadcasted_iota(jnp.int32, (tt, xw), 1)
    head_x = lane_x // xa_dim
    qs = q_xa * (xa_dim ** -0.5)
    o_xa = jnp.zeros((tt, xw), F32)
    for hh in range(xa_heads):
        sel = head_x == hh
        qh = jnp.where(sel, qs, 0.0).astype(BF16)
        sc = lax.dot_general(qh, mk, NT_DIMS, preferred_element_type=F32)
        e = jnp.exp(sc - jnp.max(sc, axis=-1, keepdims=True))
        p = e / jnp.sum(e, axis=-1, keepdims=True)
        oh = jnp.dot(p.astype(BF16), mv, preferred_element_type=F32)
        o_xa = jnp.where(sel, oh, o_xa)

    m_sb = (osb_ref[0] * _silu(g_sb)).astype(BF16)
    m_pool = (o_pool * _silu(g_pool)).astype(BF16)
    m_xa = (o_xa * _silu(g_xa)).astype(BF16)
    y = (x_ref[0]
         + jnp.dot(m_sb, wout_ref[0:s, :], preferred_element_type=F32)
         + jnp.dot(m_pool, wout_ref[s:s + pw, :], preferred_element_type=F32)
         + jnp.dot(m_xa, wout_ref[s + pw:s + pw + xw, :], preferred_element_type=F32))
    if apply_final:
        y = _rms(y, gfin_ref[...])
    y_ref[0] = y


def _epilogue(x, o_sb, rest, hist0, mk, mv, pool_bd, pool_scale, w_out_bf16, g_final,
              *, tt, start, sb_width, pool_width, xa_width, xa_heads, apply_final):
    b, t, d = x.shape
    n_rest = rest.shape[-1]
    n_mem = mk.shape[1]
    hist_blocks = tt // HIST_ROWS
    u_col = sb_width // pool_width
    kern = functools.partial(
        _epilogue_kernel, tt=tt, start=start, sb_width=sb_width, pool_width=pool_width,
        xa_width=xa_width, xa_heads=xa_heads, apply_final=apply_final)
    return pl.pallas_call(
        kern,
        grid=(b, t // tt),
        in_specs=[
            pl.BlockSpec((1, tt, d), lambda bi, i: (bi, i, 0)),
            pl.BlockSpec((1, tt, sb_width), lambda bi, i: (bi, i, 0)),
            pl.BlockSpec((1, tt, n_rest), lambda bi, i: (bi, i, 0)),
            pl.BlockSpec((1, HIST_ROWS, pool_width),
                         lambda bi, i: (bi, jnp.maximum(i * hist_blocks - 1, 0), u_col)),
            pl.BlockSpec((1, HIST_ROWS, pool_width), lambda bi, i: (bi, 0, 0)),
            pl.BlockSpec((1, n_mem, xa_width), lambda bi, i: (bi, 0, 0)),
            pl.BlockSpec((1, n_mem, xa_width), lambda bi, i: (bi, 0, 0)),
            pl.BlockSpec((pool_width, pool_width), lambda bi, i: (0, 0)),
            pl.BlockSpec((1, pool_width), lambda bi, i: (0, 0)),
            pl.BlockSpec((d, d), lambda bi, i: (0, 0)),
            pl.BlockSpec((1, d), lambda bi, i: (0, 0)),
        ],
        out_specs=pl.BlockSpec((1, tt, d), lambda bi, i: (bi, i, 0)),
        out_shape=jax.ShapeDtypeStruct((b, t, d), F32),
        compiler_params=pltpu.CompilerParams(
            dimension_semantics=("arbitrary", "arbitrary"), vmem_limit_bytes=VMEM_LIMIT),
        name="epilogue",
    )(x, o_sb, rest, rest, hist0, mk, mv, pool_bd, pool_scale, w_out_bf16, g_final)


def _block_diag(pool_w):
    g, c, _ = pool_w.shape
    eye = jnp.eye(g, dtype=pool_w.dtype)
    return (eye[:, None, :, None] * pool_w[:, :, None, :]).reshape(g * c, g * c)


def _pad_hist(hist):
    return jnp.pad(hist, ((0, 0), (HIST_ROWS - hist.shape[1], 0), (0, 0)))


def _mixer_layer(x, start, k_past, v_past, pool_hist, mk, mv, g_norm, w_in_bf16, wkv_t, pool_bd,
                 pool_scale, w_out_bf16, g_final, *, dims, tiles, apply_final):
    sb_width, pool_width, xa_width, xa_heads = dims
    tm, tq, tk, tt = tiles
    b, t, d = x.shape
    heads = sb_width // HEAD_DIM
    k_new, v_new, qkv, rest = _project(x.reshape(b * t, d), g_norm, w_in_bf16, wkv_t,
                                       sb_width, tm, t)
    if k_new.ndim == 3:
        to_cache = lambda a: jnp.transpose(a.reshape(b, heads, HEAD_DIM, t), (0, 3, 1, 2))
    else:
        to_cache = lambda a: a.reshape(b, t, heads, HEAD_DIM)
    qkv = qkv.reshape(b, t, -1)
    rest = rest.reshape(b, t, -1)
    if k_past is None:
        o_sb = _stick_breaking_self(qkv, tq=tq, sb_width=sb_width,
                                    pairs_per_step=PAIRS_PER_STEP)
    else:
        o_sb = _stick_breaking_cached(qkv, k_past, v_past, tk=tk)
    y = _epilogue(x, o_sb, rest, _pad_hist(pool_hist), mk, mv, pool_bd, pool_scale,
                  w_out_bf16, g_final, tt=tt, start=start, sb_width=sb_width,
                  pool_width=pool_width, xa_width=xa_width, xa_heads=xa_heads,
                  apply_final=apply_final)
    u_pool = rest[:, :, sb_width:sb_width + pool_width]
    new_hist = jnp.concatenate([pool_hist, u_pool], axis=1)[:, -POOL_STATE:]
    return y, to_cache(k_new), to_cache(v_new), new_hist


def kernel(x_prompt, x_sample, cache_sb_k, cache_sb_v, state_pool, cache_mem_k, cache_mem_v,
           mem_prompt, g_norm, w_in, pool_w, pool_scale, g_mem, w_mem_kv, w_out, g_final):
    depth = g_norm.shape[0]
    bp, tp, d = x_prompt.shape
    bs, ts, _ = x_sample.shape
    past = cache_sb_k.shape[2]
    sb_heads, head_dim = cache_sb_k.shape[3], cache_sb_k.shape[4]
    assert head_dim == HEAD_DIM
    sb_width = sb_heads * head_dim
    pool_width = pool_w.shape[1] * pool_w.shape[2]
    n_mem, xa_heads, xa_dim = cache_mem_k.shape[2:]
    xa_width = xa_heads * xa_dim
    dims = (sb_width, pool_width, xa_width, xa_heads)
    g_fin = g_final.reshape(1, d)

    yp, ys = x_prompt, x_sample
    kp_l, vp_l, hp_l, mkp_l, mvp_l, ks_l, vs_l, hs_l = [], [], [], [], [], [], [], []
    for l in range(depth):
        final = l == depth - 1
        g_l = g_norm[l].reshape(1, d)
        w_in_b = w_in[l].astype(BF16)
        w_out_b = w_out[l].astype(BF16)
        pool_bd = _block_diag(pool_w[l]).astype(BF16)
        p_scale = pool_scale[l].reshape(1, pool_width)
        mk, mv = _memory_kv(mem_prompt.reshape(bp * n_mem, d), g_mem[l].reshape(1, d),
                            w_mem_kv[l].astype(BF16), tm=256)
        mk = mk.reshape(bp, n_mem, xa_width)
        mv = mv.reshape(bp, n_mem, xa_width)
        hist0 = jnp.zeros((bp, POOL_STATE, pool_width), x_prompt.dtype)
        wkv_t = jnp.transpose(w_in[l][:, sb_width:3 * sb_width]).astype(BF16)
        yp, kp, vp, hp = _mixer_layer(
            yp, 0, None, None, hist0, mk, mv, g_l, w_in_b, wkv_t, pool_bd, p_scale, w_out_b,
            g_fin, dims=dims, tiles=(512, 256, 256, 512), apply_final=final)
        ys, kn, vn, hn = _mixer_layer(
            ys, past, jnp.transpose(cache_sb_k[l], (0, 2, 3, 1)),
            jnp.transpose(cache_sb_v[l], (0, 2, 3, 1)), state_pool[l],
            cache_mem_k[l].reshape(bs, n_mem, xa_width),
            cache_mem_v[l].reshape(bs, n_mem, xa_width),
            g_l, w_in_b, wkv_t, pool_bd, p_scale, w_out_b, g_fin,
            dims=dims, tiles=(512, ts, 256, ts), apply_final=final)
        kp_l.append(kp)
        vp_l.append(vp)
        hp_l.append(hp)
        mkp_l.append(mk.reshape(bp, n_mem, xa_heads, xa_dim))
        mvp_l.append(mv.reshape(bp, n_mem, xa_heads, xa_dim))
        ks_l.append(kn)
        vs_l.append(vn)
        hs_l.append(hn)
    stack = lambda xs: jnp.stack(xs, axis=0)
    return (yp, ys, stack(kp_l), stack(vp_l), stack(hp_l), stack(mkp_l), stack(mvp_l),
            stack(ks_l), stack(vs_l), stack(hs_l))
```

```python
import functools

import jax
import jax.numpy as jnp
from jax import lax
from jax.experimental import pallas as pl
from jax.experimental.pallas import tpu as pltpu

F32 = jnp.float32
BF16 = jnp.bfloat16

HEAD_DIM = 64
LANES = 128
POOL_WINDOWS = (2, 4, 8, 16)
POOL_STATE = max(POOL_WINDOWS) - 1
HIST_ROWS = 16
EPS = 1e-6
DEAD_CARRY = 104.0
VMEM_LIMIT = 56 * 1024 * 1024
NT_DIMS = (((1,), (1,)), ((), ()))
PAIRS_PER_STEP = 2


def _rms(x, g):
    ms = jnp.mean(x * x, axis=-1, keepdims=True)
    return (x * lax.rsqrt(ms + EPS)) * g


def _proj_kernel(x_ref, g_ref, w_ref, wkv_t_ref, k_ref, v_ref, qkv_ref, rest_ref,
                 *, sb_width, time_minor):
    h = _rms(x_ref[...], g_ref[...]).astype(BF16)
    s = sb_width
    q = jnp.dot(h, w_ref[:, 0:s], preferred_element_type=F32)
    qkv_ref[:, 0:s] = (q * (HEAD_DIM ** -0.5)).astype(BF16)
    if time_minor:
        kt = lax.dot_general(wkv_t_ref[0:s, :], h, NT_DIMS, preferred_element_type=F32)
        vt = lax.dot_general(wkv_t_ref[s:2 * s, :], h, NT_DIMS, preferred_element_type=F32)
        k_ref[0] = kt
        v_ref[0] = vt
        qkv_ref[:, s:2 * s] = kt.T.astype(BF16)
        qkv_ref[:, 2 * s:3 * s] = vt.T.astype(BF16)
    else:
        k = jnp.dot(h, w_ref[:, s:2 * s], preferred_element_type=F32)
        v = jnp.dot(h, w_ref[:, 2 * s:3 * s], preferred_element_type=F32)
        k_ref[...] = k
        v_ref[...] = v
        qkv_ref[:, s:2 * s] = k.astype(BF16)
        qkv_ref[:, 2 * s:3 * s] = v.astype(BF16)
    rest_ref[...] = jnp.dot(h, w_ref[:, 3 * s:], preferred_element_type=F32)


def _project(x2d, g, w_bf16, wkv_t, sb_width, tm, rows_per_stream):
    m, d = x2d.shape
    n = w_bf16.shape[1]
    n_rest = n - 3 * sb_width
    time_minor = rows_per_stream % tm == 0
    if time_minor:
        nt = rows_per_stream // tm
        kv_spec = pl.BlockSpec((1, sb_width, tm), lambda i: (i // nt, 0, i % nt))
        kv_shape = jax.ShapeDtypeStruct((m // rows_per_stream, sb_width, rows_per_stream), F32)
    else:
        kv_spec = pl.BlockSpec((tm, sb_width), lambda i: (i, 0))
        kv_shape = jax.ShapeDtypeStruct((m, sb_width), F32)
    return pl.pallas_call(
        functools.partial(_proj_kernel, sb_width=sb_width, time_minor=time_minor),
        grid=(m // tm,),
        in_specs=[
            pl.BlockSpec((tm, d), lambda i: (i, 0)),
            pl.BlockSpec((1, d), lambda i: (0, 0)),
            pl.BlockSpec((d, n), lambda i: (0, 0)),
            pl.BlockSpec((2 * sb_width, d), lambda i: (0, 0)),
        ],
        out_specs=[
            kv_spec,
            kv_spec,
            pl.BlockSpec((tm, 3 * sb_width), lambda i: (i, 0)),
            pl.BlockSpec((tm, n_rest), lambda i: (i, 0)),
        ],
        out_shape=[
            kv_shape,
            kv_shape,
            jax.ShapeDtypeStruct((m, 3 * sb_width), BF16),
            jax.ShapeDtypeStruct((m, n_rest), F32),
        ],
        compiler_params=pltpu.CompilerParams(
            dimension_semantics=("arbitrary",), vmem_limit_bytes=VMEM_LIMIT),
        name="proj",
    )(x2d, g, w_bf16, wkv_t)


def _memkv_kernel(x_ref, g_ref, w_ref, mk_ref, mv_ref, *, xa_width):
    h = _rms(x_ref[...], g_ref[...]).astype(BF16)
    kv = jnp.dot(h, w_ref[...], preferred_element_type=F32)
    mk_ref[...] = kv[:, :xa_width]
    mv_ref[...] = kv[:, xa_width:]


def _memory_kv(mem2d, g, w_bf16, tm):
    m, d = mem2d.shape
    xa_width = w_bf16.shape[1] // 2
    return pl.pallas_call(
        functools.partial(_memkv_kernel, xa_width=xa_width),
        grid=(m // tm,),
        in_specs=[
            pl.BlockSpec((tm, d), lambda i: (i, 0)),
            pl.BlockSpec((1, d), lambda i: (0, 0)),
            pl.BlockSpec((d, 2 * xa_width), lambda i: (0, 0)),
        ],
        out_specs=[
            pl.BlockSpec((tm, xa_width), lambda i: (i, 0)),
            pl.BlockSpec((tm, xa_width), lambda i: (i, 0)),
        ],
        out_shape=[
            jax.ShapeDtypeStruct((m, xa_width), F32),
            jax.ShapeDtypeStruct((m, xa_width), F32),
        ],
        compiler_params=pltpu.CompilerParams(
            dimension_semantics=("arbitrary",), vmem_limit_bytes=VMEM_LIMIT),
        name="memkv",
    )(mem2d, g, w_bf16)


def _strict_upper(n):
    r = lax.broadcasted_iota(jnp.int32, (n, n), 0)
    c = lax.broadcasted_iota(jnp.int32, (n, n), 1)
    return jnp.where(r > c, 1.0, 0.0).astype(BF16)


def _softplus(z):
    return jnp.maximum(z, 0.0) + jnp.log(1.0 + jnp.exp(-jnp.abs(z)))


def _sum_later(sp, upper):
    hi = sp.astype(BF16)
    lo = (sp - hi.astype(F32)).astype(BF16)
    return jnp.dot(jnp.concatenate([hi, lo], axis=1), jnp.concatenate([upper, upper], axis=0),
                   preferred_element_type=F32)


def _sb_block(qh, kblk, vblk, upper, carry, mask, transposed=False):
    if transposed:
        z = jnp.dot(qh, kblk, preferred_element_type=F32)
    else:
        z = lax.dot_general(qh, kblk, NT_DIMS, preferred_element_type=F32)
    sp = _softplus(z)
    if mask is not None:
        sp = jnp.where(mask, sp, 0.0)
    later = _sum_later(sp, upper)
    w = jnp.exp((z - sp) - later - carry)
    if mask is not None:
        w = jnp.where(mask, w, 0.0)
    if transposed:
        out = lax.dot_general(w.astype(BF16), vblk, NT_DIMS, preferred_element_type=F32)
    else:
        out = jnp.dot(w.astype(BF16), vblk, preferred_element_type=F32)
    return out, carry + jnp.sum(sp, axis=-1, keepdims=True)


def _still_live(state, n_blocks):
    lowest = functools.reduce(jnp.minimum, state[1:])
    return jnp.logical_and(state[0] < n_blocks, jnp.min(lowest) <= DEAD_CARRY)


def _sb_window_kernel(q_ref, k_ref, v_ref, o_ref, acc_ref, *, tq, pairs):
    i = pl.program_id(2)
    lane = lax.broadcasted_iota(jnp.int32, (tq, LANES), 1)
    first = lane < HEAD_DIM
    qs, cols = [], []
    for p in range(pairs):
        col = slice(p * LANES, (p + 1) * LANES)
        q = q_ref[0, :, col]
        zero = jnp.zeros_like(q)
        qs += [jnp.where(first, q, zero), jnp.where(first, zero, q)]
        cols += [col, col]
    n = len(qs)
    r = lax.broadcasted_iota(jnp.int32, (tq, tq), 0)
    c = lax.broadcasted_iota(jnp.int32, (tq, tq), 1)
    causal = c < r
    upper = _strict_upper(tq)

    @pl.when(i == 0)
    def _():
        zc = jnp.zeros((tq, 1), F32)
        for h in range(n):
            o, _ = _sb_block(qs[h], k_ref[0, 0:tq, cols[h]], v_ref[0, 0:tq, cols[h]],
                             upper, zc, causal)
            acc_ref[h] = o

    @pl.when(i > 0)
    def _():
        ws = pl.multiple_of((i - 1) * tq, tq)
        zs = [lax.dot_general(qs[h], k_ref[0, pl.ds(ws, 2 * tq), cols[h]], NT_DIMS,
                              preferred_element_type=F32) for h in range(n)]
        zp = [z[:, :tq] for z in zs]
        zd = [z[:, tq:] for z in zs]
        sp_p = [_softplus(z) for z in zp]
        sp_d = [jnp.where(causal, _softplus(z), 0.0) for z in zd]
        later = _sum_later(jnp.concatenate(sp_p + sp_d, axis=0), upper)
        carries = []
        for h in range(n):
            rs_d = jnp.sum(sp_d[h], axis=-1, keepdims=True)
            later_p = later[h * tq:(h + 1) * tq]
            later_d = later[(n + h) * tq:(n + h + 1) * tq]
            w_p = jnp.exp((zp[h] - sp_p[h]) - later_p - rs_d)
            w_d = jnp.where(causal, jnp.exp((zd[h] - sp_d[h]) - later_d), 0.0)
            w = jnp.concatenate([w_p.astype(BF16), w_d.astype(BF16)], axis=1)
            acc_ref[h] = jnp.dot(w, v_ref[0, pl.ds(ws, 2 * tq), cols[h]],
                                 preferred_element_type=F32)
            carries.append(rs_d + jnp.sum(sp_p[h], axis=-1, keepdims=True))

        n_older = i - 1

        def body(state):
            j = state[0]
            start = pl.multiple_of((n_older - 1 - j) * tq, tq)
            new = []
            for h in range(n):
                o, ch = _sb_block(qs[h], k_ref[0, pl.ds(start, tq), cols[h]],
                                  v_ref[0, pl.ds(start, tq), cols[h]], upper, state[1 + h], None)
                acc_ref[h] += o
                new.append(ch)
            return (j + 1, *new)

        lax.while_loop(functools.partial(_still_live, n_blocks=n_older), body,
                       (jnp.int32(0), *carries))

    for p in range(pairs):
        o_ref[0, :, p * LANES:(p + 1) * LANES] = jnp.where(
            first, acc_ref[2 * p], acc_ref[2 * p + 1])


def _stick_breaking_self(qkv, *, tq, sb_width, pairs_per_step):
    b, t, _ = qkv.shape
    width = pairs_per_step * LANES
    groups = sb_width // width
    return pl.pallas_call(
        functools.partial(_sb_window_kernel, tq=tq, pairs=pairs_per_step),
        grid=(b, groups, t // tq),
        in_specs=[
            pl.BlockSpec((1, tq, width), lambda bi, g, i: (bi, i, g)),
            pl.BlockSpec((1, t, width), lambda bi, g, i: (bi, 0, groups + g)),
            pl.BlockSpec((1, t, width), lambda bi, g, i: (bi, 0, 2 * groups + g)),
        ],
        out_specs=pl.BlockSpec((1, tq, width), lambda bi, g, i: (bi, i, g)),
        out_shape=jax.ShapeDtypeStruct((b, t, sb_width), F32),
        scratch_shapes=[pltpu.VMEM((2 * pairs_per_step, tq, LANES), F32)],
        compiler_params=pltpu.CompilerParams(
            dimension_semantics=("arbitrary", "arbitrary", "arbitrary"),
            vmem_limit_bytes=VMEM_LIMIT),
        name="stick_breaking_self",
    )(qkv, qkv, qkv)


def _sb_cached_kernel(qkv_ref, kwin_ref, vwin_ref, kc_ref, vc_ref, o_ref,
                      kbuf, vbuf, acc_ref, sem, *, heads, tk, n_older):
    b = pl.program_id(0)
    tq = qkv_ref.shape[1]
    s = heads * HEAD_DIM
    r = lax.broadcasted_iota(jnp.int32, (tq, tq), 0)
    c = lax.broadcasted_iota(jnp.int32, (tq, tq), 1)
    causal = c < r
    upper_d = _strict_upper(tq)
    upper = _strict_upper(tk)
    q = qkv_ref[0, :, 0:s]
    kn = qkv_ref[0, :, s:2 * s]
    vn = qkv_ref[0, :, 2 * s:3 * s]
    head = lambda x, h: x[:, h * HEAD_DIM:(h + 1) * HEAD_DIM]
    qs = [head(q, h) for h in range(heads)]

    zd = [lax.dot_general(qs[h], head(kn, h), NT_DIMS, preferred_element_type=F32)
          for h in range(heads)]
    zw = [jnp.dot(qs[h], kwin_ref[0, h].astype(BF16), preferred_element_type=F32)
          for h in range(heads)]
    sp_d = [jnp.where(causal, _softplus(z), 0.0) for z in zd]
    sp_w = [_softplus(z) for z in zw]
    later_d = _sum_later(jnp.concatenate(sp_d, axis=0), upper_d)
    later_w = _sum_later(jnp.concatenate(sp_w, axis=0), upper)
    carries = []
    for h in range(heads):
        rows = slice(h * tq, (h + 1) * tq)
        rs_d = jnp.sum(sp_d[h], axis=-1, keepdims=True)
        w_d = jnp.where(causal, jnp.exp((zd[h] - sp_d[h]) - later_d[rows]), 0.0)
        w_w = jnp.exp((zw[h] - sp_w[h]) - later_w[rows] - rs_d)
        acc_ref[h] = (jnp.dot(w_d.astype(BF16), head(vn, h), preferred_element_type=F32)
                      + lax.dot_general(w_w.astype(BF16), vwin_ref[0, h].astype(BF16), NT_DIMS,
                                        preferred_element_type=F32))
        carries.append(rs_d + jnp.sum(sp_w[h], axis=-1, keepdims=True))

    def block_copies(j):
        start = pl.multiple_of((n_older - 1 - j) * tk, tk)
        return (pltpu.make_async_copy(kc_ref.at[b, :, :, pl.ds(start, tk)], kbuf, sem.at[0]),
                pltpu.make_async_copy(vc_ref.at[b, :, :, pl.ds(start, tk)], vbuf, sem.at[1]))

    def body(state):
        j = state[0]
        copies = block_copies(j)
        for cp in copies:
            cp.start()
        for cp in copies:
            cp.wait()
        new = []
        for h in range(heads):
            o, ch = _sb_block(qs[h], kbuf[h].astype(BF16), vbuf[h].astype(BF16),
                              upper, state[1 + h], None, transposed=True)
            acc_ref[h] += o
            new.append(ch)
        return (j + 1, *new)

    lax.while_loop(functools.partial(_still_live, n_blocks=n_older), body,
                   (jnp.int32(0), *carries))
    o_ref[0] = jnp.concatenate([acc_ref[h] for h in range(heads)], axis=1)


def _stick_breaking_cached(qkv, cache_k, cache_v, *, tk):
    b, t, _ = qkv.shape
    _, heads, dh, p = cache_k.shape
    s = heads * dh
    last = p // tk - 1
    win = pl.BlockSpec((1, heads, dh, tk), lambda bi: (bi, 0, 0, last))
    return pl.pallas_call(
        functools.partial(_sb_cached_kernel, heads=heads, tk=tk, n_older=last),
        grid=(b,),
        in_specs=[
            pl.BlockSpec((1, t, 3 * s), lambda bi: (bi, 0, 0)),
            win, win,
            pl.BlockSpec(memory_space=pl.ANY),
            pl.BlockSpec(memory_space=pl.ANY),
        ],
        out_specs=pl.BlockSpec((1, t, s), lambda bi: (bi, 0, 0)),
        out_shape=jax.ShapeDtypeStruct((b, t, s), F32),
        scratch_shapes=[
            pltpu.VMEM((heads, dh, tk), F32),
            pltpu.VMEM((heads, dh, tk), F32),
            pltpu.VMEM((heads, t, dh), F32),
            pltpu.SemaphoreType.DMA((2,)),
        ],
        compiler_params=pltpu.CompilerParams(
            dimension_semantics=("arbitrary",), vmem_limit_bytes=VMEM_LIMIT),
        name="stick_breaking_cached",
    )(qkv, cache_k, cache_v, cache_k, cache_v)


def _silu(g):
    return g / (1.0 + jnp.exp(-g))


def _epilogue_kernel(x_ref, osb_ref, rest_ref, prev_ref, hist0_ref, mk_ref, mv_ref,
                     poolw_ref, pscale_ref, wout_ref, gfin_ref, y_ref,
                     *, tt, start, sb_width, pool_width, xa_width, xa_heads, apply_final):
    i = pl.program_id(1)
    s, pw, xw = sb_width, pool_width, xa_width
    g_sb = rest_ref[0, :, 0:s]
    u = rest_ref[0, :, s:s + pw]
    g_pool = rest_ref[0, :, s + pw:s + 2 * pw]
    q_xa = rest_ref[0, :, s + 2 * pw:s + 2 * pw + xw]
    g_xa = rest_ref[0, :, s + 2 * pw + xw:s + 2 * pw + 2 * xw]

    hist = jnp.where(i == 0, hist0_ref[0], prev_ref[0])
    ext = jnp.concatenate([hist, u], axis=0)
    sums = {1: ext}
    width = 1
    while width < max(POOL_WINDOWS):
        prev = sums[width]
        sums[2 * width] = prev + pltpu.roll(prev, width, 0)
        width *= 2
    lane = lax.broadcasted_iota(jnp.int32, (tt, pw), 1)
    group = lane // (pw // len(POOL_WINDOWS))
    pos = start + i * tt + lax.broadcasted_iota(jnp.int32, (tt, pw), 0)
    win_sum = jnp.zeros((tt, pw), F32)
    win = jnp.zeros((tt, pw), jnp.int32)
    for gi, w in enumerate(POOL_WINDOWS):
        sel = group == gi
        win_sum = jnp.where(sel, sums[w][HIST_ROWS:], win_sum)
        win = jnp.where(sel, w, win)
    cnt = jnp.minimum(pos + 1, win).astype(F32)
    pooled = win_sum / cnt - u
    o_pool = jnp.dot(pooled.astype(BF16), poolw_ref[...],
                     preferred_element_type=F32) * pscale_ref[...]

    mk = mk_ref[0].astype(BF16)
    mv = mv_ref[0].astype(BF16)
    xa_dim = xw // xa_heads
    lane_x = lax.broadcasted_iota(jnp.int32, (tt, xw), 1)
    head_x = lane_x // xa_dim
    qs = q_xa * (xa_dim ** -0.5)
    o_xa = jnp.zeros((tt, xw), F32)
    for hh in range(xa_heads):
        sel = head_x == hh
        qh = jnp.where(sel, qs, 0.0).astype(BF16)
        sc = lax.dot_general(qh, mk, NT_DIMS, preferred_element_type=F32)
        e = jnp.exp(sc - jnp.max(sc, axis=-1, keepdims=True))
        p = e / jnp.sum(e, axis=-1, keepdims=True)
        oh = jnp.dot(p.astype(BF16), mv, preferred_element_type=F32)
        o_xa = jnp.where(sel, oh, o_xa)

    m_sb = (osb_ref[0] * _silu(g_sb)).astype(BF16)
    m_pool = (o_pool * _silu(g_pool)).astype(BF16)
    m_xa = (o_xa * _silu(g_xa)).astype(BF16)
    y = (x_ref[0]
         + jnp.dot(m_sb, wout_ref[0:s, :], preferred_element_type=F32)
         + jnp.dot(m_pool, wout_ref[s:s + pw, :], preferred_element_type=F32)
         + jnp.dot(m_xa, wout_ref[s + pw:s + pw + xw, :], preferred_element_type=F32))
    if apply_final:
        y = _rms(y, gfin_ref[...])
    y_ref[0] = y


def _epilogue(x, o_sb, rest, hist0, mk, mv, pool_bd, pool_scale, w_out_bf16, g_final,
              *, tt, start, sb_width, pool_width, xa_width, xa_heads, apply_final):
    b, t, d = x.shape
    n_rest = rest.shape[-1]
    n_mem = mk.shape[1]
    hist_blocks = tt // HIST_ROWS
    u_col = sb_width // pool_width
    kern = functools.partial(
        _epilogue_kernel, tt=tt, start=start, sb_width=sb_width, pool_width=pool_width,
        xa_width=xa_width, xa_heads=xa_heads, apply_final=apply_final)
    return pl.pallas_call(
        kern,
        grid=(b, t // tt),
        in_specs=[
            pl.BlockSpec((1, tt, d), lambda bi, i: (bi, i, 0)),
            pl.BlockSpec((1, tt, sb_width), lambda bi, i: (bi, i, 0)),
            pl.BlockSpec((1, tt, n_rest), lambda bi, i: (bi, i, 0)),
            pl.BlockSpec((1, HIST_ROWS, pool_width),
                         lambda bi, i: (bi, jnp.maximum(i * hist_blocks - 1, 0), u_col)),
            pl.BlockSpec((1, HIST_ROWS, pool_width), lambda bi, i: (bi, 0, 0)),
            pl.BlockSpec((1, n_mem, xa_width), lambda bi, i: (bi, 0, 0)),
            pl.BlockSpec((1, n_mem, xa_width), lambda bi, i: (bi, 0, 0)),
            pl.BlockSpec((pool_width, pool_width), lambda bi, i: (0, 0)),
            pl.BlockSpec((1, pool_width), lambda bi, i: (0, 0)),
            pl.BlockSpec((d, d), lambda bi, i: (0, 0)),
            pl.BlockSpec((1, d), lambda bi, i: (0, 0)),
        ],
        out_specs=pl.BlockSpec((1, tt, d), lambda bi, i: (bi, i, 0)),
        out_shape=jax.ShapeDtypeStruct((b, t, d), F32),
        compiler_params=pltpu.CompilerParams(
            dimension_semantics=("arbitrary", "arbitrary"), vmem_limit_bytes=VMEM_LIMIT),
        name="epilogue",
    )(x, o_sb, rest, rest, hist0, mk, mv, pool_bd, pool_scale, w_out_bf16, g_final)


def _block_diag(pool_w):
    g, c, _ = pool_w.shape
    eye = jnp.eye(g, dtype=pool_w.dtype)
    return (eye[:, None, :, None] * pool_w[:, :, None, :]).reshape(g * c, g * c)


def _pad_hist(hist):
    return jnp.pad(hist, ((0, 0), (HIST_ROWS - hist.shape[1], 0), (0, 0)))


def _mixer_layer(x, start, k_past, v_past, pool_hist, mk, mv, g_norm, w_in_bf16, wkv_t, pool_bd,
                 pool_scale, w_out_bf16, g_final, *, dims, tiles, apply_final):
    sb_width, pool_width, xa_width, xa_heads = dims
    tm, tq, tk, tt = tiles
    b, t, d = x.shape
    heads = sb_width // HEAD_DIM
    k_new, v_new, qkv, rest = _project(x.reshape(b * t, d), g_norm, w_in_bf16, wkv_t,
                                       sb_width, tm, t)
    if k_new.ndim == 3:
        to_cache = lambda a: jnp.transpose(a.reshape(b, heads, HEAD_DIM, t), (0, 3, 1, 2))
    else:
        to_cache = lambda a: a.reshape(b, t, heads, HEAD_DIM)
    qkv = qkv.reshape(b, t, -1)
    rest = rest.reshape(b, t, -1)
    if k_past is None:
        o_sb = _stick_breaking_self(qkv, tq=tq, sb_width=sb_width,
                                    pairs_per_step=PAIRS_PER_STEP)
    else:
        o_sb = _stick_breaking_cached(qkv, k_past, v_past, tk=tk)
    y = _epilogue(x, o_sb, rest, _pad_hist(pool_hist), mk, mv, pool_bd, pool_scale,
                  w_out_bf16, g_final, tt=tt, start=start, sb_width=sb_width,
                  pool_width=pool_width, xa_width=xa_width, xa_heads=xa_heads,
                  apply_final=apply_final)
    u_pool = rest[:, :, sb_width:sb_width + pool_width]
    new_hist = jnp.concatenate([pool_hist, u_pool], axis=1)[:, -POOL_STATE:]
    return y, to_cache(k_new), to_cache(v_new), new_hist


def kernel(x_prompt, x_sample, cache_sb_k, cache_sb_v, state_pool, cache_mem_k, cache_mem_v,
           mem_prompt, g_norm, w_in, pool_w, pool_scale, g_mem, w_mem_kv, w_out, g_final):
    depth = g_norm.shape[0]
    bp, tp, d = x_prompt.shape
    bs, ts, _ = x_sample.shape
    past = cache_sb_k.shape[2]
    sb_heads, head_dim = cache_sb_k.shape[3], cache_sb_k.shape[4]
    assert head_dim == HEAD_DIM
    sb_width = sb_heads * head_dim
    pool_width = pool_w.shape[1] * pool_w.shape[2]
    n_mem, xa_heads, xa_dim = cache_mem_k.shape[2:]
    xa_width = xa_heads * xa_dim
    dims = (sb_width, pool_width, xa_width, xa_heads)
    g_fin = g_final.reshape(1, d)

    yp, ys = x_prompt, x_sample
    kp_l, vp_l, hp_l, mkp_l, mvp_l, ks_l, vs_l, hs_l = [], [], [], [], [], [], [], []
    for l in range(depth):
        final = l == depth - 1
        g_l = g_norm[l].reshape(1, d)
        w_in_b = w_in[l].astype(BF16)
        w_out_b = w_out[l].astype(BF16)
        pool_bd = _block_diag(pool_w[l]).astype(BF16)
        p_scale = pool_scale[l].reshape(1, pool_width)
        mk, mv = _memory_kv(mem_prompt.reshape(bp * n_mem, d), g_mem[l].reshape(1, d),
                            w_mem_kv[l].astype(BF16), tm=256)
        mk = mk.reshape(bp, n_mem, xa_width)
        mv = mv.reshape(bp, n_mem, xa_width)
        hist0 = jnp.zeros((bp, POOL_STATE, pool_width), x_prompt.dtype)
        wkv_t = jnp.transpose(w_in[l][:, sb_width:3 * sb_width]).astype(BF16)
        yp, kp, vp, hp = _mixer_layer(
            yp, 0, None, None, hist0, mk, mv, g_l, w_in_b, wkv_t, pool_bd, p_scale, w_out_b,
            g_fin, dims=dims, tiles=(512, 256, 256, 512), apply_final=final)
        ys, kn, vn, hn = _mixer_layer(
            ys, past, jnp.transpose(cache_sb_k[l], (0, 2, 3, 1)),
            jnp.transpose(cache_sb_v[l], (0, 2, 3, 1)), state_pool[l],
            cache_mem_k[l].reshape(bs, n_mem, xa_width),
            cache_mem_v[l].reshape(bs, n_mem, xa_width),
            g_l, w_in_b, wkv_t, pool_bd, p_scale, w_out_b, g_fin,
            dims=dims, tiles=(512, ts, 256, ts), apply_final=final)
        kp_l.append(kp)
        vp_l.append(vp)
        hp_l.append(hp)
        mkp_l.append(mk.reshape(bp, n_mem, xa_heads, xa_dim))
        mvp_l.append(mv.reshape(bp, n_mem, xa_heads, xa_dim))
        ks_l.append(kn)
        vs_l.append(vn)
        hs_l.append(hn)
    stack = lambda xs: jnp.stack(xs, axis=0)
    return (yp, ys, stack(kp_l), stack(vp_l), stack(hp_l), stack(mkp_l), stack(mvp_l),
            stack(ks_l), stack(vs_l), stack(hs_l))
```

```python
import functools

import jax
import jax.numpy as jnp
from jax import lax
from jax.experimental import pallas as pl
from jax.experimental.pallas import tpu as pltpu

F32 = jnp.float32
BF16 = jnp.bfloat16

HEAD_DIM = 64
LANES = 128
POOL_WINDOWS = (2, 4, 8, 16)
POOL_STATE = max(POOL_WINDOWS) - 1
HIST_ROWS = 16
EPS = 1e-6
DEAD_CARRY = 104.0
VMEM_LIMIT = 56 * 1024 * 1024
NT_DIMS = (((1,), (1,)), ((), ()))
PAIRS_PER_STEP = 2


def _rms(x, g):
    ms = jnp.mean(x * x, axis=-1, keepdims=True)
    return (x * lax.rsqrt(ms + EPS)) * g


def _proj_kernel(x_ref, g_ref, w_ref, wkv_t_ref, k_ref, v_ref, qkv_ref, rest_ref,
                 *, sb_width, time_minor):
    h = _rms(x_ref[...], g_ref[...]).astype(BF16)
    s = sb_width
    q = jnp.dot(h, w_ref[:, 0:s], preferred_element_type=F32)
    qkv_ref[:, 0:s] = (q * (HEAD_DIM ** -0.5)).astype(BF16)
    if time_minor:
        kt = lax.dot_general(wkv_t_ref[0:s, :], h, NT_DIMS, preferred_element_type=F32)
        vt = lax.dot_general(wkv_t_ref[s:2 * s, :], h, NT_DIMS, preferred_element_type=F32)
        k_ref[0] = kt
        v_ref[0] = vt
        qkv_ref[:, s:2 * s] = kt.T.astype(BF16)
        qkv_ref[:, 2 * s:3 * s] = vt.T.astype(BF16)
    else:
        k = jnp.dot(h, w_ref[:, s:2 * s], preferred_element_type=F32)
        v = jnp.dot(h, w_ref[:, 2 * s:3 * s], preferred_element_type=F32)
        k_ref[...] = k
        v_ref[...] = v
        qkv_ref[:, s:2 * s] = k.astype(BF16)
        qkv_ref[:, 2 * s:3 * s] = v.astype(BF16)
    rest_ref[...] = jnp.dot(h, w_ref[:, 3 * s:], preferred_element_type=F32)


def _project(x2d, g, w_bf16, wkv_t, sb_width, tm, rows_per_stream):
    m, d = x2d.shape
    n = w_bf16.shape[1]
    n_rest = n - 3 * sb_width
    time_minor = rows_per_stream % tm == 0
    if time_minor:
        nt = rows_per_stream // tm
        kv_spec = pl.BlockSpec((1, sb_width, tm), lambda i: (i // nt, 0, i % nt))
        kv_shape = jax.ShapeDtypeStruct((m // rows_per_stream, sb_width, rows_per_stream), F32)
    else:
        kv_spec = pl.BlockSpec((tm, sb_width), lambda i: (i, 0))
        kv_shape = jax.ShapeDtypeStruct((m, sb_width), F32)
    return pl.pallas_call(
        functools.partial(_proj_kernel, sb_width=sb_width, time_minor=time_minor),
        grid=(m // tm,),
        in_specs=[
            pl.BlockSpec((tm, d), lambda i: (i, 0)),
            pl.BlockSpec((1, d), lambda i: (0, 0)),
            pl.BlockSpec((d, n), lambda i: (0, 0)),
            pl.BlockSpec((2 * sb_width, d), lambda i: (0, 0)),
        ],
        out_specs=[
            kv_spec,
            kv_spec,
            pl.BlockSpec((tm, 3 * sb_width), lambda i: (i, 0)),
            pl.BlockSpec((tm, n_rest), lambda i: (i, 0)),
        ],
        out_shape=[
            kv_shape,
            kv_shape,
            jax.ShapeDtypeStruct((m, 3 * sb_width), BF16),
            jax.ShapeDtypeStruct((m, n_rest), F32),
        ],
        compiler_params=pltpu.CompilerParams(
            dimension_semantics=("arbitrary",), vmem_limit_bytes=VMEM_LIMIT),
        name="proj",
    )(x2d, g, w_bf16, wkv_t)


def _memkv_kernel(x_ref, g_ref, w_ref, mk_ref, mv_ref, *, xa_width):
    h = _rms(x_ref[...], g_ref[...]).astype(BF16)
    kv = jnp.dot(h, w_ref[...], preferred_element_type=F32)
    mk_ref[...] = kv[:, :xa_width]
    mv_ref[...] = kv[:, xa_width:]


def _memory_kv(mem2d, g, w_bf16, tm):
    m, d = mem2d.shape
    xa_width = w_bf16.shape[1] // 2
    return pl.pallas_call(
        functools.partial(_memkv_kernel, xa_width=xa_width),
        grid=(m // tm,),
        in_specs=[
            pl.BlockSpec((tm, d), lambda i: (i, 0)),
            pl.BlockSpec((1, d), lambda i: (0, 0)),
            pl.BlockSpec((d, 2 * xa_width), lambda i: (0, 0)),
        ],
        out_specs=[
            pl.BlockSpec((tm, xa_width), lambda i: (i, 0)),
            pl.BlockSpec((tm, xa_width), lambda i: (i, 0)),
        ],
        out_shape=[
            jax.ShapeDtypeStruct((m, xa_width), F32),
            jax.ShapeDtypeStruct((m, xa_width), F32),
        ],
        compiler_params=pltpu.CompilerParams(
            dimension_semantics=("arbitrary",), vmem_limit_bytes=VMEM_LIMIT),
        name="memkv",
    )(mem2d, g, w_bf16)


def _strict_upper(n):
    r = lax.broadcasted_iota(jnp.int32, (n, n), 0)
    c = lax.broadcasted_iota(jnp.int32, (n, n), 1)
    return jnp.where(r > c, 1.0, 0.0).astype(BF16)


def _softplus(z):
    return jnp.maximum(z, 0.0) + jnp.log(1.0 + jnp.exp(-jnp.abs(z)))


def _sum_later(sp, upper):
    hi = sp.astype(BF16)
    lo = (sp - hi.astype(F32)).astype(BF16)
    return jnp.dot(jnp.concatenate([hi, lo], axis=1), jnp.concatenate([upper, upper], axis=0),
                   preferred_element_type=F32)


def _sb_block(qh, kblk, vblk, upper, carry, mask, transposed=False):
    if transposed:
        z = jnp.dot(qh, kblk, preferred_element_type=F32)
    else:
        z = lax.dot_general(qh, kblk, NT_DIMS, preferred_element_type=F32)
    sp = _softplus(z)
    if mask is not None:
        sp = jnp.where(mask, sp, 0.0)
    later = _sum_later(sp, upper)
    w = jnp.exp((z - sp) - later - carry)
    if mask is not None:
        w = jnp.where(mask, w, 0.0)
    if transposed:
        out = lax.dot_general(w.astype(BF16), vblk, NT_DIMS, preferred_element_type=F32)
    else:
        out = jnp.dot(w.astype(BF16), vblk, preferred_element_type=F32)
    return out, carry + jnp.sum(sp, axis=-1, keepdims=True)


def _still_live(state, n_blocks):
    lowest = functools.reduce(jnp.minimum, state[1:])
    return jnp.logical_and(state[0] < n_blocks, jnp.min(lowest) <= DEAD_CARRY)


def _sb_window_kernel(q_ref, k_ref, v_ref, o_ref, acc_ref, *, tq, pairs):
    i = pl.program_id(2)
    lane = lax.broadcasted_iota(jnp.int32, (tq, LANES), 1)
    first = lane < HEAD_DIM
    qs, cols = [], []
    for p in range(pairs):
        col = slice(p * LANES, (p + 1) * LANES)
        q = q_ref[0, :, col]
        zero = jnp.zeros_like(q)
        qs += [jnp.where(first, q, zero), jnp.where(first, zero, q)]
        cols += [col, col]
    n = len(qs)
    r = lax.broadcasted_iota(jnp.int32, (tq, tq), 0)
    c = lax.broadcasted_iota(jnp.int32, (tq, tq), 1)
    causal = c < r
    upper = _strict_upper(tq)

    @pl.when(i == 0)
    def _():
        zc = jnp.zeros((tq, 1), F32)
        for h in range(n):
            o, _ = _sb_block(qs[h], k_ref[0, 0:tq, cols[h]], v_ref[0, 0:tq, cols[h]],
                             upper, zc, causal)
            acc_ref[h] = o

    @pl.when(i > 0)
    def _():
        ws = pl.multiple_of((i - 1) * tq, tq)
        zs = [lax.dot_general(qs[h], k_ref[0, pl.ds(ws, 2 * tq), cols[h]], NT_DIMS,
                              preferred_element_type=F32) for h in range(n)]
        zp = [z[:, :tq] for z in zs]
        zd = [z[:, tq:] for z in zs]
        sp_p = [_softplus(z) for z in zp]
        sp_d = [jnp.where(causal, _softplus(z), 0.0) for z in zd]
        later = _sum_later(jnp.concatenate(sp_p + sp_d, axis=0), upper)
        carries = []
        for h in range(n):
            rs_d = jnp.sum(sp_d[h], axis=-1, keepdims=True)
            later_p = later[h * tq:(h + 1) * tq]
            later_d = later[(n + h) * tq:(n + h + 1) * tq]
            w_p = jnp.exp((zp[h] - sp_p[h]) - later_p - rs_d)
            w_d = jnp.where(causal, jnp.exp((zd[h] - sp_d[h]) - later_d), 0.0)
            w = jnp.concatenate([w_p.astype(BF16), w_d.astype(BF16)], axis=1)
            acc_ref[h] = jnp.dot(w, v_ref[0, pl.ds(ws, 2 * tq), cols[h]],
                                 preferred_element_type=F32)
            carries.append(rs_d + jnp.sum(sp_p[h], axis=-1, keepdims=True))

        n_older = i - 1

        def body(state):
            j = state[0]
            start = pl.multiple_of((n_older - 1 - j) * tq, tq)
            new = []
            for h in range(n):
                o, ch = _sb_block(qs[h], k_ref[0, pl.ds(start, tq), cols[h]],
                                  v_ref[0, pl.ds(start, tq), cols[h]], upper, state[1 + h], None)
                acc_ref[h] += o
                new.append(ch)
            return (j + 1, *new)

        lax.while_loop(functools.partial(_still_live, n_blocks=n_older), body,
                       (jnp.int32(0), *carries))

    for p in range(pairs):
        o_ref[0, :, p * LANES:(p + 1) * LANES] = jnp.where(
            first, acc_ref[2 * p], acc_ref[2 * p + 1])


def _stick_breaking_self(qkv, *, tq, sb_width, pairs_per_step):
    b, t, _ = qkv.shape
    width = pairs_per_step * LANES
    groups = sb_width // width
    return pl.pallas_call(
        functools.partial(_sb_window_kernel, tq=tq, pairs=pairs_per_step),
        grid=(b, groups, t // tq),
        in_specs=[
            pl.BlockSpec((1, tq, width), lambda bi, g, i: (bi, i, g)),
            pl.BlockSpec((1, t, width), lambda bi, g, i: (bi, 0, groups + g)),
            pl.BlockSpec((1, t, width), lambda bi, g, i: (bi, 0, 2 * groups + g)),
        ],
        out_specs=pl.BlockSpec((1, tq, width), lambda bi, g, i: (bi, i, g)),
        out_shape=jax.ShapeDtypeStruct((b, t, sb_width), F32),
        scratch_shapes=[pltpu.VMEM((2 * pairs_per_step, tq, LANES), F32)],
        compiler_params=pltpu.CompilerParams(
            dimension_semantics=("arbitrary", "arbitrary", "arbitrary"),
            vmem_limit_bytes=VMEM_LIMIT),
        name="stick_breaking_self",
    )(qkv, qkv, qkv)


def _sb_cached_kernel(qkv_ref, kwin_ref, vwin_ref, kc_ref, vc_ref, o_ref,
                      kbuf, vbuf, acc_ref, sem, *, heads, tk, n_older):
    b = pl.program_id(0)
    tq = qkv_ref.shape[1]
    s = heads * HEAD_DIM
    r = lax.broadcasted_iota(jnp.int32, (tq, tq), 0)
    c = lax.broadcasted_iota(jnp.int32, (tq, tq), 1)
    causal = c < r
    upper_d = _strict_upper(tq)
    upper = _strict_upper(tk)
    q = qkv_ref[0, :, 0:s]
    kn = qkv_ref[0, :, s:2 * s]
    vn = qkv_ref[0, :, 2 * s:3 * s]
    head = lambda x, h: x[:, h * HEAD_DIM:(h + 1) * HEAD_DIM]
    qs = [head(q, h) for h in range(heads)]

    zd = [lax.dot_general(qs[h], head(kn, h), NT_DIMS, preferred_element_type=F32)
          for h in range(heads)]
    zw = [jnp.dot(qs[h], kwin_ref[0, h].astype(BF16), preferred_element_type=F32)
          for h in range(heads)]
    sp_d = [jnp.where(causal, _softplus(z), 0.0) for z in zd]
    sp_w = [_softplus(z) for z in zw]
    later_d = _sum_later(jnp.concatenate(sp_d, axis=0), upper_d)
    later_w = _sum_later(jnp.concatenate(sp_w, axis=0), upper)
    carries = []
    for h in range(heads):
        rows = slice(h * tq, (h + 1) * tq)
        rs_d = jnp.sum(sp_d[h], axis=-1, keepdims=True)
        w_d = jnp.where(causal, jnp.exp((zd[h] - sp_d[h]) - later_d[rows]), 0.0)
        w_w = jnp.exp((zw[h] - sp_w[h]) - later_w[rows] - rs_d)
        acc_ref[h] = (jnp.dot(w_d.astype(BF16), head(vn, h), preferred_element_type=F32)
                      + lax.dot_general(w_w.astype(BF16), vwin_ref[0, h].astype(BF16), NT_DIMS,
                                        preferred_element_type=F32))
        carries.append(rs_d + jnp.sum(sp_w[h], axis=-1, keepdims=True))

    def block_copies(j):
        start = pl.multiple_of((n_older - 1 - j) * tk, tk)
        return (pltpu.make_async_copy(kc_ref.at[b, :, :, pl.ds(start, tk)], kbuf, sem.at[0]),
                pltpu.make_async_copy(vc_ref.at[b, :, :, pl.ds(start, tk)], vbuf, sem.at[1]))

    def body(state):
        j = state[0]
        copies = block_copies(j)
        for cp in copies:
            cp.start()
        for cp in copies:
            cp.wait()
        new = []
        for h in range(heads):
            o, ch = _sb_block(qs[h], kbuf[h].astype(BF16), vbuf[h].astype(BF16),
                              upper, state[1 + h], None, transposed=True)
            acc_ref[h] += o
            new.append(ch)
        return (j + 1, *new)

    lax.while_loop(functools.partial(_still_live, n_blocks=n_older), body,
                   (jnp.int32(0), *carries))
    o_ref[0] = jnp.concatenate([acc_ref[h] for h in range(heads)], axis=1)


def _stick_breaking_cached(qkv, cache_k, cache_v, *, tk):
    b, t, _ = qkv.shape
    _, heads, dh, p = cache_k.shape
    s = heads * dh
    last = p // tk - 1
    win = pl.BlockSpec((1, heads, dh, tk), lambda bi: (bi, 0, 0, last))
    return pl.pallas_call(
        functools.partial(_sb_cached_kernel, heads=heads, tk=tk, n_older=last),
        grid=(b,),
        in_specs=[
            pl.BlockSpec((1, t, 3 * s), lambda bi: (bi, 0, 0)),
            win, win,
            pl.BlockSpec(memory_space=pl.ANY),
            pl.BlockSpec(memory_space=pl.ANY),
        ],
        out_specs=pl.BlockSpec((1, t, s), lambda bi: (bi, 0, 0)),
        out_shape=jax.ShapeDtypeStruct((b, t, s), F32),
        scratch_shapes=[
            pltpu.VMEM((heads, dh, tk), F32),
            pltpu.VMEM((heads, dh, tk), F32),
            pltpu.VMEM((heads, t, dh), F32),
            pltpu.SemaphoreType.DMA((2,)),
        ],
        compiler_params=pltpu.CompilerParams(
            dimension_semantics=("arbitrary",), vmem_limit_bytes=VMEM_LIMIT),
        name="stick_breaking_cached",
    )(qkv, cache_k, cache_v, cache_k, cache_v)


def _silu(g):
    return g / (1.0 + jnp.exp(-g))


def _epilogue_kernel(x_ref, osb_ref, rest_ref, prev_ref, hist0_ref, mk_ref, mv_ref,
                     poolw_ref, pscale_ref, wout_ref, gfin_ref, y_ref,
                     *, tt, start, sb_width, pool_width, xa_width, xa_heads, apply_final):
    i = pl.program_id(1)
    s, pw, xw = sb_width, pool_width, xa_width
    g_sb = rest_ref[0, :, 0:s]
    u = rest_ref[0, :, s:s + pw]
    g_pool = rest_ref[0, :, s + pw:s + 2 * pw]
    q_xa = rest_ref[0, :, s + 2 * pw:s + 2 * pw + xw]
    g_xa = rest_ref[0, :, s + 2 * pw + xw:s + 2 * pw + 2 * xw]

    hist = jnp.where(i == 0, hist0_ref[0], prev_ref[0])
    ext = jnp.concatenate([hist, u], axis=0)
    sums = {1: ext}
    width = 1
    while width < max(POOL_WINDOWS):
        prev = sums[width]
        sums[2 * width] = prev + pltpu.roll(prev, width, 0)
        width *= 2
    lane = lax.broadcasted_iota(jnp.int32, (tt, pw), 1)
    group = lane // (pw // len(POOL_WINDOWS))
    pos = start + i * tt + lax.broadcasted_iota(jnp.int32, (tt, pw), 0)
    win_sum = jnp.zeros((tt, pw), F32)
    win = jnp.zeros((tt, pw), jnp.int32)
    for gi, w in enumerate(POOL_WINDOWS):
        sel = group == gi
        win_sum = jnp.where(sel, sums[w][HIST_ROWS:], win_sum)
        win = jnp.where(sel, w, win)
    cnt = jnp.minimum(pos + 1, win).astype(F32)
    pooled = win_sum / cnt - u
    o_pool = jnp.dot(pooled.astype(BF16), poolw_ref[...],
                     preferred_element_type=F32) * pscale_ref[...]

    mk = mk_ref[0].astype(BF16)
    mv = mv_ref[0].astype(BF16)
    xa_dim = xw // xa_heads
    lane_x = lax.broadcasted_iota(jnp.int32, (tt, xw), 1)
    head_x = lane_x // xa_dim
    qs = q_xa * (xa_dim ** -0.5)
    o_xa = jnp.zeros((tt, xw), F32)
    for hh in range(xa_heads):
        sel = head_x == hh
        qh = jnp.where(sel, qs, 0.0).astype(BF16)
        sc = lax.dot_general(qh, mk, NT_DIMS, preferred_element_type=F32)
        e = jnp.exp(sc - jnp.max(sc, axis=-1, keepdims=True))
        p = e / jnp.sum(e, axis=-1, keepdims=True)
        oh = jnp.dot(p.astype(BF16), mv, preferred_element_type=F32)
        o_xa = jnp.where(sel, oh, o_xa)

    m_sb = (osb_ref[0] * _silu(g_sb)).astype(BF16)
    m_pool = (o_pool * _silu(g_pool)).astype(BF16)
    m_xa = (o_xa * _silu(g_xa)).astype(BF16)
    y = (x_ref[0]
         + jnp.dot(m_sb, wout_ref[0:s, :], preferred_element_type=F32)
         + jnp.dot(m_pool, wout_ref[s:s + pw, :], preferred_element_type=F32)
         + jnp.dot(m_xa, wout_ref[s + pw:s + pw + xw, :], preferred_element_type=F32))
    if apply_final:
        y = _rms(y, gfin_ref[...])
    y_ref[0] = y


def _epilogue(x, o_sb, rest, hist0, mk, mv, pool_bd, pool_scale, w_out_bf16, g_final,
              *, tt, start, sb_width, pool_width, xa_width, xa_heads, apply_final):
    b, t, d = x.shape
    n_rest = rest.shape[-1]
    n_mem = mk.shape[1]
    hist_blocks = tt // HIST_ROWS
    u_col = sb_width // pool_width
    kern = functools.partial(
        _epilogue_kernel, tt=tt, start=start, sb_width=sb_width, pool_width=pool_width,
        xa_width=xa_width, xa_heads=xa_heads, apply_final=apply_final)
    return pl.pallas_call(
        kern,
        grid=(b, t // tt),
        in_specs=[
            pl.BlockSpec((1, tt, d), lambda bi, i: (bi, i, 0)),
            pl.BlockSpec((1, tt, sb_width), lambda bi, i: (bi, i, 0)),
            pl.BlockSpec((1, tt, n_rest), lambda bi, i: (bi, i, 0)),
            pl.BlockSpec((1, HIST_ROWS, pool_width),
                         lambda bi, i: (bi, jnp.maximum(i * hist_blocks - 1, 0), u_col)),
            pl.BlockSpec((1, HIST_ROWS, pool_width), lambda bi, i: (bi, 0, 0)),
            pl.BlockSpec((1, n_mem, xa_width), lambda bi, i: (bi, 0, 0)),
            pl.BlockSpec((1, n_mem, xa_width), lambda bi, i: (bi, 0, 0)),
            pl.BlockSpec((pool_width, pool_width), lambda bi, i: (0, 0)),
            pl.BlockSpec((1, pool_width), lambda bi, i: (0, 0)),
            pl.BlockSpec((d, d), lambda bi, i: (0, 0)),
            pl.BlockSpec((1, d), lambda bi, i: (0, 0)),
        ],
        out_specs=pl.BlockSpec((1, tt, d), lambda bi, i: (bi, i, 0)),
        out_shape=jax.ShapeDtypeStruct((b, t, d), F32),
        compiler_params=pltpu.CompilerParams(
            dimension_semantics=("arbitrary", "arbitrary"), vmem_limit_bytes=VMEM_LIMIT),
        name="epilogue",
    )(x, o_sb, rest, rest, hist0, mk, mv, pool_bd, pool_scale, w_out_bf16, g_final)


def _block_diag(pool_w):
    g, c, _ = pool_w.shape
    eye = jnp.eye(g, dtype=pool_w.dtype)
    return (eye[:, None, :, None] * pool_w[:, :, None, :]).reshape(g * c, g * c)


def _pad_hist(hist):
    return jnp.pad(hist, ((0, 0), (HIST_ROWS - hist.shape[1], 0), (0, 0)))


def _mixer_layer(x, start, k_past, v_past, pool_hist, mk, mv, g_norm, w_in_bf16, wkv_t, pool_bd,
                 pool_scale, w_out_bf16, g_final, *, dims, tiles, apply_final):
    sb_width, pool_width, xa_width, xa_heads = dims
    tm, tq, tk, tt = tiles
    b, t, d = x.shape
    heads = sb_width // HEAD_DIM
    k_new, v_new, qkv, rest = _project(x.reshape(b * t, d), g_norm, w_in_bf16, wkv_t,
                                       sb_width, tm, t)
    if k_new.ndim == 3:
        to_cache = lambda a: jnp.transpose(a.reshape(b, heads, HEAD_DIM, t), (0, 3, 1, 2))
    else:
        to_cache = lambda a: a.reshape(b, t, heads, HEAD_DIM)
    qkv = qkv.reshape(b, t, -1)
    rest = rest.reshape(b, t, -1)
    if k_past is None:
        o_sb = _stick_breaking_self(qkv, tq=tq, sb_width=sb_width,
                                    pairs_per_step=PAIRS_PER_STEP)
    else:
        o_sb = _stick_breaking_cached(qkv, k_past, v_past, tk=tk)
    y = _epilogue(x, o_sb, rest, _pad_hist(pool_hist), mk, mv, pool_bd, pool_scale,
                  w_out_bf16, g_final, tt=tt, start=start, sb_width=sb_width,
                  pool_width=pool_width, xa_width=xa_width, xa_heads=xa_heads,
                  apply_final=apply_final)
    u_pool = rest[:, :, sb_width:sb_width + pool_width]
    new_hist = jnp.concatenate([pool_hist, u_pool], axis=1)[:, -POOL_STATE:]
    return y, to_cache(k_new), to_cache(v_new), new_hist


def kernel(x_prompt, x_sample, cache_sb_k, cache_sb_v, state_pool, cache_mem_k, cache_mem_v,
           mem_prompt, g_norm, w_in, pool_w, pool_scale, g_mem, w_mem_kv, w_out, g_final):
    depth = g_norm.shape[0]
    bp, tp, d = x_prompt.shape
    bs, ts, _ = x_sample.shape
    past = cache_sb_k.shape[2]
    sb_heads, head_dim = cache_sb_k.shape[3], cache_sb_k.shape[4]
    assert head_dim == HEAD_DIM
    sb_width = sb_heads * head_dim
    pool_width = pool_w.shape[1] * pool_w.shape[2]
    n_mem, xa_heads, xa_dim = cache_mem_k.shape[2:]
    xa_width = xa_heads * xa_dim
    dims = (sb_width, pool_width, xa_width, xa_heads)
    g_fin = g_final.reshape(1, d)

    yp, ys = x_prompt, x_sample
    kp_l, vp_l, hp_l, mkp_l, mvp_l, ks_l, vs_l, hs_l = [], [], [], [], [], [], [], []
    for l in range(depth):
        final = l == depth - 1
        g_l = g_norm[l].reshape(1, d)
        w_in_b = w_in[l].astype(BF16)
        w_out_b = w_out[l].astype(BF16)
        pool_bd = _block_diag(pool_w[l]).astype(BF16)
        p_scale = pool_scale[l].reshape(1, pool_width)
        mk, mv = _memory_kv(mem_prompt.reshape(bp * n_mem, d), g_mem[l].reshape(1, d),
                            w_mem_kv[l].astype(BF16), tm=256)
        mk = mk.reshape(bp, n_mem, xa_width)
        mv = mv.reshape(bp, n_mem, xa_width)
        hist0 = jnp.zeros((bp, POOL_STATE, pool_width), x_prompt.dtype)
        wkv_t = jnp.transpose(w_in[l][:, sb_width:3 * sb_width]).astype(BF16)
        yp, kp, vp, hp = _mixer_layer(
            yp, 0, None, None, hist0, mk, mv, g_l, w_in_b, wkv_t, pool_bd, p_scale, w_out_b,
            g_fin, dims=dims, tiles=(512, 256, 256, 1024), apply_final=final)
        ys, kn, vn, hn = _mixer_layer(
            ys, past, jnp.transpose(cache_sb_k[l], (0, 2, 3, 1)),
            jnp.transpose(cache_sb_v[l], (0, 2, 3, 1)), state_pool[l],
            cache_mem_k[l].reshape(bs, n_mem, xa_width),
            cache_mem_v[l].reshape(bs, n_mem, xa_width),
            g_l, w_in_b, wkv_t, pool_bd, p_scale, w_out_b, g_fin,
            dims=dims, tiles=(512, ts, 256, ts), apply_final=final)
        kp_l.append(kp)
        vp_l.append(vp)
        hp_l.append(hp)
        mkp_l.append(mk.reshape(bp, n_mem, xa_heads, xa_dim))
        mvp_l.append(mv.reshape(bp, n_mem, xa_heads, xa_dim))
        ks_l.append(kn)
        vs_l.append(vn)
        hs_l.append(hn)
    stack = lambda xs: jnp.stack(xs, axis=0)
    return (yp, ys, stack(kp_l), stack(vp_l), stack(hp_l), stack(mkp_l), stack(mvp_l),
            stack(ks_l), stack(vs_l), stack(hs_l))
```

```python
import functools

import jax
import jax.numpy as jnp
from jax import lax
from jax.experimental import pallas as pl
from jax.experimental.pallas import tpu as pltpu

F32 = jnp.float32
BF16 = jnp.bfloat16

HEAD_DIM = 64
LANES = 128
POOL_WINDOWS = (2, 4, 8, 16)
POOL_STATE = max(POOL_WINDOWS) - 1
HIST_ROWS = 16
EPS = 1e-6
DEAD_CARRY = 104.0
VMEM_LIMIT = 56 * 1024 * 1024
NT_DIMS = (((1,), (1,)), ((), ()))
PAIRS_PER_STEP = 2
EPILOGUE_ROWS = 256


def _rms(x, g):
    ms = jnp.mean(x * x, axis=-1, keepdims=True)
    return (x * lax.rsqrt(ms + EPS)) * g


def _proj_kernel(x_ref, g_ref, w_ref, wkv_t_ref, k_ref, v_ref, qkv_ref, rest_ref,
                 *, sb_width, time_minor):
    h = _rms(x_ref[...], g_ref[...]).astype(BF16)
    s = sb_width
    q = jnp.dot(h, w_ref[:, 0:s], preferred_element_type=F32)
    qkv_ref[:, 0:s] = (q * (HEAD_DIM ** -0.5)).astype(BF16)
    if time_minor:
        kt = lax.dot_general(wkv_t_ref[0:s, :], h, NT_DIMS, preferred_element_type=F32)
        vt = lax.dot_general(wkv_t_ref[s:2 * s, :], h, NT_DIMS, preferred_element_type=F32)
        k_ref[0] = kt
        v_ref[0] = vt
        qkv_ref[:, s:2 * s] = kt.T.astype(BF16)
        qkv_ref[:, 2 * s:3 * s] = vt.T.astype(BF16)
    else:
        k = jnp.dot(h, w_ref[:, s:2 * s], preferred_element_type=F32)
        v = jnp.dot(h, w_ref[:, 2 * s:3 * s], preferred_element_type=F32)
        k_ref[...] = k
        v_ref[...] = v
        qkv_ref[:, s:2 * s] = k.astype(BF16)
        qkv_ref[:, 2 * s:3 * s] = v.astype(BF16)
    rest_ref[...] = jnp.dot(h, w_ref[:, 3 * s:], preferred_element_type=F32)


def _project(x2d, g, w_bf16, wkv_t, sb_width, tm, rows_per_stream):
    m, d = x2d.shape
    n = w_bf16.shape[1]
    n_rest = n - 3 * sb_width
    time_minor = rows_per_stream % tm == 0
    if time_minor:
        nt = rows_per_stream // tm
        kv_spec = pl.BlockSpec((1, sb_width, tm), lambda i: (i // nt, 0, i % nt))
        kv_shape = jax.ShapeDtypeStruct((m // rows_per_stream, sb_width, rows_per_stream), F32)
    else:
        kv_spec = pl.BlockSpec((tm, sb_width), lambda i: (i, 0))
        kv_shape = jax.ShapeDtypeStruct((m, sb_width), F32)
    return pl.pallas_call(
        functools.partial(_proj_kernel, sb_width=sb_width, time_minor=time_minor),
        grid=(m // tm,),
        in_specs=[
            pl.BlockSpec((tm, d), lambda i: (i, 0)),
            pl.BlockSpec((1, d), lambda i: (0, 0)),
            pl.BlockSpec((d, n), lambda i: (0, 0)),
            pl.BlockSpec((2 * sb_width, d), lambda i: (0, 0)),
        ],
        out_specs=[
            kv_spec,
            kv_spec,
            pl.BlockSpec((tm, 3 * sb_width), lambda i: (i, 0)),
            pl.BlockSpec((tm, n_rest), lambda i: (i, 0)),
        ],
        out_shape=[
            kv_shape,
            kv_shape,
            jax.ShapeDtypeStruct((m, 3 * sb_width), BF16),
            jax.ShapeDtypeStruct((m, n_rest), F32),
        ],
        compiler_params=pltpu.CompilerParams(
            dimension_semantics=("arbitrary",), vmem_limit_bytes=VMEM_LIMIT),
        name="proj",
    )(x2d, g, w_bf16, wkv_t)


def _memkv_kernel(x_ref, g_ref, w_ref, mk_ref, mv_ref, *, xa_width):
    h = _rms(x_ref[...], g_ref[...]).astype(BF16)
    kv = jnp.dot(h, w_ref[...], preferred_element_type=F32)
    mk_ref[...] = kv[:, :xa_width]
    mv_ref[...] = kv[:, xa_width:]


def _memory_kv(mem2d, g, w_bf16, tm):
    m, d = mem2d.shape
    xa_width = w_bf16.shape[1] // 2
    return pl.pallas_call(
        functools.partial(_memkv_kernel, xa_width=xa_width),
        grid=(m // tm,),
        in_specs=[
            pl.BlockSpec((tm, d), lambda i: (i, 0)),
            pl.BlockSpec((1, d), lambda i: (0, 0)),
            pl.BlockSpec((d, 2 * xa_width), lambda i: (0, 0)),
        ],
        out_specs=[
            pl.BlockSpec((tm, xa_width), lambda i: (i, 0)),
            pl.BlockSpec((tm, xa_width), lambda i: (i, 0)),
        ],
        out_shape=[
            jax.ShapeDtypeStruct((m, xa_width), F32),
            jax.ShapeDtypeStruct((m, xa_width), F32),
        ],
        compiler_params=pltpu.CompilerParams(
            dimension_semantics=("arbitrary",), vmem_limit_bytes=VMEM_LIMIT),
        name="memkv",
    )(mem2d, g, w_bf16)


def _strict_upper(n):
    r = lax.broadcasted_iota(jnp.int32, (n, n), 0)
    c = lax.broadcasted_iota(jnp.int32, (n, n), 1)
    return jnp.where(r > c, 1.0, 0.0).astype(BF16)


def _softplus(z):
    return jnp.maximum(z, 0.0) + jnp.log(1.0 + jnp.exp(-jnp.abs(z)))


def _sum_later(sp, upper):
    hi = sp.astype(BF16)
    lo = (sp - hi.astype(F32)).astype(BF16)
    return jnp.dot(jnp.concatenate([hi, lo], axis=1), jnp.concatenate([upper, upper], axis=0),
                   preferred_element_type=F32)


def _sb_block(qh, kblk, vblk, upper, carry, mask, transposed=False):
    if transposed:
        z = jnp.dot(qh, kblk, preferred_element_type=F32)
    else:
        z = lax.dot_general(qh, kblk, NT_DIMS, preferred_element_type=F32)
    sp = _softplus(z)
    if mask is not None:
        sp = jnp.where(mask, sp, 0.0)
    later = _sum_later(sp, upper)
    w = jnp.exp((z - sp) - later - carry)
    if mask is not None:
        w = jnp.where(mask, w, 0.0)
    if transposed:
        out = lax.dot_general(w.astype(BF16), vblk, NT_DIMS, preferred_element_type=F32)
    else:
        out = jnp.dot(w.astype(BF16), vblk, preferred_element_type=F32)
    return out, carry + jnp.sum(sp, axis=-1, keepdims=True)


def _still_live(state, n_blocks):
    lowest = functools.reduce(jnp.minimum, state[1:])
    return jnp.logical_and(state[0] < n_blocks, jnp.min(lowest) <= DEAD_CARRY)


def _sb_window_kernel(q_ref, k_ref, v_ref, o_ref, acc_ref, *, tq, pairs):
    i = pl.program_id(2)
    lane = lax.broadcasted_iota(jnp.int32, (tq, LANES), 1)
    first = lane < HEAD_DIM
    qs, cols = [], []
    for p in range(pairs):
        col = slice(p * LANES, (p + 1) * LANES)
        q = q_ref[0, :, col]
        zero = jnp.zeros_like(q)
        qs += [jnp.where(first, q, zero), jnp.where(first, zero, q)]
        cols += [col, col]
    n = len(qs)
    r = lax.broadcasted_iota(jnp.int32, (tq, tq), 0)
    c = lax.broadcasted_iota(jnp.int32, (tq, tq), 1)
    causal = c < r
    upper = _strict_upper(tq)

    @pl.when(i == 0)
    def _():
        zc = jnp.zeros((tq, 1), F32)
        for h in range(n):
            o, _ = _sb_block(qs[h], k_ref[0, 0:tq, cols[h]], v_ref[0, 0:tq, cols[h]],
                             upper, zc, causal)
            acc_ref[h] = o

    @pl.when(i > 0)
    def _():
        ws = pl.multiple_of((i - 1) * tq, tq)
        zs = [lax.dot_general(qs[h], k_ref[0, pl.ds(ws, 2 * tq), cols[h]], NT_DIMS,
                              preferred_element_type=F32) for h in range(n)]
        zp = [z[:, :tq] for z in zs]
        zd = [z[:, tq:] for z in zs]
        sp_p = [_softplus(z) for z in zp]
        sp_d = [jnp.where(causal, _softplus(z), 0.0) for z in zd]
        later = _sum_later(jnp.concatenate(sp_p + sp_d, axis=0), upper)
        carries = []
        for h in range(n):
            rs_d = jnp.sum(sp_d[h], axis=-1, keepdims=True)
            later_p = later[h * tq:(h + 1) * tq]
            later_d = later[(n + h) * tq:(n + h + 1) * tq]
            w_p = jnp.exp((zp[h] - sp_p[h]) - later_p - rs_d)
            w_d = jnp.where(causal, jnp.exp((zd[h] - sp_d[h]) - later_d), 0.0)
            w = jnp.concatenate([w_p.astype(BF16), w_d.astype(BF16)], axis=1)
            acc_ref[h] = jnp.dot(w, v_ref[0, pl.ds(ws, 2 * tq), cols[h]],
                                 preferred_element_type=F32)
            carries.append(rs_d + jnp.sum(sp_p[h], axis=-1, keepdims=True))

        n_older = i - 1

        def body(state):
            j = state[0]
            start = pl.multiple_of((n_older - 1 - j) * tq, tq)
            new = []
            for h in range(n):
                o, ch = _sb_block(qs[h], k_ref[0, pl.ds(start, tq), cols[h]],
                                  v_ref[0, pl.ds(start, tq), cols[h]], upper, state[1 + h], None)
                acc_ref[h] += o
                new.append(ch)
            return (j + 1, *new)

        lax.while_loop(functools.partial(_still_live, n_blocks=n_older), body,
                       (jnp.int32(0), *carries))

    for p in range(pairs):
        o_ref[0, :, p * LANES:(p + 1) * LANES] = jnp.where(
            first, acc_ref[2 * p], acc_ref[2 * p + 1])


def _stick_breaking_self(qkv, *, tq, sb_width, pairs_per_step):
    b, t, _ = qkv.shape
    width = pairs_per_step * LANES
    groups = sb_width // width
    return pl.pallas_call(
        functools.partial(_sb_window_kernel, tq=tq, pairs=pairs_per_step),
        grid=(b, groups, t // tq),
        in_specs=[
            pl.BlockSpec((1, tq, width), lambda bi, g, i: (bi, i, g)),
            pl.BlockSpec((1, t, width), lambda bi, g, i: (bi, 0, groups + g)),
            pl.BlockSpec((1, t, width), lambda bi, g, i: (bi, 0, 2 * groups + g)),
        ],
        out_specs=pl.BlockSpec((1, tq, width), lambda bi, g, i: (bi, i, g)),
        out_shape=jax.ShapeDtypeStruct((b, t, sb_width), F32),
        scratch_shapes=[pltpu.VMEM((2 * pairs_per_step, tq, LANES), F32)],
        compiler_params=pltpu.CompilerParams(
            dimension_semantics=("arbitrary", "arbitrary", "arbitrary"),
            vmem_limit_bytes=VMEM_LIMIT),
        name="stick_breaking_self",
    )(qkv, qkv, qkv)


def _sb_cached_kernel(qkv_ref, kwin_ref, vwin_ref, kc_ref, vc_ref, o_ref,
                      kbuf, vbuf, acc_ref, sem, *, heads, tk, n_older):
    b = pl.program_id(0)
    tq = qkv_ref.shape[1]
    s = heads * HEAD_DIM
    r = lax.broadcasted_iota(jnp.int32, (tq, tq), 0)
    c = lax.broadcasted_iota(jnp.int32, (tq, tq), 1)
    causal = c < r
    upper_d = _strict_upper(tq)
    upper = _strict_upper(tk)
    q = qkv_ref[0, :, 0:s]
    kn = qkv_ref[0, :, s:2 * s]
    vn = qkv_ref[0, :, 2 * s:3 * s]
    head = lambda x, h: x[:, h * HEAD_DIM:(h + 1) * HEAD_DIM]
    qs = [head(q, h) for h in range(heads)]

    zd = [lax.dot_general(qs[h], head(kn, h), NT_DIMS, preferred_element_type=F32)
          for h in range(heads)]
    zw = [jnp.dot(qs[h], kwin_ref[0, h].astype(BF16), preferred_element_type=F32)
          for h in range(heads)]
    sp_d = [jnp.where(causal, _softplus(z), 0.0) for z in zd]
    sp_w = [_softplus(z) for z in zw]
    later_d = _sum_later(jnp.concatenate(sp_d, axis=0), upper_d)
    later_w = _sum_later(jnp.concatenate(sp_w, axis=0), upper)
    carries = []
    for h in range(heads):
        rows = slice(h * tq, (h + 1) * tq)
        rs_d = jnp.sum(sp_d[h], axis=-1, keepdims=True)
        w_d = jnp.where(causal, jnp.exp((zd[h] - sp_d[h]) - later_d[rows]), 0.0)
        w_w = jnp.exp((zw[h] - sp_w[h]) - later_w[rows] - rs_d)
        acc_ref[h] = (jnp.dot(w_d.astype(BF16), head(vn, h), preferred_element_type=F32)
                      + lax.dot_general(w_w.astype(BF16), vwin_ref[0, h].astype(BF16), NT_DIMS,
                                        preferred_element_type=F32))
        carries.append(rs_d + jnp.sum(sp_w[h], axis=-1, keepdims=True))

    def block_copies(j):
        start = pl.multiple_of((n_older - 1 - j) * tk, tk)
        return (pltpu.make_async_copy(kc_ref.at[b, :, :, pl.ds(start, tk)], kbuf, sem.at[0]),
                pltpu.make_async_copy(vc_ref.at[b, :, :, pl.ds(start, tk)], vbuf, sem.at[1]))

    def body(state):
        j = state[0]
        copies = block_copies(j)
        for cp in copies:
            cp.start()
        for cp in copies:
            cp.wait()
        new = []
        for h in range(heads):
            o, ch = _sb_block(qs[h], kbuf[h].astype(BF16), vbuf[h].astype(BF16),
                              upper, state[1 + h], None, transposed=True)
            acc_ref[h] += o
            new.append(ch)
        return (j + 1, *new)

    lax.while_loop(functools.partial(_still_live, n_blocks=n_older), body,
                   (jnp.int32(0), *carries))
    o_ref[0] = jnp.concatenate([acc_ref[h] for h in range(heads)], axis=1)


def _stick_breaking_cached(qkv, cache_k, cache_v, *, tk):
    b, t, _ = qkv.shape
    _, heads, dh, p = cache_k.shape
    s = heads * dh
    last = p // tk - 1
    win = pl.BlockSpec((1, heads, dh, tk), lambda bi: (bi, 0, 0, last))
    return pl.pallas_call(
        functools.partial(_sb_cached_kernel, heads=heads, tk=tk, n_older=last),
        grid=(b,),
        in_specs=[
            pl.BlockSpec((1, t, 3 * s), lambda bi: (bi, 0, 0)),
            win, win,
            pl.BlockSpec(memory_space=pl.ANY),
            pl.BlockSpec(memory_space=pl.ANY),
        ],
        out_specs=pl.BlockSpec((1, t, s), lambda bi: (bi, 0, 0)),
        out_shape=jax.ShapeDtypeStruct((b, t, s), F32),
        scratch_shapes=[
            pltpu.VMEM((heads, dh, tk), F32),
            pltpu.VMEM((heads, dh, tk), F32),
            pltpu.VMEM((heads, t, dh), F32),
            pltpu.SemaphoreType.DMA((2,)),
        ],
        compiler_params=pltpu.CompilerParams(
            dimension_semantics=("arbitrary",), vmem_limit_bytes=VMEM_LIMIT),
        name="stick_breaking_cached",
    )(qkv, cache_k, cache_v, cache_k, cache_v)


def _silu(g):
    return g / (1.0 + jnp.exp(-g))


def _epilogue_kernel(x_ref, osb_ref, rest_ref, prev_ref, hist0_ref, mk_ref, mv_ref,
                     poolw_ref, pscale_ref, wout_ref, gfin_ref, y_ref,
                     *, tt, start, sb_width, pool_width, xa_width, xa_heads, apply_final):
    streams = x_ref.shape[0]
    mixed = [_mixed_rows(si, rest_ref, osb_ref, prev_ref, hist0_ref, mk_ref, mv_ref, poolw_ref,
                         pscale_ref, tt=tt, start=start, sb_width=sb_width,
                         pool_width=pool_width, xa_width=xa_width, xa_heads=xa_heads)
             for si in range(streams)]
    m_sb, m_pool, m_xa = (jnp.concatenate([m[k] for m in mixed], axis=0) for k in range(3))
    s, pw, xw = sb_width, pool_width, xa_width
    proj = (jnp.dot(m_sb, wout_ref[0:s, :], preferred_element_type=F32)
            + jnp.dot(m_pool, wout_ref[s:s + pw, :], preferred_element_type=F32)
            + jnp.dot(m_xa, wout_ref[s + pw:s + pw + xw, :], preferred_element_type=F32))
    for si in range(streams):
        y = x_ref[si] + proj[si * tt:(si + 1) * tt]
        if apply_final:
            y = _rms(y, gfin_ref[...])
        y_ref[si] = y


def _mixed_rows(si, rest_ref, osb_ref, prev_ref, hist0_ref, mk_ref, mv_ref, poolw_ref, pscale_ref,
                *, tt, start, sb_width, pool_width, xa_width, xa_heads):
    i = pl.program_id(1)
    s, pw, xw = sb_width, pool_width, xa_width
    g_sb = rest_ref[si, :, 0:s]
    u = rest_ref[si, :, s:s + pw]
    g_pool = rest_ref[si, :, s + pw:s + 2 * pw]
    q_xa = rest_ref[si, :, s + 2 * pw:s + 2 * pw + xw]
    g_xa = rest_ref[si, :, s + 2 * pw + xw:s + 2 * pw + 2 * xw]

    hist = jnp.where(i == 0, hist0_ref[si], prev_ref[si])
    ext = jnp.concatenate([hist, u], axis=0)
    sums = {1: ext}
    width = 1
    while width < max(POOL_WINDOWS):
        prev = sums[width]
        sums[2 * width] = prev + pltpu.roll(prev, width, 0)
        width *= 2
    lane = lax.broadcasted_iota(jnp.int32, (tt, pw), 1)
    group = lane // (pw // len(POOL_WINDOWS))
    pos = start + i * tt + lax.broadcasted_iota(jnp.int32, (tt, pw), 0)
    win_sum = jnp.zeros((tt, pw), F32)
    win = jnp.zeros((tt, pw), jnp.int32)
    for gi, w in enumerate(POOL_WINDOWS):
        sel = group == gi
        win_sum = jnp.where(sel, sums[w][HIST_ROWS:], win_sum)
        win = jnp.where(sel, w, win)
    cnt = jnp.minimum(pos + 1, win).astype(F32)
    pooled = win_sum / cnt - u
    o_pool = jnp.dot(pooled.astype(BF16), poolw_ref[...],
                     preferred_element_type=F32) * pscale_ref[...]

    mk = mk_ref[si].astype(BF16)
    mv = mv_ref[si].astype(BF16)
    xa_dim = xw // xa_heads
    lane_x = lax.broadcasted_iota(jnp.int32, (tt, xw), 1)
    head_x = lane_x // xa_dim
    qs = q_xa * (xa_dim ** -0.5)
    o_xa = jnp.zeros((tt, xw), F32)
    for hh in range(xa_heads):
        sel = head_x == hh
        qh = jnp.where(sel, qs, 0.0).astype(BF16)
        sc = lax.dot_general(qh, mk, NT_DIMS, preferred_element_type=F32)
        e = jnp.exp(sc - jnp.max(sc, axis=-1, keepdims=True))
        p = e / jnp.sum(e, axis=-1, keepdims=True)
        oh = jnp.dot(p.astype(BF16), mv, preferred_element_type=F32)
        o_xa = jnp.where(sel, oh, o_xa)

    return ((osb_ref[si] * _silu(g_sb)).astype(BF16),
            (o_pool * _silu(g_pool)).astype(BF16),
            (o_xa * _silu(g_xa)).astype(BF16))


def _epilogue(x, o_sb, rest, hist0, mk, mv, pool_bd, pool_scale, w_out_bf16, g_final,
              *, tt, start, sb_width, pool_width, xa_width, xa_heads, apply_final):
    b, t, d = x.shape
    bb = max(1, min(b, EPILOGUE_ROWS // tt))
    while b % bb:
        bb -= 1
    n_rest = rest.shape[-1]
    n_mem = mk.shape[1]
    hist_blocks = tt // HIST_ROWS
    u_col = sb_width // pool_width
    kern = functools.partial(
        _epilogue_kernel, tt=tt, start=start, sb_width=sb_width, pool_width=pool_width,
        xa_width=xa_width, xa_heads=xa_heads, apply_final=apply_final)
    return pl.pallas_call(
        kern,
        grid=(b // bb, t // tt),
        in_specs=[
            pl.BlockSpec((bb, tt, d), lambda bi, i: (bi, i, 0)),
            pl.BlockSpec((bb, tt, sb_width), lambda bi, i: (bi, i, 0)),
            pl.BlockSpec((bb, tt, n_rest), lambda bi, i: (bi, i, 0)),
            pl.BlockSpec((bb, HIST_ROWS, pool_width),
                         lambda bi, i: (bi, jnp.maximum(i * hist_blocks - 1, 0), u_col)),
            pl.BlockSpec((bb, HIST_ROWS, pool_width), lambda bi, i: (bi, 0, 0)),
            pl.BlockSpec((bb, n_mem, xa_width), lambda bi, i: (bi, 0, 0)),
            pl.BlockSpec((bb, n_mem, xa_width), lambda bi, i: (bi, 0, 0)),
            pl.BlockSpec((pool_width, pool_width), lambda bi, i: (0, 0)),
            pl.BlockSpec((1, pool_width), lambda bi, i: (0, 0)),
            pl.BlockSpec((d, d), lambda bi, i: (0, 0)),
            pl.BlockSpec((1, d), lambda bi, i: (0, 0)),
        ],
        out_specs=pl.BlockSpec((bb, tt, d), lambda bi, i: (bi, i, 0)),
        out_shape=jax.ShapeDtypeStruct((b, t, d), F32),
        compiler_params=pltpu.CompilerParams(
            dimension_semantics=("arbitrary", "arbitrary"), vmem_limit_bytes=VMEM_LIMIT),
        name="epilogue",
    )(x, o_sb, rest, rest, hist0, mk, mv, pool_bd, pool_scale, w_out_bf16, g_final)


def _block_diag(pool_w):
    g, c, _ = pool_w.shape
    eye = jnp.eye(g, dtype=pool_w.dtype)
    return (eye[:, None, :, None] * pool_w[:, :, None, :]).reshape(g * c, g * c)


def _pad_hist(hist):
    return jnp.pad(hist, ((0, 0), (HIST_ROWS - hist.shape[1], 0), (0, 0)))


def _mixer_layer(x, start, k_past, v_past, pool_hist, mk, mv, g_norm, w_in_bf16, wkv_t, pool_bd,
                 pool_scale, w_out_bf16, g_final, *, dims, tiles, apply_final):
    sb_width, pool_width, xa_width, xa_heads = dims
    tm, tq, tk, tt = tiles
    b, t, d = x.shape
    heads = sb_width // HEAD_DIM
    k_new, v_new, qkv, rest = _project(x.reshape(b * t, d), g_norm, w_in_bf16, wkv_t,
                                       sb_width, tm, t)
    if k_new.ndim == 3:
        to_cache = lambda a: jnp.transpose(a.reshape(b, heads, HEAD_DIM, t), (0, 3, 1, 2))
    else:
        to_cache = lambda a: a.reshape(b, t, heads, HEAD_DIM)
    qkv = qkv.reshape(b, t, -1)
    rest = rest.reshape(b, t, -1)
    if k_past is None:
        o_sb = _stick_breaking_self(qkv, tq=tq, sb_width=sb_width,
                                    pairs_per_step=PAIRS_PER_STEP)
    else:
        o_sb = _stick_breaking_cached(qkv, k_past, v_past, tk=tk)
    y = _epilogue(x, o_sb, rest, _pad_hist(pool_hist), mk, mv, pool_bd, pool_scale,
                  w_out_bf16, g_final, tt=tt, start=start, sb_width=sb_width,
                  pool_width=pool_width, xa_width=xa_width, xa_heads=xa_heads,
                  apply_final=apply_final)
    u_pool = rest[:, :, sb_width:sb_width + pool_width]
    new_hist = jnp.concatenate([pool_hist, u_pool], axis=1)[:, -POOL_STATE:]
    return y, to_cache(k_new), to_cache(v_new), new_hist


def kernel(x_prompt, x_sample, cache_sb_k, cache_sb_v, state_pool, cache_mem_k, cache_mem_v,
           mem_prompt, g_norm, w_in, pool_w, pool_scale, g_mem, w_mem_kv, w_out, g_final):
    depth = g_norm.shape[0]
    bp, tp, d = x_prompt.shape
    bs, ts, _ = x_sample.shape
    past = cache_sb_k.shape[2]
    sb_heads, head_dim = cache_sb_k.shape[3], cache_sb_k.shape[4]
    assert head_dim == HEAD_DIM
    sb_width = sb_heads * head_dim
    pool_width = pool_w.shape[1] * pool_w.shape[2]
    n_mem, xa_heads, xa_dim = cache_mem_k.shape[2:]
    xa_width = xa_heads * xa_dim
    dims = (sb_width, pool_width, xa_width, xa_heads)
    g_fin = g_final.reshape(1, d)

    yp, ys = x_prompt, x_sample
    kp_l, vp_l, hp_l, mkp_l, mvp_l, ks_l, vs_l, hs_l = [], [], [], [], [], [], [], []
    for l in range(depth):
        final = l == depth - 1
        g_l = g_norm[l].reshape(1, d)
        w_in_b = w_in[l].astype(BF16)
        w_out_b = w_out[l].astype(BF16)
        pool_bd = _block_diag(pool_w[l]).astype(BF16)
        p_scale = pool_scale[l].reshape(1, pool_width)
        mk, mv = _memory_kv(mem_prompt.reshape(bp * n_mem, d), g_mem[l].reshape(1, d),
                            w_mem_kv[l].astype(BF16), tm=256)
        mk = mk.reshape(bp, n_mem, xa_width)
        mv = mv.reshape(bp, n_mem, xa_width)
        hist0 = jnp.zeros((bp, POOL_STATE, pool_width), x_prompt.dtype)
        wkv_t = jnp.transpose(w_in[l][:, sb_width:3 * sb_width]).astype(BF16)
        yp, kp, vp, hp = _mixer_layer(
            yp, 0, None, None, hist0, mk, mv, g_l, w_in_b, wkv_t, pool_bd, p_scale, w_out_b,
            g_fin, dims=dims, tiles=(512, 256, 256, 1024), apply_final=final)
        ys, kn, vn, hn = _mixer_layer(
            ys, past, jnp.transpose(cache_sb_k[l], (0, 2, 3, 1)),
            jnp.transpose(cache_sb_v[l], (0, 2, 3, 1)), state_pool[l],
            cache_mem_k[l].reshape(bs, n_mem, xa_width),
            cache_mem_v[l].reshape(bs, n_mem, xa_width),
            g_l, w_in_b, wkv_t, pool_bd, p_scale, w_out_b, g_fin,
            dims=dims, tiles=(512, ts, 256, ts), apply_final=final)
        kp_l.append(kp)
        vp_l.append(vp)
        hp_l.append(hp)
        mkp_l.append(mk.reshape(bp, n_mem, xa_heads, xa_dim))
        mvp_l.append(mv.reshape(bp, n_mem, xa_heads, xa_dim))
        ks_l.append(kn)
        vs_l.append(vn)
        hs_l.append(hn)
    stack = lambda xs: jnp.stack(xs, axis=0)
    return (yp, ys, stack(kp_l), stack(vp_l), stack(hp_l), stack(mkp_l), stack(mvp_l),
            stack(ks_l), stack(vs_l), stack(hs_l))
```

```python
import functools

import jax
import jax.numpy as jnp
from jax import lax
from jax.experimental import pallas as pl
from jax.experimental.pallas import tpu as pltpu

F32 = jnp.float32
BF16 = jnp.bfloat16

HEAD_DIM = 64
LANES = 128
POOL_WINDOWS = (2, 4, 8, 16)
POOL_STATE = max(POOL_WINDOWS) - 1
HIST_ROWS = 16
EPS = 1e-6
DEAD_CARRY = 104.0
VMEM_LIMIT = 56 * 1024 * 1024
NT_DIMS = (((1,), (1,)), ((), ()))
PAIRS_PER_STEP = 2
EPILOGUE_ROWS = 256


def _rms(x, g):
    ms = jnp.mean(x * x, axis=-1, keepdims=True)
    return (x * lax.rsqrt(ms + EPS)) * g


def _proj_kernel(x_ref, g_ref, w_ref, wkv_t_ref, k_ref, v_ref, qkv_ref, rest_ref,
                 *, sb_width, time_minor):
    h = _rms(x_ref[...], g_ref[...]).astype(BF16)
    s = sb_width
    q = jnp.dot(h, w_ref[:, 0:s], preferred_element_type=F32)
    qkv_ref[:, 0:s] = (q * (HEAD_DIM ** -0.5)).astype(BF16)
    if time_minor:
        kt = lax.dot_general(wkv_t_ref[0:s, :], h, NT_DIMS, preferred_element_type=F32)
        vt = lax.dot_general(wkv_t_ref[s:2 * s, :], h, NT_DIMS, preferred_element_type=F32)
        k_ref[0] = kt
        v_ref[0] = vt
        qkv_ref[:, s:2 * s] = kt.T.astype(BF16)
        qkv_ref[:, 2 * s:3 * s] = vt.T.astype(BF16)
    else:
        k = jnp.dot(h, w_ref[:, s:2 * s], preferred_element_type=F32)
        v = jnp.dot(h, w_ref[:, 2 * s:3 * s], preferred_element_type=F32)
        k_ref[...] = k
        v_ref[...] = v
        qkv_ref[:, s:2 * s] = k.astype(BF16)
        qkv_ref[:, 2 * s:3 * s] = v.astype(BF16)
    rest_ref[...] = jnp.dot(h, w_ref[:, 3 * s:], preferred_element_type=F32)


def _project(x2d, g, w_bf16, wkv_t, sb_width, tm, rows_per_stream):
    m, d = x2d.shape
    n = w_bf16.shape[1]
    n_rest = n - 3 * sb_width
    time_minor = rows_per_stream % tm == 0
    if time_minor:
        nt = rows_per_stream // tm
        kv_spec = pl.BlockSpec((1, sb_width, tm), lambda i: (i // nt, 0, i % nt))
        kv_shape = jax.ShapeDtypeStruct((m // rows_per_stream, sb_width, rows_per_stream), F32)
    else:
        kv_spec = pl.BlockSpec((tm, sb_width), lambda i: (i, 0))
        kv_shape = jax.ShapeDtypeStruct((m, sb_width), F32)
    return pl.pallas_call(
        functools.partial(_proj_kernel, sb_width=sb_width, time_minor=time_minor),
        grid=(m // tm,),
        in_specs=[
            pl.BlockSpec((tm, d), lambda i: (i, 0)),
            pl.BlockSpec((1, d), lambda i: (0, 0)),
            pl.BlockSpec((d, n), lambda i: (0, 0)),
            pl.BlockSpec((2 * sb_width, d), lambda i: (0, 0)),
        ],
        out_specs=[
            kv_spec,
            kv_spec,
            pl.BlockSpec((tm, 3 * sb_width), lambda i: (i, 0)),
            pl.BlockSpec((tm, n_rest), lambda i: (i, 0)),
        ],
        out_shape=[
            kv_shape,
            kv_shape,
            jax.ShapeDtypeStruct((m, 3 * sb_width), BF16),
            jax.ShapeDtypeStruct((m, n_rest), F32),
        ],
        compiler_params=pltpu.CompilerParams(
            dimension_semantics=("arbitrary",), vmem_limit_bytes=VMEM_LIMIT),
        name="proj",
    )(x2d, g, w_bf16, wkv_t)


def _memkv_kernel(x_ref, g_ref, w_ref, mk_ref, mv_ref, *, xa_width):
    h = _rms(x_ref[...], g_ref[...]).astype(BF16)
    kv = jnp.dot(h, w_ref[...], preferred_element_type=F32)
    mk_ref[...] = kv[:, :xa_width]
    mv_ref[...] = kv[:, xa_width:]


def _memory_kv(mem2d, g, w_bf16, tm):
    m, d = mem2d.shape
    xa_width = w_bf16.shape[1] // 2
    return pl.pallas_call(
        functools.partial(_memkv_kernel, xa_width=xa_width),
        grid=(m // tm,),
        in_specs=[
            pl.BlockSpec((tm, d), lambda i: (i, 0)),
            pl.BlockSpec((1, d), lambda i: (0, 0)),
            pl.BlockSpec((d, 2 * xa_width), lambda i: (0, 0)),
        ],
        out_specs=[
            pl.BlockSpec((tm, xa_width), lambda i: (i, 0)),
            pl.BlockSpec((tm, xa_width), lambda i: (i, 0)),
        ],
        out_shape=[
            jax.ShapeDtypeStruct((m, xa_width), F32),
            jax.ShapeDtypeStruct((m, xa_width), F32),
        ],
        compiler_params=pltpu.CompilerParams(
            dimension_semantics=("arbitrary",), vmem_limit_bytes=VMEM_LIMIT),
        name="memkv",
    )(mem2d, g, w_bf16)


def _strict_upper(n):
    r = lax.broadcasted_iota(jnp.int32, (n, n), 0)
    c = lax.broadcasted_iota(jnp.int32, (n, n), 1)
    return jnp.where(r > c, 1.0, 0.0).astype(BF16)


def _softplus(z):
    return jnp.maximum(z, 0.0) + jnp.log(1.0 + jnp.exp(-jnp.abs(z)))


def _sum_later(sp, upper):
    hi = sp.astype(BF16)
    lo = (sp - hi.astype(F32)).astype(BF16)
    return jnp.dot(jnp.concatenate([hi, lo], axis=1), jnp.concatenate([upper, upper], axis=0),
                   preferred_element_type=F32)


def _sb_block(qh, kblk, vblk, upper, carry, mask, transposed=False):
    if transposed:
        z = jnp.dot(qh, kblk, preferred_element_type=F32)
    else:
        z = lax.dot_general(qh, kblk, NT_DIMS, preferred_element_type=F32)
    sp = _softplus(z)
    if mask is not None:
        sp = jnp.where(mask, sp, 0.0)
    later = _sum_later(sp, upper)
    w = jnp.exp((z - sp) - later - carry)
    if mask is not None:
        w = jnp.where(mask, w, 0.0)
    if transposed:
        out = lax.dot_general(w.astype(BF16), vblk, NT_DIMS, preferred_element_type=F32)
    else:
        out = jnp.dot(w.astype(BF16), vblk, preferred_element_type=F32)
    return out, carry + jnp.sum(sp, axis=-1, keepdims=True)


def _any_live(carries):
    lowest = functools.reduce(jnp.minimum, carries)
    return (jnp.min(lowest) <= DEAD_CARRY).astype(jnp.int32)


def _walk_older(n_blocks, carries, block_fn):
    def test(state):
        return jnp.logical_and(state[0] < n_blocks, state[1] > 0)

    def body(state):
        new = block_fn(state[0], state[2:])
        return (state[0] + 1, _any_live(new), *new)

    lax.while_loop(test, body, (jnp.int32(0), _any_live(carries), *carries))


def _sb_window_kernel(q_ref, k_ref, v_ref, o_ref, acc_ref, *, tq, pairs):
    i = pl.program_id(2)
    lane = lax.broadcasted_iota(jnp.int32, (tq, LANES), 1)
    first = lane < HEAD_DIM
    qs, cols = [], []
    for p in range(pairs):
        col = slice(p * LANES, (p + 1) * LANES)
        q = q_ref[0, :, col]
        zero = jnp.zeros_like(q)
        qs += [jnp.where(first, q, zero), jnp.where(first, zero, q)]
        cols += [col, col]
    n = len(qs)
    r = lax.broadcasted_iota(jnp.int32, (tq, tq), 0)
    c = lax.broadcasted_iota(jnp.int32, (tq, tq), 1)
    causal = c < r
    upper = _strict_upper(tq)

    @pl.when(i == 0)
    def _():
        zc = jnp.zeros((tq, 1), F32)
        for h in range(n):
            o, _ = _sb_block(qs[h], k_ref[0, 0:tq, cols[h]], v_ref[0, 0:tq, cols[h]],
                             upper, zc, causal)
            acc_ref[h] = o

    @pl.when(i > 0)
    def _():
        ws = pl.multiple_of((i - 1) * tq, tq)
        zs = [lax.dot_general(qs[h], k_ref[0, pl.ds(ws, 2 * tq), cols[h]], NT_DIMS,
                              preferred_element_type=F32) for h in range(n)]
        zp = [z[:, :tq] for z in zs]
        zd = [z[:, tq:] for z in zs]
        sp_p = [_softplus(z) for z in zp]
        sp_d = [jnp.where(causal, _softplus(z), 0.0) for z in zd]
        later = _sum_later(jnp.concatenate(sp_p + sp_d, axis=0), upper)
        carries = []
        for h in range(n):
            rs_d = jnp.sum(sp_d[h], axis=-1, keepdims=True)
            later_p = later[h * tq:(h + 1) * tq]
            later_d = later[(n + h) * tq:(n + h + 1) * tq]
            w_p = jnp.exp((zp[h] - sp_p[h]) - later_p - rs_d)
            w_d = jnp.where(causal, jnp.exp((zd[h] - sp_d[h]) - later_d), 0.0)
            w = jnp.concatenate([w_p.astype(BF16), w_d.astype(BF16)], axis=1)
            acc_ref[h] = jnp.dot(w, v_ref[0, pl.ds(ws, 2 * tq), cols[h]],
                                 preferred_element_type=F32)
            carries.append(rs_d + jnp.sum(sp_p[h], axis=-1, keepdims=True))

        n_older = i - 1

        def older_block(j, cs):
            start = pl.multiple_of((n_older - 1 - j) * tq, tq)
            new = []
            for h in range(n):
                o, ch = _sb_block(qs[h], k_ref[0, pl.ds(start, tq), cols[h]],
                                  v_ref[0, pl.ds(start, tq), cols[h]], upper, cs[h], None)
                acc_ref[h] += o
                new.append(ch)
            return new

        _walk_older(n_older, carries, older_block)

    for p in range(pairs):
        o_ref[0, :, p * LANES:(p + 1) * LANES] = jnp.where(
            first, acc_ref[2 * p], acc_ref[2 * p + 1])


def _stick_breaking_self(qkv, *, tq, sb_width, pairs_per_step):
    b, t, _ = qkv.shape
    width = pairs_per_step * LANES
    groups = sb_width // width
    return pl.pallas_call(
        functools.partial(_sb_window_kernel, tq=tq, pairs=pairs_per_step),
        grid=(b, groups, t // tq),
        in_specs=[
            pl.BlockSpec((1, tq, width), lambda bi, g, i: (bi, i, g)),
            pl.BlockSpec((1, t, width), lambda bi, g, i: (bi, 0, groups + g)),
            pl.BlockSpec((1, t, width), lambda bi, g, i: (bi, 0, 2 * groups + g)),
        ],
        out_specs=pl.BlockSpec((1, tq, width), lambda bi, g, i: (bi, i, g)),
        out_shape=jax.ShapeDtypeStruct((b, t, sb_width), F32),
        scratch_shapes=[pltpu.VMEM((2 * pairs_per_step, tq, LANES), F32)],
        compiler_params=pltpu.CompilerParams(
            dimension_semantics=("arbitrary", "arbitrary", "arbitrary"),
            vmem_limit_bytes=VMEM_LIMIT),
        name="stick_breaking_self",
    )(qkv, qkv, qkv)


def _sb_cached_kernel(qkv_ref, kwin_ref, vwin_ref, kc_ref, vc_ref, o_ref,
                      kbuf, vbuf, acc_ref, sem, *, heads, tk, n_older):
    b = pl.program_id(0)
    tq = qkv_ref.shape[1]
    s = heads * HEAD_DIM
    r = lax.broadcasted_iota(jnp.int32, (tq, tq), 0)
    c = lax.broadcasted_iota(jnp.int32, (tq, tq), 1)
    causal = c < r
    upper_d = _strict_upper(tq)
    upper = _strict_upper(tk)
    q = qkv_ref[0, :, 0:s]
    kn = qkv_ref[0, :, s:2 * s]
    vn = qkv_ref[0, :, 2 * s:3 * s]
    head = lambda x, h: x[:, h * HEAD_DIM:(h + 1) * HEAD_DIM]
    qs = [head(q, h) for h in range(heads)]

    zd = [lax.dot_general(qs[h], head(kn, h), NT_DIMS, preferred_element_type=F32)
          for h in range(heads)]
    zw = [jnp.dot(qs[h], kwin_ref[0, h].astype(BF16), preferred_element_type=F32)
          for h in range(heads)]
    sp_d = [jnp.where(causal, _softplus(z), 0.0) for z in zd]
    sp_w = [_softplus(z) for z in zw]
    later_d = _sum_later(jnp.concatenate(sp_d, axis=0), upper_d)
    later_w = _sum_later(jnp.concatenate(sp_w, axis=0), upper)
    carries = []
    for h in range(heads):
        rows = slice(h * tq, (h + 1) * tq)
        rs_d = jnp.sum(sp_d[h], axis=-1, keepdims=True)
        w_d = jnp.where(causal, jnp.exp((zd[h] - sp_d[h]) - later_d[rows]), 0.0)
        w_w = jnp.exp((zw[h] - sp_w[h]) - later_w[rows] - rs_d)
        acc_ref[h] = (jnp.dot(w_d.astype(BF16), head(vn, h), preferred_element_type=F32)
                      + lax.dot_general(w_w.astype(BF16), vwin_ref[0, h].astype(BF16), NT_DIMS,
                                        preferred_element_type=F32))
        carries.append(rs_d + jnp.sum(sp_w[h], axis=-1, keepdims=True))

    def block_copies(j):
        start = pl.multiple_of((n_older - 1 - j) * tk, tk)
        return (pltpu.make_async_copy(kc_ref.at[b, :, :, pl.ds(start, tk)], kbuf, sem.at[0]),
                pltpu.make_async_copy(vc_ref.at[b, :, :, pl.ds(start, tk)], vbuf, sem.at[1]))

    def older_block(j, cs):
        copies = block_copies(j)
        for cp in copies:
            cp.start()
        for cp in copies:
            cp.wait()
        new = []
        for h in range(heads):
            o, ch = _sb_block(qs[h], kbuf[h].astype(BF16), vbuf[h].astype(BF16),
                              upper, cs[h], None, transposed=True)
            acc_ref[h] += o
            new.append(ch)
        return new

    _walk_older(n_older, carries, older_block)
    o_ref[0] = jnp.concatenate([acc_ref[h] for h in range(heads)], axis=1)


def _stick_breaking_cached(qkv, cache_k, cache_v, *, tk):
    b, t, _ = qkv.shape
    _, heads, dh, p = cache_k.shape
    s = heads * dh
    last = p // tk - 1
    win = pl.BlockSpec((1, heads, dh, tk), lambda bi: (bi, 0, 0, last))
    return pl.pallas_call(
        functools.partial(_sb_cached_kernel, heads=heads, tk=tk, n_older=last),
        grid=(b,),
        in_specs=[
            pl.BlockSpec((1, t, 3 * s), lambda bi: (bi, 0, 0)),
            win, win,
            pl.BlockSpec(memory_space=pl.ANY),
            pl.BlockSpec(memory_space=pl.ANY),
        ],
        out_specs=pl.BlockSpec((1, t, s), lambda bi: (bi, 0, 0)),
        out_shape=jax.ShapeDtypeStruct((b, t, s), F32),
        scratch_shapes=[
            pltpu.VMEM((heads, dh, tk), F32),
            pltpu.VMEM((heads, dh, tk), F32),
            pltpu.VMEM((heads, t, dh), F32),
            pltpu.SemaphoreType.DMA((2,)),
        ],
        compiler_params=pltpu.CompilerParams(
            dimension_semantics=("arbitrary",), vmem_limit_bytes=VMEM_LIMIT),
        name="stick_breaking_cached",
    )(qkv, cache_k, cache_v, cache_k, cache_v)


def _silu(g):
    return g / (1.0 + jnp.exp(-g))


def _epilogue_kernel(x_ref, osb_ref, rest_ref, prev_ref, hist0_ref, mk_ref, mv_ref,
                     poolw_ref, pscale_ref, wout_ref, gfin_ref, y_ref,
                     *, tt, start, sb_width, pool_width, xa_width, xa_heads, apply_final):
    streams = x_ref.shape[0]
    mixed = [_mixed_rows(si, rest_ref, osb_ref, prev_ref, hist0_ref, mk_ref, mv_ref, poolw_ref,
                         pscale_ref, tt=tt, start=start, sb_width=sb_width,
                         pool_width=pool_width, xa_width=xa_width, xa_heads=xa_heads)
             for si in range(streams)]
    m_sb, m_pool, m_xa = (jnp.concatenate([m[k] for m in mixed], axis=0) for k in range(3))
    s, pw, xw = sb_width, pool_width, xa_width
    proj = (jnp.dot(m_sb, wout_ref[0:s, :], preferred_element_type=F32)
            + jnp.dot(m_pool, wout_ref[s:s + pw, :], preferred_element_type=F32)
            + jnp.dot(m_xa, wout_ref[s + pw:s + pw + xw, :], preferred_element_type=F32))
    for si in range(streams):
        y = x_ref[si] + proj[si * tt:(si + 1) * tt]
        if apply_final:
            y = _rms(y, gfin_ref[...])
        y_ref[si] = y


def _mixed_rows(si, rest_ref, osb_ref, prev_ref, hist0_ref, mk_ref, mv_ref, poolw_ref, pscale_ref,
                *, tt, start, sb_width, pool_width, xa_width, xa_heads):
    i = pl.program_id(1)
    s, pw, xw = sb_width, pool_width, xa_width
    g_sb = rest_ref[si, :, 0:s]
    u = rest_ref[si, :, s:s + pw]
    g_pool = rest_ref[si, :, s + pw:s + 2 * pw]
    q_xa = rest_ref[si, :, s + 2 * pw:s + 2 * pw + xw]
    g_xa = rest_ref[si, :, s + 2 * pw + xw:s + 2 * pw + 2 * xw]

    hist = jnp.where(i == 0, hist0_ref[si], prev_ref[si])
    ext = jnp.concatenate([hist, u], axis=0)
    sums = {1: ext}
    width = 1
    while width < max(POOL_WINDOWS):
        prev = sums[width]
        sums[2 * width] = prev + pltpu.roll(prev, width, 0)
        width *= 2
    lane = lax.broadcasted_iota(jnp.int32, (tt, pw), 1)
    group = lane // (pw // len(POOL_WINDOWS))
    pos = start + i * tt + lax.broadcasted_iota(jnp.int32, (tt, pw), 0)
    win_sum = jnp.zeros((tt, pw), F32)
    win = jnp.zeros((tt, pw), jnp.int32)
    for gi, w in enumerate(POOL_WINDOWS):
        sel = group == gi
        win_sum = jnp.where(sel, sums[w][HIST_ROWS:], win_sum)
        win = jnp.where(sel, w, win)
    cnt = jnp.minimum(pos + 1, win).astype(F32)
    pooled = win_sum / cnt - u
    o_pool = jnp.dot(pooled.astype(BF16), poolw_ref[...],
                     preferred_element_type=F32) * pscale_ref[...]

    mk = mk_ref[si].astype(BF16)
    mv = mv_ref[si].astype(BF16)
    xa_dim = xw // xa_heads
    lane_x = lax.broadcasted_iota(jnp.int32, (tt, xw), 1)
    head_x = lane_x // xa_dim
    qs = q_xa * (xa_dim ** -0.5)
    o_xa = jnp.zeros((tt, xw), F32)
    for hh in range(xa_heads):
        sel = head_x == hh
        qh = jnp.where(sel, qs, 0.0).astype(BF16)
        sc = lax.dot_general(qh, mk, NT_DIMS, preferred_element_type=F32)
        e = jnp.exp(sc - jnp.max(sc, axis=-1, keepdims=True))
        p = e / jnp.sum(e, axis=-1, keepdims=True)
        oh = jnp.dot(p.astype(BF16), mv, preferred_element_type=F32)
        o_xa = jnp.where(sel, oh, o_xa)

    return ((osb_ref[si] * _silu(g_sb)).astype(BF16),
            (o_pool * _silu(g_pool)).astype(BF16),
            (o_xa * _silu(g_xa)).astype(BF16))


def _epilogue(x, o_sb, rest, hist0, mk, mv, pool_bd, pool_scale, w_out_bf16, g_final,
              *, tt, start, sb_width, pool_width, xa_width, xa_heads, apply_final):
    b, t, d = x.shape
    bb = max(1, min(b, EPILOGUE_ROWS // tt))
    while b % bb:
        bb -= 1
    n_rest = rest.shape[-1]
    n_mem = mk.shape[1]
    hist_blocks = tt // HIST_ROWS
    u_col = sb_width // pool_width
    kern = functools.partial(
        _epilogue_kernel, tt=tt, start=start, sb_width=sb_width, pool_width=pool_width,
        xa_width=xa_width, xa_heads=xa_heads, apply_final=apply_final)
    return pl.pallas_call(
        kern,
        grid=(b // bb, t // tt),
        in_specs=[
            pl.BlockSpec((bb, tt, d), lambda bi, i: (bi, i, 0)),
            pl.BlockSpec((bb, tt, sb_width), lambda bi, i: (bi, i, 0)),
            pl.BlockSpec((bb, tt, n_rest), lambda bi, i: (bi, i, 0)),
            pl.BlockSpec((bb, HIST_ROWS, pool_width),
                         lambda bi, i: (bi, jnp.maximum(i * hist_blocks - 1, 0), u_col)),
            pl.BlockSpec((bb, HIST_ROWS, pool_width), lambda bi, i: (bi, 0, 0)),
            pl.BlockSpec((bb, n_mem, xa_width), lambda bi, i: (bi, 0, 0)),
            pl.BlockSpec((bb, n_mem, xa_width), lambda bi, i: (bi, 0, 0)),
            pl.BlockSpec((pool_width, pool_width), lambda bi, i: (0, 0)),
            pl.BlockSpec((1, pool_width), lambda bi, i: (0, 0)),
            pl.BlockSpec((d, d), lambda bi, i: (0, 0)),
            pl.BlockSpec((1, d), lambda bi, i: (0, 0)),
        ],
        out_specs=pl.BlockSpec((bb, tt, d), lambda bi, i: (bi, i, 0)),
        out_shape=jax.ShapeDtypeStruct((b, t, d), F32),
        compiler_params=pltpu.CompilerParams(
            dimension_semantics=("arbitrary", "arbitrary"), vmem_limit_bytes=VMEM_LIMIT),
        name="epilogue",
    )(x, o_sb, rest, rest, hist0, mk, mv, pool_bd, pool_scale, w_out_bf16, g_final)


def _block_diag(pool_w):
    g, c, _ = pool_w.shape
    eye = jnp.eye(g, dtype=pool_w.dtype)
    return (eye[:, None, :, None] * pool_w[:, :, None, :]).reshape(g * c, g * c)


def _pad_hist(hist):
    return jnp.pad(hist, ((0, 0), (HIST_ROWS - hist.shape[1], 0), (0, 0)))


def _mixer_layer(x, start, k_past, v_past, pool_hist, mk, mv, g_norm, w_in_bf16, wkv_t, pool_bd,
                 pool_scale, w_out_bf16, g_final, *, dims, tiles, apply_final):
    sb_width, pool_width, xa_width, xa_heads = dims
    tm, tq, tk, tt = tiles
    b, t, d = x.shape
    heads = sb_width // HEAD_DIM
    k_new, v_new, qkv, rest = _project(x.reshape(b * t, d), g_norm, w_in_bf16, wkv_t,
                                       sb_width, tm, t)
    if k_new.ndim == 3:
        to_cache = lambda a: jnp.transpose(a.reshape(b, heads, HEAD_DIM, t), (0, 3, 1, 2))
    else:
        to_cache = lambda a: a.reshape(b, t, heads, HEAD_DIM)
    qkv = qkv.reshape(b, t, -1)
    rest = rest.reshape(b, t, -1)
    if k_past is None:
        o_sb = _stick_breaking_self(qkv, tq=tq, sb_width=sb_width,
                                    pairs_per_step=PAIRS_PER_STEP)
    else:
        o_sb = _stick_breaking_cached(qkv, k_past, v_past, tk=tk)
    y = _epilogue(x, o_sb, rest, _pad_hist(pool_hist), mk, mv, pool_bd, pool_scale,
                  w_out_bf16, g_final, tt=tt, start=start, sb_width=sb_width,
                  pool_width=pool_width, xa_width=xa_width, xa_heads=xa_heads,
                  apply_final=apply_final)
    u_pool = rest[:, :, sb_width:sb_width + pool_width]
    new_hist = jnp.concatenate([pool_hist, u_pool], axis=1)[:, -POOL_STATE:]
    return y, to_cache(k_new), to_cache(v_new), new_hist


def kernel(x_prompt, x_sample, cache_sb_k, cache_sb_v, state_pool, cache_mem_k, cache_mem_v,
           mem_prompt, g_norm, w_in, pool_w, pool_scale, g_mem, w_mem_kv, w_out, g_final):
    depth = g_norm.shape[0]
    bp, tp, d = x_prompt.shape
    bs, ts, _ = x_sample.shape
    past = cache_sb_k.shape[2]
    sb_heads, head_dim = cache_sb_k.shape[3], cache_sb_k.shape[4]
    assert head_dim == HEAD_DIM
    sb_width = sb_heads * head_dim
    pool_width = pool_w.shape[1] * pool_w.shape[2]
    n_mem, xa_heads, xa_dim = cache_mem_k.shape[2:]
    xa_width = xa_heads * xa_dim
    dims = (sb_width, pool_width, xa_width, xa_heads)
    g_fin = g_final.reshape(1, d)

    yp, ys = x_prompt, x_sample
    kp_l, vp_l, hp_l, mkp_l, mvp_l, ks_l, vs_l, hs_l = [], [], [], [], [], [], [], []
    for l in range(depth):
        final = l == depth - 1
        g_l = g_norm[l].reshape(1, d)
        w_in_b = w_in[l].astype(BF16)
        w_out_b = w_out[l].astype(BF16)
        pool_bd = _block_diag(pool_w[l]).astype(BF16)
        p_scale = pool_scale[l].reshape(1, pool_width)
        mk, mv = _memory_kv(mem_prompt.reshape(bp * n_mem, d), g_mem[l].reshape(1, d),
                            w_mem_kv[l].astype(BF16), tm=256)
        mk = mk.reshape(bp, n_mem, xa_width)
        mv = mv.reshape(bp, n_mem, xa_width)
        hist0 = jnp.zeros((bp, POOL_STATE, pool_width), x_prompt.dtype)
        wkv_t = jnp.transpose(w_in[l][:, sb_width:3 * sb_width]).astype(BF16)
        yp, kp, vp, hp = _mixer_layer(
            yp, 0, None, None, hist0, mk, mv, g_l, w_in_b, wkv_t, pool_bd, p_scale, w_out_b,
            g_fin, dims=dims, tiles=(512, 256, 256, 1024), apply_final=final)
        ys, kn, vn, hn = _mixer_layer(
            ys, past, jnp.transpose(cache_sb_k[l], (0, 2, 3, 1)),
            jnp.transpose(cache_sb_v[l], (0, 2, 3, 1)), state_pool[l],
            cache_mem_k[l].reshape(bs, n_mem, xa_width),
            cache_mem_v[l].reshape(bs, n_mem, xa_width),
            g_l, w_in_b, wkv_t, pool_bd, p_scale, w_out_b, g_fin,
            dims=dims, tiles=(512, ts, 256, ts), apply_final=final)
        kp_l.append(kp)
        vp_l.append(vp)
        hp_l.append(hp)
        mkp_l.append(mk.reshape(bp, n_mem, xa_heads, xa_dim))
        mvp_l.append(mv.reshape(bp, n_mem, xa_heads, xa_dim))
        ks_l.append(kn)
        vs_l.append(vn)
        hs_l.append(hn)
    stack = lambda xs: jnp.stack(xs, axis=0)
    return (yp, ys, stack(kp_l), stack(vp_l), stack(hp_l), stack(mkp_l), stack(mvp_l),
            stack(ks_l), stack(vs_l), stack(hs_l))
```

```python
import functools

import jax
import jax.numpy as jnp
from jax import lax
from jax.experimental import pallas as pl
from jax.experimental.pallas import tpu as pltpu

F32 = jnp.float32
BF16 = jnp.bfloat16

HEAD_DIM = 64
LANES = 128
POOL_WINDOWS = (2, 4, 8, 16)
POOL_STATE = max(POOL_WINDOWS) - 1
HIST_ROWS = 16
EPS = 1e-6
DEAD_CARRY = 104.0
VMEM_LIMIT = 56 * 1024 * 1024
NT_DIMS = (((1,), (1,)), ((), ()))
PAIRS_PER_STEP = 2
EPILOGUE_ROWS = 256


def _rms(x, g):
    ms = jnp.mean(x * x, axis=-1, keepdims=True)
    return (x * lax.rsqrt(ms + EPS)) * g


def _proj_kernel(x_ref, g_ref, w_ref, wkv_t_ref, k_ref, v_ref, qkv_ref, rest_ref,
                 *, sb_width, time_minor):
    h = _rms(x_ref[...], g_ref[...]).astype(BF16)
    s = sb_width
    q = jnp.dot(h, w_ref[:, 0:s], preferred_element_type=F32)
    qkv_ref[:, 0:s] = (q * (HEAD_DIM ** -0.5)).astype(BF16)
    if time_minor:
        kt = lax.dot_general(wkv_t_ref[0:s, :], h, NT_DIMS, preferred_element_type=F32)
        vt = lax.dot_general(wkv_t_ref[s:2 * s, :], h, NT_DIMS, preferred_element_type=F32)
        k_ref[0] = kt
        v_ref[0] = vt
        qkv_ref[:, s:2 * s] = kt.T.astype(BF16)
        qkv_ref[:, 2 * s:3 * s] = vt.T.astype(BF16)
    else:
        k = jnp.dot(h, w_ref[:, s:2 * s], preferred_element_type=F32)
        v = jnp.dot(h, w_ref[:, 2 * s:3 * s], preferred_element_type=F32)
        k_ref[...] = k
        v_ref[...] = v
        qkv_ref[:, s:2 * s] = k.astype(BF16)
        qkv_ref[:, 2 * s:3 * s] = v.astype(BF16)
    rest_ref[...] = jnp.dot(h, w_ref[:, 3 * s:], preferred_element_type=F32)


def _project(x2d, g, w_bf16, wkv_t, sb_width, tm, rows_per_stream):
    m, d = x2d.shape
    n = w_bf16.shape[1]
    n_rest = n - 3 * sb_width
    time_minor = rows_per_stream % tm == 0
    if time_minor:
        nt = rows_per_stream // tm
        kv_spec = pl.BlockSpec((1, sb_width, tm), lambda i: (i // nt, 0, i % nt))
        kv_shape = jax.ShapeDtypeStruct((m // rows_per_stream, sb_width, rows_per_stream), F32)
    else:
        kv_spec = pl.BlockSpec((tm, sb_width), lambda i: (i, 0))
        kv_shape = jax.ShapeDtypeStruct((m, sb_width), F32)
    return pl.pallas_call(
        functools.partial(_proj_kernel, sb_width=sb_width, time_minor=time_minor),
        grid=(m // tm,),
        in_specs=[
            pl.BlockSpec((tm, d), lambda i: (i, 0)),
            pl.BlockSpec((1, d), lambda i: (0, 0)),
            pl.BlockSpec((d, n), lambda i: (0, 0)),
            pl.BlockSpec((2 * sb_width, d), lambda i: (0, 0)),
        ],
        out_specs=[
            kv_spec,
            kv_spec,
            pl.BlockSpec((tm, 3 * sb_width), lambda i: (i, 0)),
            pl.BlockSpec((tm, n_rest), lambda i: (i, 0)),
        ],
        out_shape=[
            kv_shape,
            kv_shape,
            jax.ShapeDtypeStruct((m, 3 * sb_width), BF16),
            jax.ShapeDtypeStruct((m, n_rest), F32),
        ],
        compiler_params=pltpu.CompilerParams(
            dimension_semantics=("arbitrary",), vmem_limit_bytes=VMEM_LIMIT),
        name="proj",
    )(x2d, g, w_bf16, wkv_t)


def _memkv_kernel(x_ref, g_ref, w_ref, mk_ref, mv_ref, *, xa_width):
    h = _rms(x_ref[...], g_ref[...]).astype(BF16)
    kv = jnp.dot(h, w_ref[...], preferred_element_type=F32)
    mk_ref[...] = kv[:, :xa_width]
    mv_ref[...] = kv[:, xa_width:]


def _memory_kv(mem2d, g, w_bf16, tm):
    m, d = mem2d.shape
    xa_width = w_bf16.shape[1] // 2
    return pl.pallas_call(
        functools.partial(_memkv_kernel, xa_width=xa_width),
        grid=(m // tm,),
        in_specs=[
            pl.BlockSpec((tm, d), lambda i: (i, 0)),
            pl.BlockSpec((1, d), lambda i: (0, 0)),
            pl.BlockSpec((d, 2 * xa_width), lambda i: (0, 0)),
        ],
        out_specs=[
            pl.BlockSpec((tm, xa_width), lambda i: (i, 0)),
            pl.BlockSpec((tm, xa_width), lambda i: (i, 0)),
        ],
        out_shape=[
            jax.ShapeDtypeStruct((m, xa_width), F32),
            jax.ShapeDtypeStruct((m, xa_width), F32),
        ],
        compiler_params=pltpu.CompilerParams(
            dimension_semantics=("arbitrary",), vmem_limit_bytes=VMEM_LIMIT),
        name="memkv",
    )(mem2d, g, w_bf16)


def _strict_upper(n):
    r = lax.broadcasted_iota(jnp.int32, (n, n), 0)
    c = lax.broadcasted_iota(jnp.int32, (n, n), 1)
    return jnp.where(r > c, 1.0, 0.0).astype(BF16)


def _softplus(z):
    return jnp.maximum(z, 0.0) + jnp.log(1.0 + jnp.exp(-jnp.abs(z)))


def _sum_later(sp, upper):
    hi = sp.astype(BF16)
    lo = (sp - hi.astype(F32)).astype(BF16)
    return jnp.dot(jnp.concatenate([hi, lo], axis=1), jnp.concatenate([upper, upper], axis=0),
                   preferred_element_type=F32)


def _sb_block(qh, kblk, vblk, upper, carry, mask, transposed=False):
    if transposed:
        z = jnp.dot(qh, kblk, preferred_element_type=F32)
    else:
        z = lax.dot_general(qh, kblk, NT_DIMS, preferred_element_type=F32)
    sp = _softplus(z)
    if mask is not None:
        sp = jnp.where(mask, sp, 0.0)
    later = _sum_later(sp, upper)
    w = jnp.exp((z - sp) - later - carry)
    if mask is not None:
        w = jnp.where(mask, w, 0.0)
    if transposed:
        out = lax.dot_general(w.astype(BF16), vblk, NT_DIMS, preferred_element_type=F32)
    else:
        out = jnp.dot(w.astype(BF16), vblk, preferred_element_type=F32)
    return out, carry + jnp.sum(sp, axis=-1, keepdims=True)


def _any_live(carries):
    lowest = functools.reduce(jnp.minimum, carries)
    return (jnp.min(lowest) <= DEAD_CARRY).astype(jnp.int32)


def _walk_older(n_blocks, carries, block_fn):
    def test(state):
        return jnp.logical_and(state[0] < n_blocks, state[1] > 0)

    def body(state):
        new = block_fn(state[0], state[2:])
        return (state[0] + 1, _any_live(new), *new)

    lax.while_loop(test, body, (jnp.int32(0), _any_live(carries), *carries))


def _sb_window_kernel(q_ref, k_ref, v_ref, o_ref, acc_ref, *, tq, pairs):
    i = pl.program_id(2)
    lane = lax.broadcasted_iota(jnp.int32, (tq, LANES), 1)
    first = lane < HEAD_DIM
    qs, cols = [], []
    for p in range(pairs):
        col = slice(p * LANES, (p + 1) * LANES)
        q = q_ref[0, :, col]
        zero = jnp.zeros_like(q)
        qs += [jnp.where(first, q, zero), jnp.where(first, zero, q)]
        cols += [col, col]
    n = len(qs)
    r = lax.broadcasted_iota(jnp.int32, (tq, tq), 0)
    c = lax.broadcasted_iota(jnp.int32, (tq, tq), 1)
    causal = c < r
    upper = _strict_upper(tq)

    @pl.when(i == 0)
    def _():
        zc = jnp.zeros((tq, 1), F32)
        for h in range(n):
            o, _ = _sb_block(qs[h], k_ref[0, 0:tq, cols[h]], v_ref[0, 0:tq, cols[h]],
                             upper, zc, causal)
            acc_ref[h] = o

    @pl.when(i > 0)
    def _():
        ws = pl.multiple_of((i - 1) * tq, tq)
        zs = []
        for p in range(pairs):
            z2 = lax.dot_general(jnp.concatenate(qs[2 * p:2 * p + 2], axis=0),
                                 k_ref[0, pl.ds(ws, 2 * tq), cols[2 * p]], NT_DIMS,
                                 preferred_element_type=F32)
            zs += [z2[:tq], z2[tq:]]
        zp = [z[:, :tq] for z in zs]
        zd = [z[:, tq:] for z in zs]
        sp_p = [_softplus(z) for z in zp]
        sp_d = [jnp.where(causal, _softplus(z), 0.0) for z in zd]
        later = _sum_later(jnp.concatenate(sp_p + sp_d, axis=0), upper)
        carries, ws_bf16 = [], []
        for h in range(n):
            rs_d = jnp.sum(sp_d[h], axis=-1, keepdims=True)
            later_p = later[h * tq:(h + 1) * tq]
            later_d = later[(n + h) * tq:(n + h + 1) * tq]
            w_p = jnp.exp((zp[h] - sp_p[h]) - later_p - rs_d)
            w_d = jnp.where(causal, jnp.exp((zd[h] - sp_d[h]) - later_d), 0.0)
            ws_bf16.append(jnp.concatenate([w_p.astype(BF16), w_d.astype(BF16)], axis=1))
            carries.append(rs_d + jnp.sum(sp_p[h], axis=-1, keepdims=True))
        for p in range(pairs):
            o2 = jnp.dot(jnp.concatenate(ws_bf16[2 * p:2 * p + 2], axis=0),
                         v_ref[0, pl.ds(ws, 2 * tq), cols[2 * p]], preferred_element_type=F32)
            acc_ref[2 * p] = o2[:tq]
            acc_ref[2 * p + 1] = o2[tq:]

        n_older = i - 1

        def older_block(j, cs):
            start = pl.multiple_of((n_older - 1 - j) * tq, tq)
            new = []
            for h in range(n):
                o, ch = _sb_block(qs[h], k_ref[0, pl.ds(start, tq), cols[h]],
                                  v_ref[0, pl.ds(start, tq), cols[h]], upper, cs[h], None)
                acc_ref[h] += o
                new.append(ch)
            return new

        _walk_older(n_older, carries, older_block)

    for p in range(pairs):
        o_ref[0, :, p * LANES:(p + 1) * LANES] = jnp.where(
            first, acc_ref[2 * p], acc_ref[2 * p + 1])


def _stick_breaking_self(qkv, *, tq, sb_width, pairs_per_step):
    b, t, _ = qkv.shape
    width = pairs_per_step * LANES
    groups = sb_width // width
    return pl.pallas_call(
        functools.partial(_sb_window_kernel, tq=tq, pairs=pairs_per_step),
        grid=(b, groups, t // tq),
        in_specs=[
            pl.BlockSpec((1, tq, width), lambda bi, g, i: (bi, i, g)),
            pl.BlockSpec((1, t, width), lambda bi, g, i: (bi, 0, groups + g)),
            pl.BlockSpec((1, t, width), lambda bi, g, i: (bi, 0, 2 * groups + g)),
        ],
        out_specs=pl.BlockSpec((1, tq, width), lambda bi, g, i: (bi, i, g)),
        out_shape=jax.ShapeDtypeStruct((b, t, sb_width), F32),
        scratch_shapes=[pltpu.VMEM((2 * pairs_per_step, tq, LANES), F32)],
        compiler_params=pltpu.CompilerParams(
            dimension_semantics=("arbitrary", "arbitrary", "arbitrary"),
            vmem_limit_bytes=VMEM_LIMIT),
        name="stick_breaking_self",
    )(qkv, qkv, qkv)


def _sb_cached_kernel(qkv_ref, kwin_ref, vwin_ref, kc_ref, vc_ref, o_ref,
                      kbuf, vbuf, acc_ref, sem, *, heads, tk, n_older):
    b = pl.program_id(0)
    tq = qkv_ref.shape[1]
    s = heads * HEAD_DIM
    r = lax.broadcasted_iota(jnp.int32, (tq, tq), 0)
    c = lax.broadcasted_iota(jnp.int32, (tq, tq), 1)
    causal = c < r
    upper_d = _strict_upper(tq)
    upper = _strict_upper(tk)
    q = qkv_ref[0, :, 0:s]
    kn = qkv_ref[0, :, s:2 * s]
    vn = qkv_ref[0, :, 2 * s:3 * s]
    head = lambda x, h: x[:, h * HEAD_DIM:(h + 1) * HEAD_DIM]
    qs = [head(q, h) for h in range(heads)]

    zd = [lax.dot_general(qs[h], head(kn, h), NT_DIMS, preferred_element_type=F32)
          for h in range(heads)]
    zw = [jnp.dot(qs[h], kwin_ref[0, h].astype(BF16), preferred_element_type=F32)
          for h in range(heads)]
    sp_d = [jnp.where(causal, _softplus(z), 0.0) for z in zd]
    sp_w = [_softplus(z) for z in zw]
    later_d = _sum_later(jnp.concatenate(sp_d, axis=0), upper_d)
    later_w = _sum_later(jnp.concatenate(sp_w, axis=0), upper)
    carries = []
    for h in range(heads):
        rows = slice(h * tq, (h + 1) * tq)
        rs_d = jnp.sum(sp_d[h], axis=-1, keepdims=True)
        w_d = jnp.where(causal, jnp.exp((zd[h] - sp_d[h]) - later_d[rows]), 0.0)
        w_w = jnp.exp((zw[h] - sp_w[h]) - later_w[rows] - rs_d)
        acc_ref[h] = (jnp.dot(w_d.astype(BF16), head(vn, h), preferred_element_type=F32)
                      + lax.dot_general(w_w.astype(BF16), vwin_ref[0, h].astype(BF16), NT_DIMS,
                                        preferred_element_type=F32))
        carries.append(rs_d + jnp.sum(sp_w[h], axis=-1, keepdims=True))

    def block_copies(j):
        start = pl.multiple_of((n_older - 1 - j) * tk, tk)
        return (pltpu.make_async_copy(kc_ref.at[b, :, :, pl.ds(start, tk)], kbuf, sem.at[0]),
                pltpu.make_async_copy(vc_ref.at[b, :, :, pl.ds(start, tk)], vbuf, sem.at[1]))

    def older_block(j, cs):
        copies = block_copies(j)
        for cp in copies:
            cp.start()
        for cp in copies:
            cp.wait()
        new = []
        for h in range(heads):
            o, ch = _sb_block(qs[h], kbuf[h].astype(BF16), vbuf[h].astype(BF16),
                              upper, cs[h], None, transposed=True)
            acc_ref[h] += o
            new.append(ch)
        return new

    _walk_older(n_older, carries, older_block)
    o_ref[0] = jnp.concatenate([acc_ref[h] for h in range(heads)], axis=1)


def _stick_breaking_cached(qkv, cache_k, cache_v, *, tk):
    b, t, _ = qkv.shape
    _, heads, dh, p = cache_k.shape
    s = heads * dh
    last = p // tk - 1
    win = pl.BlockSpec((1, heads, dh, tk), lambda bi: (bi, 0, 0, last))
    return pl.pallas_call(
        functools.partial(_sb_cached_kernel, heads=heads, tk=tk, n_older=last),
        grid=(b,),
        in_specs=[
            pl.BlockSpec((1, t, 3 * s), lambda bi: (bi, 0, 0)),
            win, win,
            pl.BlockSpec(memory_space=pl.ANY),
            pl.BlockSpec(memory_space=pl.ANY),
        ],
        out_specs=pl.BlockSpec((1, t, s), lambda bi: (bi, 0, 0)),
        out_shape=jax.ShapeDtypeStruct((b, t, s), F32),
        scratch_shapes=[
            pltpu.VMEM((heads, dh, tk), F32),
            pltpu.VMEM((heads, dh, tk), F32),
            pltpu.VMEM((heads, t, dh), F32),
            pltpu.SemaphoreType.DMA((2,)),
        ],
        compiler_params=pltpu.CompilerParams(
            dimension_semantics=("arbitrary",), vmem_limit_bytes=VMEM_LIMIT),
        name="stick_breaking_cached",
    )(qkv, cache_k, cache_v, cache_k, cache_v)


def _silu(g):
    return g / (1.0 + jnp.exp(-g))


def _epilogue_kernel(x_ref, osb_ref, rest_ref, prev_ref, hist0_ref, mk_ref, mv_ref,
                     poolw_ref, pscale_ref, wout_ref, gfin_ref, y_ref,
                     *, tt, start, sb_width, pool_width, xa_width, xa_heads, apply_final):
    streams = x_ref.shape[0]
    mixed = [_mixed_rows(si, rest_ref, osb_ref, prev_ref, hist0_ref, mk_ref, mv_ref, poolw_ref,
                         pscale_ref, tt=tt, start=start, sb_width=sb_width,
                         pool_width=pool_width, xa_width=xa_width, xa_heads=xa_heads)
             for si in range(streams)]
    m_sb, m_pool, m_xa = (jnp.concatenate([m[k] for m in mixed], axis=0) for k in range(3))
    s, pw, xw = sb_width, pool_width, xa_width
    proj = (jnp.dot(m_sb, wout_ref[0:s, :], preferred_element_type=F32)
            + jnp.dot(m_pool, wout_ref[s:s + pw, :], preferred_element_type=F32)
            + jnp.dot(m_xa, wout_ref[s + pw:s + pw + xw, :], preferred_element_type=F32))
    for si in range(streams):
        y = x_ref[si] + proj[si * tt:(si + 1) * tt]
        if apply_final:
            y = _rms(y, gfin_ref[...])
        y_ref[si] = y


def _mixed_rows(si, rest_ref, osb_ref, prev_ref, hist0_ref, mk_ref, mv_ref, poolw_ref, pscale_ref,
                *, tt, start, sb_width, pool_width, xa_width, xa_heads):
    i = pl.program_id(1)
    s, pw, xw = sb_width, pool_width, xa_width
    g_sb = rest_ref[si, :, 0:s]
    u = rest_ref[si, :, s:s + pw]
    g_pool = rest_ref[si, :, s + pw:s + 2 * pw]
    q_xa = rest_ref[si, :, s + 2 * pw:s + 2 * pw + xw]
    g_xa = rest_ref[si, :, s + 2 * pw + xw:s + 2 * pw + 2 * xw]

    hist = jnp.where(i == 0, hist0_ref[si], prev_ref[si])
    ext = jnp.concatenate([hist, u], axis=0)
    sums = {1: ext}
    width = 1
    while width < max(POOL_WINDOWS):
        prev = sums[width]
        sums[2 * width] = prev + pltpu.roll(prev, width, 0)
        width *= 2
    lane = lax.broadcasted_iota(jnp.int32, (tt, pw), 1)
    group = lane // (pw // len(POOL_WINDOWS))
    pos = start + i * tt + lax.broadcasted_iota(jnp.int32, (tt, pw), 0)
    win_sum = jnp.zeros((tt, pw), F32)
    win = jnp.zeros((tt, pw), jnp.int32)
    for gi, w in enumerate(POOL_WINDOWS):
        sel = group == gi
        win_sum = jnp.where(sel, sums[w][HIST_ROWS:], win_sum)
        win = jnp.where(sel, w, win)
    cnt = jnp.minimum(pos + 1, win).astype(F32)
    pooled = win_sum / cnt - u
    o_pool = jnp.dot(pooled.astype(BF16), poolw_ref[...],
                     preferred_element_type=F32) * pscale_ref[...]

    mk = mk_ref[si].astype(BF16)
    mv = mv_ref[si].astype(BF16)
    xa_dim = xw // xa_heads
    lane_x = lax.broadcasted_iota(jnp.int32, (tt, xw), 1)
    head_x = lane_x // xa_dim
    qs = q_xa * (xa_dim ** -0.5)
    o_xa = jnp.zeros((tt, xw), F32)
    for hh in range(xa_heads):
        sel = head_x == hh
        qh = jnp.where(sel, qs, 0.0).astype(BF16)
        sc = lax.dot_general(qh, mk, NT_DIMS, preferred_element_type=F32)
        e = jnp.exp(sc - jnp.max(sc, axis=-1, keepdims=True))
        p = e / jnp.sum(e, axis=-1, keepdims=True)
        oh = jnp.dot(p.astype(BF16), mv, preferred_element_type=F32)
        o_xa = jnp.where(sel, oh, o_xa)

    return ((osb_ref[si] * _silu(g_sb)).astype(BF16),
            (o_pool * _silu(g_pool)).astype(BF16),
            (o_xa * _silu(g_xa)).astype(BF16))


def _epilogue(x, o_sb, rest, hist0, mk, mv, pool_bd, pool_scale, w_out_bf16, g_final,
              *, tt, start, sb_width, pool_width, xa_width, xa_heads, apply_final):
    b, t, d = x.shape
    bb = max(1, min(b, EPILOGUE_ROWS // tt))
    while b % bb:
        bb -= 1
    n_rest = rest.shape[-1]
    n_mem = mk.shape[1]
    hist_blocks = tt // HIST_ROWS
    u_col = sb_width // pool_width
    kern = functools.partial(
        _epilogue_kernel, tt=tt, start=start, sb_width=sb_width, pool_width=pool_width,
        xa_width=xa_width, xa_heads=xa_heads, apply_final=apply_final)
    return pl.pallas_call(
        kern,
        grid=(b // bb, t // tt),
        in_specs=[
            pl.BlockSpec((bb, tt, d), lambda bi, i: (bi, i, 0)),
            pl.BlockSpec((bb, tt, sb_width), lambda bi, i: (bi, i, 0)),
            pl.BlockSpec((bb, tt, n_rest), lambda bi, i: (bi, i, 0)),
            pl.BlockSpec((bb, HIST_ROWS, pool_width),
                         lambda bi, i: (bi, jnp.maximum(i * hist_blocks - 1, 0), u_col)),
            pl.BlockSpec((bb, HIST_ROWS, pool_width), lambda bi, i: (bi, 0, 0)),
            pl.BlockSpec((bb, n_mem, xa_width), lambda bi, i: (bi, 0, 0)),
            pl.BlockSpec((bb, n_mem, xa_width), lambda bi, i: (bi, 0, 0)),
            pl.BlockSpec((pool_width, pool_width), lambda bi, i: (0, 0)),
            pl.BlockSpec((1, pool_width), lambda bi, i: (0, 0)),
            pl.BlockSpec((d, d), lambda bi, i: (0, 0)),
            pl.BlockSpec((1, d), lambda bi, i: (0, 0)),
        ],
        out_specs=pl.BlockSpec((bb, tt, d), lambda bi, i: (bi, i, 0)),
        out_shape=jax.ShapeDtypeStruct((b, t, d), F32),
        compiler_params=pltpu.CompilerParams(
            dimension_semantics=("arbitrary", "arbitrary"), vmem_limit_bytes=VMEM_LIMIT),
        name="epilogue",
    )(x, o_sb, rest, rest, hist0, mk, mv, pool_bd, pool_scale, w_out_bf16, g_final)


def _block_diag(pool_w):
    g, c, _ = pool_w.shape
    eye = jnp.eye(g, dtype=pool_w.dtype)
    return (eye[:, None, :, None] * pool_w[:, :, None, :]).reshape(g * c, g * c)


def _pad_hist(hist):
    return jnp.pad(hist, ((0, 0), (HIST_ROWS - hist.shape[1], 0), (0, 0)))


def _mixer_layer(x, start, k_past, v_past, pool_hist, mk, mv, g_norm, w_in_bf16, wkv_t, pool_bd,
                 pool_scale, w_out_bf16, g_final, *, dims, tiles, apply_final):
    sb_width, pool_width, xa_width, xa_heads = dims
    tm, tq, tk, tt = tiles
    b, t, d = x.shape
    heads = sb_width // HEAD_DIM
    k_new, v_new, qkv, rest = _project(x.reshape(b * t, d), g_norm, w_in_bf16, wkv_t,
                                       sb_width, tm, t)
    if k_new.ndim == 3:
        to_cache = lambda a: jnp.transpose(a.reshape(b, heads, HEAD_DIM, t), (0, 3, 1, 2))
    else:
        to_cache = lambda a: a.reshape(b, t, heads, HEAD_DIM)
    qkv = qkv.reshape(b, t, -1)
    rest = rest.reshape(b, t, -1)
    if k_past is None:
        o_sb = _stick_breaking_self(qkv, tq=tq, sb_width=sb_width,
                                    pairs_per_step=PAIRS_PER_STEP)
    else:
        o_sb = _stick_breaking_cached(qkv, k_past, v_past, tk=tk)
    y = _epilogue(x, o_sb, rest, _pad_hist(pool_hist), mk, mv, pool_bd, pool_scale,
                  w_out_bf16, g_final, tt=tt, start=start, sb_width=sb_width,
                  pool_width=pool_width, xa_width=xa_width, xa_heads=xa_heads,
                  apply_final=apply_final)
    u_pool = rest[:, :, sb_width:sb_width + pool_width]
    new_hist = jnp.concatenate([pool_hist, u_pool], axis=1)[:, -POOL_STATE:]
    return y, to_cache(k_new), to_cache(v_new), new_hist


def kernel(x_prompt, x_sample, cache_sb_k, cache_sb_v, state_pool, cache_mem_k, cache_mem_v,
           mem_prompt, g_norm, w_in, pool_w, pool_scale, g_mem, w_mem_kv, w_out, g_final):
    depth = g_norm.shape[0]
    bp, tp, d = x_prompt.shape
    bs, ts, _ = x_sample.shape
    past = cache_sb_k.shape[2]
    sb_heads, head_dim = cache_sb_k.shape[3], cache_sb_k.shape[4]
    assert head_dim == HEAD_DIM
    sb_width = sb_heads * head_dim
    pool_width = pool_w.shape[1] * pool_w.shape[2]
    n_mem, xa_heads, xa_dim = cache_mem_k.shape[2:]
    xa_width = xa_heads * xa_dim
    dims = (sb_width, pool_width, xa_width, xa_heads)
    g_fin = g_final.reshape(1, d)

    yp, ys = x_prompt, x_sample
    kp_l, vp_l, hp_l, mkp_l, mvp_l, ks_l, vs_l, hs_l = [], [], [], [], [], [], [], []
    for l in range(depth):
        final = l == depth - 1
        g_l = g_norm[l].reshape(1, d)
        w_in_b = w_in[l].astype(BF16)
        w_out_b = w_out[l].astype(BF16)
        pool_bd = _block_diag(pool_w[l]).astype(BF16)
        p_scale = pool_scale[l].reshape(1, pool_width)
        mk, mv = _memory_kv(mem_prompt.reshape(bp * n_mem, d), g_mem[l].reshape(1, d),
                            w_mem_kv[l].astype(BF16), tm=256)
        mk = mk.reshape(bp, n_mem, xa_width)
        mv = mv.reshape(bp, n_mem, xa_width)
        hist0 = jnp.zeros((bp, POOL_STATE, pool_width), x_prompt.dtype)
        wkv_t = jnp.transpose(w_in[l][:, sb_width:3 * sb_width]).astype(BF16)
        yp, kp, vp, hp = _mixer_layer(
            yp, 0, None, None, hist0, mk, mv, g_l, w_in_b, wkv_t, pool_bd, p_scale, w_out_b,
            g_fin, dims=dims, tiles=(512, 256, 256, 1024), apply_final=final)
        ys, kn, vn, hn = _mixer_layer(
            ys, past, jnp.transpose(cache_sb_k[l], (0, 2, 3, 1)),
            jnp.transpose(cache_sb_v[l], (0, 2, 3, 1)), state_pool[l],
            cache_mem_k[l].reshape(bs, n_mem, xa_width),
            cache_mem_v[l].reshape(bs, n_mem, xa_width),
            g_l, w_in_b, wkv_t, pool_bd, p_scale, w_out_b, g_fin,
            dims=dims, tiles=(512, ts, 256, ts), apply_final=final)
        kp_l.append(kp)
        vp_l.append(vp)
        hp_l.append(hp)
        mkp_l.append(mk.reshape(bp, n_mem, xa_heads, xa_dim))
        mvp_l.append(mv.reshape(bp, n_mem, xa_heads, xa_dim))
        ks_l.append(kn)
        vs_l.append(vn)
        hs_l.append(hn)
    stack = lambda xs: jnp.stack(xs, axis=0)
    return (yp, ys, stack(kp_l), stack(vp_l), stack(hp_l), stack(mkp_l), stack(mvp_l),
            stack(ks_l), stack(vs_l), stack(hs_l))
```

```python
import functools

import jax
import jax.numpy as jnp
from jax import lax
from jax.experimental import pallas as pl
from jax.experimental.pallas import tpu as pltpu

F32 = jnp.float32
BF16 = jnp.bfloat16

HEAD_DIM = 64
LANES = 128
POOL_WINDOWS = (2, 4, 8, 16)
POOL_STATE = max(POOL_WINDOWS) - 1
HIST_ROWS = 16
EPS = 1e-6
DEAD_CARRY = 104.0
VMEM_LIMIT = 56 * 1024 * 1024
NT_DIMS = (((1,), (1,)), ((), ()))
PAIRS_PER_STEP = 2
EPILOGUE_ROWS = 256


def _rms(x, g):
    ms = jnp.mean(x * x, axis=-1, keepdims=True)
    return (x * lax.rsqrt(ms + EPS)) * g


def _proj_kernel(x_ref, g_ref, w_ref, wkv_t_ref, k_ref, v_ref, qkv_ref, rest_ref,
                 *, sb_width, time_minor):
    h = _rms(x_ref[...], g_ref[...]).astype(BF16)
    s = sb_width
    q = jnp.dot(h, w_ref[:, 0:s], preferred_element_type=F32)
    qkv_ref[:, 0:s] = (q * (HEAD_DIM ** -0.5)).astype(BF16)
    if time_minor:
        kt = lax.dot_general(wkv_t_ref[0:s, :], h, NT_DIMS, preferred_element_type=F32)
        vt = lax.dot_general(wkv_t_ref[s:2 * s, :], h, NT_DIMS, preferred_element_type=F32)
        k_ref[0] = kt
        v_ref[0] = vt
        qkv_ref[:, s:2 * s] = kt.T.astype(BF16)
        qkv_ref[:, 2 * s:3 * s] = vt.T.astype(BF16)
    else:
        k = jnp.dot(h, w_ref[:, s:2 * s], preferred_element_type=F32)
        v = jnp.dot(h, w_ref[:, 2 * s:3 * s], preferred_element_type=F32)
        k_ref[...] = k
        v_ref[...] = v
        qkv_ref[:, s:2 * s] = k.astype(BF16)
        qkv_ref[:, 2 * s:3 * s] = v.astype(BF16)
    rest_ref[...] = jnp.dot(h, w_ref[:, 3 * s:], preferred_element_type=F32)


def _project(x2d, g, w_bf16, wkv_t, sb_width, tm, rows_per_stream):
    m, d = x2d.shape
    n = w_bf16.shape[1]
    n_rest = n - 3 * sb_width
    time_minor = rows_per_stream % tm == 0
    if time_minor:
        nt = rows_per_stream // tm
        kv_spec = pl.BlockSpec((1, sb_width, tm), lambda i: (i // nt, 0, i % nt))
        kv_shape = jax.ShapeDtypeStruct((m // rows_per_stream, sb_width, rows_per_stream), F32)
    else:
        kv_spec = pl.BlockSpec((tm, sb_width), lambda i: (i, 0))
        kv_shape = jax.ShapeDtypeStruct((m, sb_width), F32)
    return pl.pallas_call(
        functools.partial(_proj_kernel, sb_width=sb_width, time_minor=time_minor),
        grid=(m // tm,),
        in_specs=[
            pl.BlockSpec((tm, d), lambda i: (i, 0)),
            pl.BlockSpec((1, d), lambda i: (0, 0)),
            pl.BlockSpec((d, n), lambda i: (0, 0)),
            pl.BlockSpec((2 * sb_width, d), lambda i: (0, 0)),
        ],
        out_specs=[
            kv_spec,
            kv_spec,
            pl.BlockSpec((tm, 3 * sb_width), lambda i: (i, 0)),
            pl.BlockSpec((tm, n_rest), lambda i: (i, 0)),
        ],
        out_shape=[
            kv_shape,
            kv_shape,
            jax.ShapeDtypeStruct((m, 3 * sb_width), BF16),
            jax.ShapeDtypeStruct((m, n_rest), F32),
        ],
        compiler_params=pltpu.CompilerParams(
            dimension_semantics=("arbitrary",), vmem_limit_bytes=VMEM_LIMIT),
        name="proj",
    )(x2d, g, w_bf16, wkv_t)


def _memkv_kernel(x_ref, g_ref, w_ref, mk_ref, mv_ref, *, xa_width):
    h = _rms(x_ref[...], g_ref[...]).astype(BF16)
    kv = jnp.dot(h, w_ref[...], preferred_element_type=F32)
    mk_ref[...] = kv[:, :xa_width]
    mv_ref[...] = kv[:, xa_width:]


def _memory_kv(mem2d, g, w_bf16, tm):
    m, d = mem2d.shape
    xa_width = w_bf16.shape[1] // 2
    return pl.pallas_call(
        functools.partial(_memkv_kernel, xa_width=xa_width),
        grid=(m // tm,),
        in_specs=[
            pl.BlockSpec((tm, d), lambda i: (i, 0)),
            pl.BlockSpec((1, d), lambda i: (0, 0)),
            pl.BlockSpec((d, 2 * xa_width), lambda i: (0, 0)),
        ],
        out_specs=[
            pl.BlockSpec((tm, xa_width), lambda i: (i, 0)),
            pl.BlockSpec((tm, xa_width), lambda i: (i, 0)),
        ],
        out_shape=[
            jax.ShapeDtypeStruct((m, xa_width), F32),
            jax.ShapeDtypeStruct((m, xa_width), F32),
        ],
        compiler_params=pltpu.CompilerParams(
            dimension_semantics=("arbitrary",), vmem_limit_bytes=VMEM_LIMIT),
        name="memkv",
    )(mem2d, g, w_bf16)


def _self_and_later(n):
    r = lax.broadcasted_iota(jnp.int32, (n, n), 0)
    c = lax.broadcasted_iota(jnp.int32, (n, n), 1)
    return jnp.where(r >= c, 1.0, 0.0).astype(BF16)


def _softplus(z):
    return jnp.maximum(z, 0.0) + jnp.log(1.0 + jnp.exp(-jnp.abs(z)))


def _sum_later(sp, upper):
    hi = sp.astype(BF16)
    lo = (sp - hi.astype(F32)).astype(BF16)
    return jnp.dot(jnp.concatenate([hi, lo], axis=1), jnp.concatenate([upper, upper], axis=0),
                   preferred_element_type=F32)


def _sb_block(qh, kblk, vblk, upper, carry, mask, transposed=False):
    if transposed:
        z = jnp.dot(qh, kblk, preferred_element_type=F32)
    else:
        z = lax.dot_general(qh, kblk, NT_DIMS, preferred_element_type=F32)
    sp = _softplus(z)
    if mask is not None:
        sp = jnp.where(mask, sp, 0.0)
    later = _sum_later(sp, upper)
    w = jnp.exp(z - later - carry)
    if mask is not None:
        w = jnp.where(mask, w, 0.0)
    if transposed:
        out = lax.dot_general(w.astype(BF16), vblk, NT_DIMS, preferred_element_type=F32)
    else:
        out = jnp.dot(w.astype(BF16), vblk, preferred_element_type=F32)
    return out, carry + jnp.sum(sp, axis=-1, keepdims=True)


def _any_live(carries):
    lowest = functools.reduce(jnp.minimum, carries)
    return (jnp.min(lowest) <= DEAD_CARRY).astype(jnp.int32)


def _walk_older(n_blocks, carries, block_fn):
    def test(state):
        return jnp.logical_and(state[0] < n_blocks, state[1] > 0)

    def body(state):
        new = block_fn(state[0], state[2:])
        return (state[0] + 1, _any_live(new), *new)

    lax.while_loop(test, body, (jnp.int32(0), _any_live(carries), *carries))


def _sb_window_kernel(q_ref, k_ref, v_ref, o_ref, acc_ref, *, tq, pairs):
    i = pl.program_id(2)
    lane = lax.broadcasted_iota(jnp.int32, (tq, LANES), 1)
    first = lane < HEAD_DIM
    qs, cols = [], []
    for p in range(pairs):
        col = slice(p * LANES, (p + 1) * LANES)
        q = q_ref[0, :, col]
        zero = jnp.zeros_like(q)
        qs += [jnp.where(first, q, zero), jnp.where(first, zero, q)]
        cols += [col, col]
    n = len(qs)
    r = lax.broadcasted_iota(jnp.int32, (tq, tq), 0)
    c = lax.broadcasted_iota(jnp.int32, (tq, tq), 1)
    causal = c < r
    upper = _self_and_later(tq)

    @pl.when(i == 0)
    def _():
        zc = jnp.zeros((tq, 1), F32)
        for h in range(n):
            o, _ = _sb_block(qs[h], k_ref[0, 0:tq, cols[h]], v_ref[0, 0:tq, cols[h]],
                             upper, zc, causal)
            acc_ref[h] = o

    @pl.when(i > 0)
    def _():
        ws = pl.multiple_of((i - 1) * tq, tq)
        zs = []
        for p in range(pairs):
            z2 = lax.dot_general(jnp.concatenate(qs[2 * p:2 * p + 2], axis=0),
                                 k_ref[0, pl.ds(ws, 2 * tq), cols[2 * p]], NT_DIMS,
                                 preferred_element_type=F32)
            zs += [z2[:tq], z2[tq:]]
        zp = [z[:, :tq] for z in zs]
        zd = [z[:, tq:] for z in zs]
        sp_p = [_softplus(z) for z in zp]
        sp_d = [jnp.where(causal, _softplus(z), 0.0) for z in zd]
        later = _sum_later(jnp.concatenate(sp_p + sp_d, axis=0), upper)
        carries, ws_bf16 = [], []
        for h in range(n):
            rs_d = jnp.sum(sp_d[h], axis=-1, keepdims=True)
            later_p = later[h * tq:(h + 1) * tq]
            later_d = later[(n + h) * tq:(n + h + 1) * tq]
            w_p = jnp.exp(zp[h] - later_p - rs_d)
            w_d = jnp.where(causal, jnp.exp(zd[h] - later_d), 0.0)
            ws_bf16.append(jnp.concatenate([w_p.astype(BF16), w_d.astype(BF16)], axis=1))
            carries.append(rs_d + jnp.sum(sp_p[h], axis=-1, keepdims=True))
        for p in range(pairs):
            o2 = jnp.dot(jnp.concatenate(ws_bf16[2 * p:2 * p + 2], axis=0),
                         v_ref[0, pl.ds(ws, 2 * tq), cols[2 * p]], preferred_element_type=F32)
            acc_ref[2 * p] = o2[:tq]
            acc_ref[2 * p + 1] = o2[tq:]

        n_older = i - 1

        def older_block(j, cs):
            start = pl.multiple_of((n_older - 1 - j) * tq, tq)
            new = []
            for h in range(n):
                o, ch = _sb_block(qs[h], k_ref[0, pl.ds(start, tq), cols[h]],
                                  v_ref[0, pl.ds(start, tq), cols[h]], upper, cs[h], None)
                acc_ref[h] += o
                new.append(ch)
            return new

        _walk_older(n_older, carries, older_block)

    for p in range(pairs):
        o_ref[0, :, p * LANES:(p + 1) * LANES] = jnp.where(
            first, acc_ref[2 * p], acc_ref[2 * p + 1])


def _stick_breaking_self(qkv, *, tq, sb_width, pairs_per_step):
    b, t, _ = qkv.shape
    width = pairs_per_step * LANES
    groups = sb_width // width
    return pl.pallas_call(
        functools.partial(_sb_window_kernel, tq=tq, pairs=pairs_per_step),
        grid=(b, groups, t // tq),
        in_specs=[
            pl.BlockSpec((1, tq, width), lambda bi, g, i: (bi, i, g)),
            pl.BlockSpec((1, t, width), lambda bi, g, i: (bi, 0, groups + g)),
            pl.BlockSpec((1, t, width), lambda bi, g, i: (bi, 0, 2 * groups + g)),
        ],
        out_specs=pl.BlockSpec((1, tq, width), lambda bi, g, i: (bi, i, g)),
        out_shape=jax.ShapeDtypeStruct((b, t, sb_width), F32),
        scratch_shapes=[pltpu.VMEM((2 * pairs_per_step, tq, LANES), F32)],
        compiler_params=pltpu.CompilerParams(
            dimension_semantics=("arbitrary", "arbitrary", "arbitrary"),
            vmem_limit_bytes=VMEM_LIMIT),
        name="stick_breaking_self",
    )(qkv, qkv, qkv)


def _sb_cached_kernel(qkv_ref, kwin_ref, vwin_ref, kc_ref, vc_ref, o_ref,
                      kbuf, vbuf, acc_ref, sem, *, heads, tk, n_older):
    b = pl.program_id(0)
    tq = qkv_ref.shape[1]
    s = heads * HEAD_DIM
    r = lax.broadcasted_iota(jnp.int32, (tq, tq), 0)
    c = lax.broadcasted_iota(jnp.int32, (tq, tq), 1)
    causal = c < r
    upper_d = _self_and_later(tq)
    upper = _self_and_later(tk)
    q = qkv_ref[0, :, 0:s]
    kn = qkv_ref[0, :, s:2 * s]
    vn = qkv_ref[0, :, 2 * s:3 * s]
    head = lambda x, h: x[:, h * HEAD_DIM:(h + 1) * HEAD_DIM]
    qs = [head(q, h) for h in range(heads)]

    zd = [lax.dot_general(qs[h], head(kn, h), NT_DIMS, preferred_element_type=F32)
          for h in range(heads)]
    zw = [jnp.dot(qs[h], kwin_ref[0, h].astype(BF16), preferred_element_type=F32)
          for h in range(heads)]
    sp_d = [jnp.where(causal, _softplus(z), 0.0) for z in zd]
    sp_w = [_softplus(z) for z in zw]
    later_d = _sum_later(jnp.concatenate(sp_d, axis=0), upper_d)
    later_w = _sum_later(jnp.concatenate(sp_w, axis=0), upper)
    carries = []
    for h in range(heads):
        rows = slice(h * tq, (h + 1) * tq)
        rs_d = jnp.sum(sp_d[h], axis=-1, keepdims=True)
        w_d = jnp.where(causal, jnp.exp(zd[h] - later_d[rows]), 0.0)
        w_w = jnp.exp(zw[h] - later_w[rows] - rs_d)
        acc_ref[h] = (jnp.dot(w_d.astype(BF16), head(vn, h), preferred_element_type=F32)
                      + lax.dot_general(w_w.astype(BF16), vwin_ref[0, h].astype(BF16), NT_DIMS,
                                        preferred_element_type=F32))
        carries.append(rs_d + jnp.sum(sp_w[h], axis=-1, keepdims=True))

    def block_copies(j):
        start = pl.multiple_of((n_older - 1 - j) * tk, tk)
        return (pltpu.make_async_copy(kc_ref.at[b, :, :, pl.ds(start, tk)], kbuf, sem.at[0]),
                pltpu.make_async_copy(vc_ref.at[b, :, :, pl.ds(start, tk)], vbuf, sem.at[1]))

    def older_block(j, cs):
        copies = block_copies(j)
        for cp in copies:
            cp.start()
        for cp in copies:
            cp.wait()
        new = []
        for h in range(heads):
            o, ch = _sb_block(qs[h], kbuf[h].astype(BF16), vbuf[h].astype(BF16),
                              upper, cs[h], None, transposed=True)
            acc_ref[h] += o
            new.append(ch)
        return new

    _walk_older(n_older, carries, older_block)
    o_ref[0] = jnp.concatenate([acc_ref[h] for h in range(heads)], axis=1)


def _stick_breaking_cached(qkv, cache_k, cache_v, *, tk):
    b, t, _ = qkv.shape
    _, heads, dh, p = cache_k.shape
    s = heads * dh
    last = p // tk - 1
    win = pl.BlockSpec((1, heads, dh, tk), lambda bi: (bi, 0, 0, last))
    return pl.pallas_call(
        functools.partial(_sb_cached_kernel, heads=heads, tk=tk, n_older=last),
        grid=(b,),
        in_specs=[
            pl.BlockSpec((1, t, 3 * s), lambda bi: (bi, 0, 0)),
            win, win,
            pl.BlockSpec(memory_space=pl.ANY),
            pl.BlockSpec(memory_space=pl.ANY),
        ],
        out_specs=pl.BlockSpec((1, t, s), lambda bi: (bi, 0, 0)),
        out_shape=jax.ShapeDtypeStruct((b, t, s), F32),
        scratch_shapes=[
            pltpu.VMEM((heads, dh, tk), F32),
            pltpu.VMEM((heads, dh, tk), F32),
            pltpu.VMEM((heads, t, dh), F32),
            pltpu.SemaphoreType.DMA((2,)),
        ],
        compiler_params=pltpu.CompilerParams(
            dimension_semantics=("arbitrary",), vmem_limit_bytes=VMEM_LIMIT),
        name="stick_breaking_cached",
    )(qkv, cache_k, cache_v, cache_k, cache_v)


def _silu(g):
    return g / (1.0 + jnp.exp(-g))


def _epilogue_kernel(x_ref, osb_ref, rest_ref, prev_ref, hist0_ref, mk_ref, mv_ref,
                     poolw_ref, pscale_ref, wout_ref, gfin_ref, y_ref,
                     *, tt, start, sb_width, pool_width, xa_width, xa_heads, apply_final):
    streams = x_ref.shape[0]
    mixed = [_mixed_rows(si, rest_ref, osb_ref, prev_ref, hist0_ref, mk_ref, mv_ref, poolw_ref,
                         pscale_ref, tt=tt, start=start, sb_width=sb_width,
                         pool_width=pool_width, xa_width=xa_width, xa_heads=xa_heads)
             for si in range(streams)]
    m_sb, m_pool, m_xa = (jnp.concatenate([m[k] for m in mixed], axis=0) for k in range(3))
    s, pw, xw = sb_width, pool_width, xa_width
    proj = (jnp.dot(m_sb, wout_ref[0:s, :], preferred_element_type=F32)
            + jnp.dot(m_pool, wout_ref[s:s + pw, :], preferred_element_type=F32)
            + jnp.dot(m_xa, wout_ref[s + pw:s + pw + xw, :], preferred_element_type=F32))
    for si in range(streams):
        y = x_ref[si] + proj[si * tt:(si + 1) * tt]
        if apply_final:
            y = _rms(y, gfin_ref[...])
        y_ref[si] = y


def _mixed_rows(si, rest_ref, osb_ref, prev_ref, hist0_ref, mk_ref, mv_ref, poolw_ref, pscale_ref,
                *, tt, start, sb_width, pool_width, xa_width, xa_heads):
    i = pl.program_id(1)
    s, pw, xw = sb_width, pool_width, xa_width
    g_sb = rest_ref[si, :, 0:s]
    u = rest_ref[si, :, s:s + pw]
    g_pool = rest_ref[si, :, s + pw:s + 2 * pw]
    q_xa = rest_ref[si, :, s + 2 * pw:s + 2 * pw + xw]
    g_xa = rest_ref[si, :, s + 2 * pw + xw:s + 2 * pw + 2 * xw]

    hist = jnp.where(i == 0, hist0_ref[si], prev_ref[si])
    ext = jnp.concatenate([hist, u], axis=0)
    sums = {1: ext}
    width = 1
    while width < max(POOL_WINDOWS):
        prev = sums[width]
        sums[2 * width] = prev + pltpu.roll(prev, width, 0)
        width *= 2
    lane = lax.broadcasted_iota(jnp.int32, (tt, pw), 1)
    group = lane // (pw // len(POOL_WINDOWS))
    pos = start + i * tt + lax.broadcasted_iota(jnp.int32, (tt, pw), 0)
    win_sum = jnp.zeros((tt, pw), F32)
    win = jnp.zeros((tt, pw), jnp.int32)
    for gi, w in enumerate(POOL_WINDOWS):
        sel = group == gi
        win_sum = jnp.where(sel, sums[w][HIST_ROWS:], win_sum)
        win = jnp.where(sel, w, win)
    cnt = jnp.minimum(pos + 1, win).astype(F32)
    pooled = win_sum / cnt - u
    o_pool = jnp.dot(pooled.astype(BF16), poolw_ref[...],
                     preferred_element_type=F32) * pscale_ref[...]

    mk = mk_ref[si].astype(BF16)
    mv = mv_ref[si].astype(BF16)
    xa_dim = xw // xa_heads
    lane_x = lax.broadcasted_iota(jnp.int32, (tt, xw), 1)
    head_x = lane_x // xa_dim
    qs = q_xa * (xa_dim ** -0.5)
    o_xa = jnp.zeros((tt, xw), F32)
    for hh in range(xa_heads):
        sel = head_x == hh
        qh = jnp.where(sel, qs, 0.0).astype(BF16)
        sc = lax.dot_general(qh, mk, NT_DIMS, preferred_element_type=F32)
        e = jnp.exp(sc - jnp.max(sc, axis=-1, keepdims=True))
        p = e / jnp.sum(e, axis=-1, keepdims=True)
        oh = jnp.dot(p.astype(BF16), mv, preferred_element_type=F32)
        o_xa = jnp.where(sel, oh, o_xa)

    return ((osb_ref[si] * _silu(g_sb)).astype(BF16),
            (o_pool * _silu(g_pool)).astype(BF16),
            (o_xa * _silu(g_xa)).astype(BF16))


def _epilogue(x, o_sb, rest, hist0, mk, mv, pool_bd, pool_scale, w_out_bf16, g_final,
              *, tt, start, sb_width, pool_width, xa_width, xa_heads, apply_final):
    b, t, d = x.shape
    bb = max(1, min(b, EPILOGUE_ROWS // tt))
    while b % bb:
        bb -= 1
    n_rest = rest.shape[-1]
    n_mem = mk.shape[1]
    hist_blocks = tt // HIST_ROWS
    u_col = sb_width // pool_width
    kern = functools.partial(
        _epilogue_kernel, tt=tt, start=start, sb_width=sb_width, pool_width=pool_width,
        xa_width=xa_width, xa_heads=xa_heads, apply_final=apply_final)
    return pl.pallas_call(
        kern,
        grid=(b // bb, t // tt),
        in_specs=[
            pl.BlockSpec((bb, tt, d), lambda bi, i: (bi, i, 0)),
            pl.BlockSpec((bb, tt, sb_width), lambda bi, i: (bi, i, 0)),
            pl.BlockSpec((bb, tt, n_rest), lambda bi, i: (bi, i, 0)),
            pl.BlockSpec((bb, HIST_ROWS, pool_width),
                         lambda bi, i: (bi, jnp.maximum(i * hist_blocks - 1, 0), u_col)),
            pl.BlockSpec((bb, HIST_ROWS, pool_width), lambda bi, i: (bi, 0, 0)),
            pl.BlockSpec((bb, n_mem, xa_width), lambda bi, i: (bi, 0, 0)),
            pl.BlockSpec((bb, n_mem, xa_width), lambda bi, i: (bi, 0, 0)),
            pl.BlockSpec((pool_width, pool_width), lambda bi, i: (0, 0)),
            pl.BlockSpec((1, pool_width), lambda bi, i: (0, 0)),
            pl.BlockSpec((d, d), lambda bi, i: (0, 0)),
            pl.BlockSpec((1, d), lambda bi, i: (0, 0)),
        ],
        out_specs=pl.BlockSpec((bb, tt, d), lambda bi, i: (bi, i, 0)),
        out_shape=jax.ShapeDtypeStruct((b, t, d), F32),
        compiler_params=pltpu.CompilerParams(
            dimension_semantics=("arbitrary", "arbitrary"), vmem_limit_bytes=VMEM_LIMIT),
        name="epilogue",
    )(x, o_sb, rest, rest, hist0, mk, mv, pool_bd, pool_scale, w_out_bf16, g_final)


def _block_diag(pool_w):
    g, c, _ = pool_w.shape
    eye = jnp.eye(g, dtype=pool_w.dtype)
    return (eye[:, None, :, None] * pool_w[:, :, None, :]).reshape(g * c, g * c)


def _pad_hist(hist):
    return jnp.pad(hist, ((0, 0), (HIST_ROWS - hist.shape[1], 0), (0, 0)))


def _mixer_layer(x, start, k_past, v_past, pool_hist, mk, mv, g_norm, w_in_bf16, wkv_t, pool_bd,
                 pool_scale, w_out_bf16, g_final, *, dims, tiles, apply_final):
    sb_width, pool_width, xa_width, xa_heads = dims
    tm, tq, tk, tt = tiles
    b, t, d = x.shape
    heads = sb_width // HEAD_DIM
    k_new, v_new, qkv, rest = _project(x.reshape(b * t, d), g_norm, w_in_bf16, wkv_t,
                                       sb_width, tm, t)
    if k_new.ndim == 3:
        to_cache = lambda a: jnp.transpose(a.reshape(b, heads, HEAD_DIM, t), (0, 3, 1, 2))
    else:
        to_cache = lambda a: a.reshape(b, t, heads, HEAD_DIM)
    qkv = qkv.reshape(b, t, -1)
    rest = rest.reshape(b, t, -1)
    if k_past is None:
        o_sb = _stick_breaking_self(qkv, tq=tq, sb_width=sb_width,
                                    pairs_per_step=PAIRS_PER_STEP)
    else:
        o_sb = _stick_breaking_cached(qkv, k_past, v_past, tk=tk)
    y = _epilogue(x, o_sb, rest, _pad_hist(pool_hist), mk, mv, pool_bd, pool_scale,
                  w_out_bf16, g_final, tt=tt, start=start, sb_width=sb_width,
                  pool_width=pool_width, xa_width=xa_width, xa_heads=xa_heads,
                  apply_final=apply_final)
    u_pool = rest[:, :, sb_width:sb_width + pool_width]
    new_hist = jnp.concatenate([pool_hist, u_pool], axis=1)[:, -POOL_STATE:]
    return y, to_cache(k_new), to_cache(v_new), new_hist


def kernel(x_prompt, x_sample, cache_sb_k, cache_sb_v, state_pool, cache_mem_k, cache_mem_v,
           mem_prompt, g_norm, w_in, pool_w, pool_scale, g_mem, w_mem_kv, w_out, g_final):
    depth = g_norm.shape[0]
    bp, tp, d = x_prompt.shape
    bs, ts, _ = x_sample.shape
    past = cache_sb_k.shape[2]
    sb_heads, head_dim = cache_sb_k.shape[3], cache_sb_k.shape[4]
    assert head_dim == HEAD_DIM
    sb_width = sb_heads * head_dim
    pool_width = pool_w.shape[1] * pool_w.shape[2]
    n_mem, xa_heads, xa_dim = cache_mem_k.shape[2:]
    xa_width = xa_heads * xa_dim
    dims = (sb_width, pool_width, xa_width, xa_heads)
    g_fin = g_final.reshape(1, d)

    yp, ys = x_prompt, x_sample
    kp_l, vp_l, hp_l, mkp_l, mvp_l, ks_l, vs_l, hs_l = [], [], [], [], [], [], [], []
    for l in range(depth):
        final = l == depth - 1
        g_l = g_norm[l].reshape(1, d)
        w_in_b = w_in[l].astype(BF16)
        w_out_b = w_out[l].astype(BF16)
        pool_bd = _block_diag(pool_w[l]).astype(BF16)
        p_scale = pool_scale[l].reshape(1, pool_width)
        mk, mv = _memory_kv(mem_prompt.reshape(bp * n_mem, d), g_mem[l].reshape(1, d),
                            w_mem_kv[l].astype(BF16), tm=256)
        mk = mk.reshape(bp, n_mem, xa_width)
        mv = mv.reshape(bp, n_mem, xa_width)
        hist0 = jnp.zeros((bp, POOL_STATE, pool_width), x_prompt.dtype)
        wkv_t = jnp.transpose(w_in[l][:, sb_width:3 * sb_width]).astype(BF16)
        yp, kp, vp, hp = _mixer_layer(
            yp, 0, None, None, hist0, mk, mv, g_l, w_in_b, wkv_t, pool_bd, p_scale, w_out_b,
            g_fin, dims=dims, tiles=(512, 256, 256, 1024), apply_final=final)
        ys, kn, vn, hn = _mixer_layer(
            ys, past, jnp.transpose(cache_sb_k[l], (0, 2, 3, 1)),
            jnp.transpose(cache_sb_v[l], (0, 2, 3, 1)), state_pool[l],
            cache_mem_k[l].reshape(bs, n_mem, xa_width),
            cache_mem_v[l].reshape(bs, n_mem, xa_width),
            g_l, w_in_b, wkv_t, pool_bd, p_scale, w_out_b, g_fin,
            dims=dims, tiles=(512, ts, 256, ts), apply_final=final)
        kp_l.append(kp)
        vp_l.append(vp)
        hp_l.append(hp)
        mkp_l.append(mk.reshape(bp, n_mem, xa_heads, xa_dim))
        mvp_l.append(mv.reshape(bp, n_mem, xa_heads, xa_dim))
        ks_l.append(kn)
        vs_l.append(vn)
        hs_l.append(hn)
    stack = lambda xs: jnp.stack(xs, axis=0)
    return (yp, ys, stack(kp_l), stack(vp_l), stack(hp_l), stack(mkp_l), stack(mvp_l),
            stack(ks_l), stack(vs_l), stack(hs_l))
```

```python
import functools

import jax
import jax.numpy as jnp
from jax import lax
from jax.experimental import pallas as pl
from jax.experimental.pallas import tpu as pltpu

F32 = jnp.float32
BF16 = jnp.bfloat16

HEAD_DIM = 64
LANES = 128
POOL_WINDOWS = (2, 4, 8, 16)
POOL_STATE = max(POOL_WINDOWS) - 1
HIST_ROWS = 16
EPS = 1e-6
DEAD_CARRY = 104.0
VMEM_LIMIT = 56 * 1024 * 1024
NT_DIMS = (((1,), (1,)), ((), ()))
PAIRS_PER_STEP = 2
EPILOGUE_ROWS = 256


def _rms(x, g):
    ms = jnp.mean(x * x, axis=-1, keepdims=True)
    return (x * lax.rsqrt(ms + EPS)) * g


def _proj_kernel(x_ref, g_ref, w_ref, wkv_t_ref, k_ref, v_ref, qkv_ref, rest_ref,
                 *, sb_width, time_minor):
    h = _rms(x_ref[...], g_ref[...]).astype(BF16)
    s = sb_width
    q = jnp.dot(h, w_ref[:, 0:s], preferred_element_type=F32)
    qkv_ref[:, 0:s] = (q * (HEAD_DIM ** -0.5)).astype(BF16)
    if time_minor:
        kt = lax.dot_general(wkv_t_ref[0:s, :], h, NT_DIMS, preferred_element_type=F32)
        vt = lax.dot_general(wkv_t_ref[s:2 * s, :], h, NT_DIMS, preferred_element_type=F32)
        k_ref[0] = kt
        v_ref[0] = vt
        qkv_ref[:, s:2 * s] = kt.T.astype(BF16)
        qkv_ref[:, 2 * s:3 * s] = vt.T.astype(BF16)
    else:
        k = jnp.dot(h, w_ref[:, s:2 * s], preferred_element_type=F32)
        v = jnp.dot(h, w_ref[:, 2 * s:3 * s], preferred_element_type=F32)
        k_ref[...] = k
        v_ref[...] = v
        qkv_ref[:, s:2 * s] = k.astype(BF16)
        qkv_ref[:, 2 * s:3 * s] = v.astype(BF16)
    rest_ref[...] = jnp.dot(h, w_ref[:, 3 * s:], preferred_element_type=F32)


def _project(x2d, g, w_bf16, wkv_t, sb_width, tm, rows_per_stream):
    m, d = x2d.shape
    n = w_bf16.shape[1]
    n_rest = n - 3 * sb_width
    time_minor = rows_per_stream % tm == 0
    if time_minor:
        nt = rows_per_stream // tm
        kv_spec = pl.BlockSpec((1, sb_width, tm), lambda i: (i // nt, 0, i % nt))
        kv_shape = jax.ShapeDtypeStruct((m // rows_per_stream, sb_width, rows_per_stream), F32)
    else:
        kv_spec = pl.BlockSpec((tm, sb_width), lambda i: (i, 0))
        kv_shape = jax.ShapeDtypeStruct((m, sb_width), F32)
    return pl.pallas_call(
        functools.partial(_proj_kernel, sb_width=sb_width, time_minor=time_minor),
        grid=(m // tm,),
        in_specs=[
            pl.BlockSpec((tm, d), lambda i: (i, 0)),
            pl.BlockSpec((1, d), lambda i: (0, 0)),
            pl.BlockSpec((d, n), lambda i: (0, 0)),
            pl.BlockSpec((2 * sb_width, d), lambda i: (0, 0)),
        ],
        out_specs=[
            kv_spec,
            kv_spec,
            pl.BlockSpec((tm, 3 * sb_width), lambda i: (i, 0)),
            pl.BlockSpec((tm, n_rest), lambda i: (i, 0)),
        ],
        out_shape=[
            kv_shape,
            kv_shape,
            jax.ShapeDtypeStruct((m, 3 * sb_width), BF16),
            jax.ShapeDtypeStruct((m, n_rest), F32),
        ],
        compiler_params=pltpu.CompilerParams(
            dimension_semantics=("arbitrary",), vmem_limit_bytes=VMEM_LIMIT),
        name="proj",
    )(x2d, g, w_bf16, wkv_t)


def _memkv_kernel(x_ref, g_ref, w_ref, mk_ref, mv_ref, *, xa_width):
    h = _rms(x_ref[...], g_ref[...]).astype(BF16)
    kv = jnp.dot(h, w_ref[...], preferred_element_type=F32)
    mk_ref[...] = kv[:, :xa_width]
    mv_ref[...] = kv[:, xa_width:]


def _memory_kv(mem2d, g, w_bf16, tm):
    m, d = mem2d.shape
    xa_width = w_bf16.shape[1] // 2
    return pl.pallas_call(
        functools.partial(_memkv_kernel, xa_width=xa_width),
        grid=(m // tm,),
        in_specs=[
            pl.BlockSpec((tm, d), lambda i: (i, 0)),
            pl.BlockSpec((1, d), lambda i: (0, 0)),
            pl.BlockSpec((d, 2 * xa_width), lambda i: (0, 0)),
        ],
        out_specs=[
            pl.BlockSpec((tm, xa_width), lambda i: (i, 0)),
            pl.BlockSpec((tm, xa_width), lambda i: (i, 0)),
        ],
        out_shape=[
            jax.ShapeDtypeStruct((m, xa_width), F32),
            jax.ShapeDtypeStruct((m, xa_width), F32),
        ],
        compiler_params=pltpu.CompilerParams(
            dimension_semantics=("arbitrary",), vmem_limit_bytes=VMEM_LIMIT),
        name="memkv",
    )(mem2d, g, w_bf16)


def _strict_upper(n):
    r = lax.broadcasted_iota(jnp.int32, (n, n), 0)
    c = lax.broadcasted_iota(jnp.int32, (n, n), 1)
    return jnp.where(r > c, 1.0, 0.0).astype(BF16)


def _softplus(z):
    return jnp.maximum(z, 0.0) + jnp.log(1.0 + jnp.exp(-jnp.abs(z)))


def _sum_later(sp, upper):
    hi = sp.astype(BF16)
    lo = (sp - hi.astype(F32)).astype(BF16)
    return jnp.dot(jnp.concatenate([hi, lo], axis=1), jnp.concatenate([upper, upper], axis=0),
                   preferred_element_type=F32)


def _sb_block(qh, kblk, vblk, upper, carry, mask, transposed=False):
    if transposed:
        z = jnp.dot(qh, kblk, preferred_element_type=F32)
    else:
        z = lax.dot_general(qh, kblk, NT_DIMS, preferred_element_type=F32)
    sp = _softplus(z)
    if mask is not None:
        sp = jnp.where(mask, sp, 0.0)
    later = _sum_later(sp, upper)
    w = jnp.exp((z - sp) - later - carry)
    if mask is not None:
        w = jnp.where(mask, w, 0.0)
    if transposed:
        out = lax.dot_general(w.astype(BF16), vblk, NT_DIMS, preferred_element_type=F32)
    else:
        out = jnp.dot(w.astype(BF16), vblk, preferred_element_type=F32)
    return out, carry + jnp.sum(sp, axis=-1, keepdims=True)


def _any_live(carries):
    lowest = functools.reduce(jnp.minimum, carries)
    return (jnp.min(lowest) <= DEAD_CARRY).astype(jnp.int32)


def _walk_older(n_blocks, carries, block_fn):
    def test(state):
        return jnp.logical_and(state[0] < n_blocks, state[1] > 0)

    def body(state):
        new = block_fn(state[0], state[2:])
        return (state[0] + 1, _any_live(new), *new)

    lax.while_loop(test, body, (jnp.int32(0), _any_live(carries), *carries))


def _sb_window_kernel(q_ref, k_ref, v_ref, o_ref, acc_ref, *, tq, pairs):
    i = pl.program_id(2)
    lane = lax.broadcasted_iota(jnp.int32, (tq, LANES), 1)
    first = lane < HEAD_DIM
    qs, cols = [], []
    for p in range(pairs):
        col = slice(p * LANES, (p + 1) * LANES)
        q = q_ref[0, :, col]
        zero = jnp.zeros_like(q)
        qs += [jnp.where(first, q, zero), jnp.where(first, zero, q)]
        cols += [col, col]
    n = len(qs)
    r = lax.broadcasted_iota(jnp.int32, (tq, tq), 0)
    c = lax.broadcasted_iota(jnp.int32, (tq, tq), 1)
    causal = c < r
    upper = _strict_upper(tq)

    @pl.when(i == 0)
    def _():
        zc = jnp.zeros((tq, 1), F32)
        for h in range(n):
            o, _ = _sb_block(qs[h], k_ref[0, 0:tq, cols[h]], v_ref[0, 0:tq, cols[h]],
                             upper, zc, causal)
            acc_ref[h] = o

    @pl.when(i > 0)
    def _():
        ws = pl.multiple_of((i - 1) * tq, tq)
        zs = []
        for p in range(pairs):
            z2 = lax.dot_general(jnp.concatenate(qs[2 * p:2 * p + 2], axis=0),
                                 k_ref[0, pl.ds(ws, 2 * tq), cols[2 * p]], NT_DIMS,
                                 preferred_element_type=F32)
            zs += [z2[:tq], z2[tq:]]
        zp = [z[:, :tq] for z in zs]
        zd = [z[:, tq:] for z in zs]
        sp_p = [_softplus(z) for z in zp]
        sp_d = [jnp.where(causal, _softplus(z), 0.0) for z in zd]
        later = _sum_later(jnp.concatenate(sp_p + sp_d, axis=0), upper)
        carries, ws_bf16 = [], []
        for h in range(n):
            rs_d = jnp.sum(sp_d[h], axis=-1, keepdims=True)
            later_p = later[h * tq:(h + 1) * tq]
            later_d = later[(n + h) * tq:(n + h + 1) * tq]
            w_p = jnp.exp((zp[h] - sp_p[h]) - later_p - rs_d)
            w_d = jnp.where(causal, jnp.exp((zd[h] - sp_d[h]) - later_d), 0.0)
            ws_bf16.append(jnp.concatenate([w_p.astype(BF16), w_d.astype(BF16)], axis=1))
            carries.append(rs_d + jnp.sum(sp_p[h], axis=-1, keepdims=True))
        for p in range(pairs):
            o2 = jnp.dot(jnp.concatenate(ws_bf16[2 * p:2 * p + 2], axis=0),
                         v_ref[0, pl.ds(ws, 2 * tq), cols[2 * p]], preferred_element_type=F32)
            acc_ref[2 * p] = o2[:tq]
            acc_ref[2 * p + 1] = o2[tq:]

        n_older = i - 1

        def older_block(j, cs):
            start = pl.multiple_of((n_older - 1 - j) * tq, tq)
            new = []
            for h in range(n):
                o, ch = _sb_block(qs[h], k_ref[0, pl.ds(start, tq), cols[h]],
                                  v_ref[0, pl.ds(start, tq), cols[h]], upper, cs[h], None)
                acc_ref[h] += o
                new.append(ch)
            return new

        _walk_older(n_older, carries, older_block)

    for p in range(pairs):
        o_ref[0, :, p * LANES:(p + 1) * LANES] = jnp.where(
            first, acc_ref[2 * p], acc_ref[2 * p + 1])


def _stick_breaking_self(qkv, *, tq, sb_width, pairs_per_step):
    b, t, _ = qkv.shape
    width = pairs_per_step * LANES
    groups = sb_width // width
    return pl.pallas_call(
        functools.partial(_sb_window_kernel, tq=tq, pairs=pairs_per_step),
        grid=(b, groups, t // tq),
        in_specs=[
            pl.BlockSpec((1, tq, width), lambda bi, g, i: (bi, i, g)),
            pl.BlockSpec((1, t, width), lambda bi, g, i: (bi, 0, groups + g)),
            pl.BlockSpec((1, t, width), lambda bi, g, i: (bi, 0, 2 * groups + g)),
        ],
        out_specs=pl.BlockSpec((1, tq, width), lambda bi, g, i: (bi, i, g)),
        out_shape=jax.ShapeDtypeStruct((b, t, sb_width), F32),
        scratch_shapes=[pltpu.VMEM((2 * pairs_per_step, tq, LANES), F32)],
        compiler_params=pltpu.CompilerParams(
            dimension_semantics=("arbitrary", "arbitrary", "arbitrary"),
            vmem_limit_bytes=VMEM_LIMIT),
        name="stick_breaking_self",
    )(qkv, qkv, qkv)


def _sb_cached_kernel(qkv_ref, kwin_ref, vwin_ref, kc_ref, vc_ref, o_ref,
                      kbuf, vbuf, acc_ref, sem, *, heads, tk, n_older):
    b = pl.program_id(0)
    tq = qkv_ref.shape[1]
    s = heads * HEAD_DIM
    r = lax.broadcasted_iota(jnp.int32, (tq, tq), 0)
    c = lax.broadcasted_iota(jnp.int32, (tq, tq), 1)
    causal = c < r
    upper_d = _strict_upper(tq)
    upper = _strict_upper(tk)
    q = qkv_ref[0, :, 0:s]
    kn = qkv_ref[0, :, s:2 * s]
    vn = qkv_ref[0, :, 2 * s:3 * s]
    head = lambda x, h: x[:, h * HEAD_DIM:(h + 1) * HEAD_DIM]
    qs = [head(q, h) for h in range(heads)]

    zd = [lax.dot_general(qs[h], head(kn, h), NT_DIMS, preferred_element_type=F32)
          for h in range(heads)]
    zw = [jnp.dot(qs[h], kwin_ref[0, h].astype(BF16), preferred_element_type=F32)
          for h in range(heads)]
    sp_d = [jnp.where(causal, _softplus(z), 0.0) for z in zd]
    sp_w = [_softplus(z) for z in zw]
    later_d = _sum_later(jnp.concatenate(sp_d, axis=0), upper_d)
    later_w = _sum_later(jnp.concatenate(sp_w, axis=0), upper)
    carries = []
    for h in range(heads):
        rows = slice(h * tq, (h + 1) * tq)
        rs_d = jnp.sum(sp_d[h], axis=-1, keepdims=True)
        w_d = jnp.where(causal, jnp.exp((zd[h] - sp_d[h]) - later_d[rows]), 0.0)
        w_w = jnp.exp((zw[h] - sp_w[h]) - later_w[rows] - rs_d)
        acc_ref[h] = (jnp.dot(w_d.astype(BF16), head(vn, h), preferred_element_type=F32)
                      + lax.dot_general(w_w.astype(BF16), vwin_ref[0, h].astype(BF16), NT_DIMS,
                                        preferred_element_type=F32))
        carries.append(rs_d + jnp.sum(sp_w[h], axis=-1, keepdims=True))

    def block_copies(j):
        start = pl.multiple_of((n_older - 1 - j) * tk, tk)
        return (pltpu.make_async_copy(kc_ref.at[b, :, :, pl.ds(start, tk)], kbuf, sem.at[0]),
                pltpu.make_async_copy(vc_ref.at[b, :, :, pl.ds(start, tk)], vbuf, sem.at[1]))

    def older_block(j, cs):
        copies = block_copies(j)
        for cp in copies:
            cp.start()
        for cp in copies:
            cp.wait()
        new = []
        for h in range(heads):
            o, ch = _sb_block(qs[h], kbuf[h].astype(BF16), vbuf[h].astype(BF16),
                              upper, cs[h], None, transposed=True)
            acc_ref[h] += o
            new.append(ch)
        return new

    _walk_older(n_older, carries, older_block)
    o_ref[0] = jnp.concatenate([acc_ref[h] for h in range(heads)], axis=1)


def _stick_breaking_cached(qkv, cache_k, cache_v, *, tk):
    b, t, _ = qkv.shape
    _, heads, dh, p = cache_k.shape
    s = heads * dh
    last = p // tk - 1
    win = pl.BlockSpec((1, heads, dh, tk), lambda bi: (bi, 0, 0, last))
    return pl.pallas_call(
        functools.partial(_sb_cached_kernel, heads=heads, tk=tk, n_older=last),
        grid=(b,),
        in_specs=[
            pl.BlockSpec((1, t, 3 * s), lambda bi: (bi, 0, 0)),
            win, win,
            pl.BlockSpec(memory_space=pl.ANY),
            pl.BlockSpec(memory_space=pl.ANY),
        ],
        out_specs=pl.BlockSpec((1, t, s), lambda bi: (bi, 0, 0)),
        out_shape=jax.ShapeDtypeStruct((b, t, s), F32),
        scratch_shapes=[
            pltpu.VMEM((heads, dh, tk), F32),
            pltpu.VMEM((heads, dh, tk), F32),
            pltpu.VMEM((heads, t, dh), F32),
            pltpu.SemaphoreType.DMA((2,)),
        ],
        compiler_params=pltpu.CompilerParams(
            dimension_semantics=("arbitrary",), vmem_limit_bytes=VMEM_LIMIT),
        name="stick_breaking_cached",
    )(qkv, cache_k, cache_v, cache_k, cache_v)


def _silu(g):
    return g / (1.0 + jnp.exp(-g))


def _epilogue_kernel(x_ref, osb_ref, rest_ref, prev_ref, hist0_ref, mk_ref, mv_ref,
                     poolw_ref, pscale_ref, wout_ref, gfin_ref, y_ref,
                     *, tt, start, sb_width, pool_width, xa_width, xa_heads, apply_final):
    streams = x_ref.shape[0]
    mixed = [_mixed_rows(si, rest_ref, osb_ref, prev_ref, hist0_ref, mk_ref, mv_ref, poolw_ref,
                         pscale_ref, tt=tt, start=start, sb_width=sb_width,
                         pool_width=pool_width, xa_width=xa_width, xa_heads=xa_heads)
             for si in range(streams)]
    m_sb, m_pool, m_xa = (jnp.concatenate([m[k] for m in mixed], axis=0) for k in range(3))
    s, pw, xw = sb_width, pool_width, xa_width
    proj = (jnp.dot(m_sb, wout_ref[0:s, :], preferred_element_type=F32)
            + jnp.dot(m_pool, wout_ref[s:s + pw, :], preferred_element_type=F32)
            + jnp.dot(m_xa, wout_ref[s + pw:s + pw + xw, :], preferred_element_type=F32))
    for si in range(streams):
        y = x_ref[si] + proj[si * tt:(si + 1) * tt]
        if apply_final:
            y = _rms(y, gfin_ref[...])
        y_ref[si] = y


def _mixed_rows(si, rest_ref, osb_ref, prev_ref, hist0_ref, mk_ref, mv_ref, poolw_ref, pscale_ref,
                *, tt, start, sb_width, pool_width, xa_width, xa_heads):
    i = pl.program_id(1)
    s, pw, xw = sb_width, pool_width, xa_width
    g_sb = rest_ref[si, :, 0:s]
    u = rest_ref[si, :, s:s + pw]
    g_pool = rest_ref[si, :, s + pw:s + 2 * pw]
    q_xa = rest_ref[si, :, s + 2 * pw:s + 2 * pw + xw]
    g_xa = rest_ref[si, :, s + 2 * pw + xw:s + 2 * pw + 2 * xw]

    hist = jnp.where(i == 0, hist0_ref[si], prev_ref[si])
    ext = jnp.concatenate([hist, u], axis=0)
    sums = {1: ext}
    width = 1
    while width < max(POOL_WINDOWS):
        prev = sums[width]
        sums[2 * width] = prev + pltpu.roll(prev, width, 0)
        width *= 2
    lane = lax.broadcasted_iota(jnp.int32, (tt, pw), 1)
    group = lane // (pw // len(POOL_WINDOWS))
    pos = start + i * tt + lax.broadcasted_iota(jnp.int32, (tt, pw), 0)
    win_sum = jnp.zeros((tt, pw), F32)
    win = jnp.zeros((tt, pw), jnp.int32)
    for gi, w in enumerate(POOL_WINDOWS):
        sel = group == gi
        win_sum = jnp.where(sel, sums[w][HIST_ROWS:], win_sum)
        win = jnp.where(sel, w, win)
    cnt = jnp.minimum(pos + 1, win).astype(F32)
    pooled = win_sum / cnt - u
    o_pool = jnp.dot(pooled.astype(BF16), poolw_ref[...],
                     preferred_element_type=F32) * pscale_ref[...]

    mk = mk_ref[si].astype(BF16)
    mv = mv_ref[si].astype(BF16)
    xa_dim = xw // xa_heads
    lane_x = lax.broadcasted_iota(jnp.int32, (tt, xw), 1)
    head_x = lane_x // xa_dim
    qs = q_xa * (xa_dim ** -0.5)
    q_all = jnp.concatenate(
        [jnp.where(head_x == hh, qs, 0.0).astype(BF16) for hh in range(xa_heads)], axis=0)
    sc = lax.dot_general(q_all, mk, NT_DIMS, preferred_element_type=F32)
    e = jnp.exp(sc - jnp.max(sc, axis=-1, keepdims=True))
    p = e / jnp.sum(e, axis=-1, keepdims=True)
    o_all = jnp.dot(p.astype(BF16), mv, preferred_element_type=F32)
    o_xa = jnp.zeros((tt, xw), F32)
    for hh in range(xa_heads):
        o_xa = jnp.where(head_x == hh, o_all[hh * tt:(hh + 1) * tt], o_xa)

    return ((osb_ref[si] * _silu(g_sb)).astype(BF16),
            (o_pool * _silu(g_pool)).astype(BF16),
            (o_xa * _silu(g_xa)).astype(BF16))


def _epilogue(x, o_sb, rest, hist0, mk, mv, pool_bd, pool_scale, w_out_bf16, g_final,
              *, tt, start, sb_width, pool_width, xa_width, xa_heads, apply_final):
    b, t, d = x.shape
    bb = max(1, min(b, EPILOGUE_ROWS // tt))
    while b % bb:
        bb -= 1
    n_rest = rest.shape[-1]
    n_mem = mk.shape[1]
    hist_blocks = tt // HIST_ROWS
    u_col = sb_width // pool_width
    kern = functools.partial(
        _epilogue_kernel, tt=tt, start=start, sb_width=sb_width, pool_width=pool_width,
        xa_width=xa_width, xa_heads=xa_heads, apply_final=apply_final)
    return pl.pallas_call(
        kern,
        grid=(b // bb, t // tt),
        in_specs=[
            pl.BlockSpec((bb, tt, d), lambda bi, i: (bi, i, 0)),
            pl.BlockSpec((bb, tt, sb_width), lambda bi, i: (bi, i, 0)),
            pl.BlockSpec((bb, tt, n_rest), lambda bi, i: (bi, i, 0)),
            pl.BlockSpec((bb, HIST_ROWS, pool_width),
                         lambda bi, i: (bi, jnp.maximum(i * hist_blocks - 1, 0), u_col)),
            pl.BlockSpec((bb, HIST_ROWS, pool_width), lambda bi, i: (bi, 0, 0)),
            pl.BlockSpec((bb, n_mem, xa_width), lambda bi, i: (bi, 0, 0)),
            pl.BlockSpec((bb, n_mem, xa_width), lambda bi, i: (bi, 0, 0)),
            pl.BlockSpec((pool_width, pool_width), lambda bi, i: (0, 0)),
            pl.BlockSpec((1, pool_width), lambda bi, i: (0, 0)),
            pl.BlockSpec((d, d), lambda bi, i: (0, 0)),
            pl.BlockSpec((1, d), lambda bi, i: (0, 0)),
        ],
        out_specs=pl.BlockSpec((bb, tt, d), lambda bi, i: (bi, i, 0)),
        out_shape=jax.ShapeDtypeStruct((b, t, d), F32),
        compiler_params=pltpu.CompilerParams(
            dimension_semantics=("arbitrary", "arbitrary"), vmem_limit_bytes=VMEM_LIMIT),
        name="epilogue",
    )(x, o_sb, rest, rest, hist0, mk, mv, pool_bd, pool_scale, w_out_bf16, g_final)


def _block_diag(pool_w):
    g, c, _ = pool_w.shape
    eye = jnp.eye(g, dtype=pool_w.dtype)
    return (eye[:, None, :, None] * pool_w[:, :, None, :]).reshape(g * c, g * c)


def _pad_hist(hist):
    return jnp.pad(hist, ((0, 0), (HIST_ROWS - hist.shape[1], 0), (0, 0)))


def _mixer_layer(x, start, k_past, v_past, pool_hist, mk, mv, g_norm, w_in_bf16, wkv_t, pool_bd,
                 pool_scale, w_out_bf16, g_final, *, dims, tiles, apply_final):
    sb_width, pool_width, xa_width, xa_heads = dims
    tm, tq, tk, tt = tiles
    b, t, d = x.shape
    heads = sb_width // HEAD_DIM
    k_new, v_new, qkv, rest = _project(x.reshape(b * t, d), g_norm, w_in_bf16, wkv_t,
                                       sb_width, tm, t)
    if k_new.ndim == 3:
        to_cache = lambda a: jnp.transpose(a.reshape(b, heads, HEAD_DIM, t), (0, 3, 1, 2))
    else:
        to_cache = lambda a: a.reshape(b, t, heads, HEAD_DIM)
    qkv = qkv.reshape(b, t, -1)
    rest = rest.reshape(b, t, -1)
    if k_past is None:
        o_sb = _stick_breaking_self(qkv, tq=tq, sb_width=sb_width,
                                    pairs_per_step=PAIRS_PER_STEP)
    else:
        o_sb = _stick_breaking_cached(qkv, k_past, v_past, tk=tk)
    y = _epilogue(x, o_sb, rest, _pad_hist(pool_hist), mk, mv, pool_bd, pool_scale,
                  w_out_bf16, g_final, tt=tt, start=start, sb_width=sb_width,
                  pool_width=pool_width, xa_width=xa_width, xa_heads=xa_heads,
                  apply_final=apply_final)
    u_pool = rest[:, :, sb_width:sb_width + pool_width]
    new_hist = jnp.concatenate([pool_hist, u_pool], axis=1)[:, -POOL_STATE:]
    return y, to_cache(k_new), to_cache(v_new), new_hist


def kernel(x_prompt, x_sample, cache_sb_k, cache_sb_v, state_pool, cache_mem_k, cache_mem_v,
           mem_prompt, g_norm, w_in, pool_w, pool_scale, g_mem, w_mem_kv, w_out, g_final):
    depth = g_norm.shape[0]
    bp, tp, d = x_prompt.shape
    bs, ts, _ = x_sample.shape
    past = cache_sb_k.shape[2]
    sb_heads, head_dim = cache_sb_k.shape[3], cache_sb_k.shape[4]
    assert head_dim == HEAD_DIM
    sb_width = sb_heads * head_dim
    pool_width = pool_w.shape[1] * pool_w.shape[2]
    n_mem, xa_heads, xa_dim = cache_mem_k.shape[2:]
    xa_width = xa_heads * xa_dim
    dims = (sb_width, pool_width, xa_width, xa_heads)
    g_fin = g_final.reshape(1, d)

    yp, ys = x_prompt, x_sample
    kp_l, vp_l, hp_l, mkp_l, mvp_l, ks_l, vs_l, hs_l = [], [], [], [], [], [], [], []
    for l in range(depth):
        final = l == depth - 1
        g_l = g_norm[l].reshape(1, d)
        w_in_b = w_in[l].astype(BF16)
        w_out_b = w_out[l].astype(BF16)
        pool_bd = _block_diag(pool_w[l]).astype(BF16)
        p_scale = pool_scale[l].reshape(1, pool_width)
        mk, mv = _memory_kv(mem_prompt.reshape(bp * n_mem, d), g_mem[l].reshape(1, d),
                            w_mem_kv[l].astype(BF16), tm=256)
        mk = mk.reshape(bp, n_mem, xa_width)
        mv = mv.reshape(bp, n_mem, xa_width)
        hist0 = jnp.zeros((bp, POOL_STATE, pool_width), x_prompt.dtype)
        wkv_t = jnp.transpose(w_in[l][:, sb_width:3 * sb_width]).astype(BF16)
        yp, kp, vp, hp = _mixer_layer(
            yp, 0, None, None, hist0, mk, mv, g_l, w_in_b, wkv_t, pool_bd, p_scale, w_out_b,
            g_fin, dims=dims, tiles=(512, 256, 256, 1024), apply_final=final)
        ys, kn, vn, hn = _mixer_layer(
            ys, past, jnp.transpose(cache_sb_k[l], (0, 2, 3, 1)),
            jnp.transpose(cache_sb_v[l], (0, 2, 3, 1)), state_pool[l],
            cache_mem_k[l].reshape(bs, n_mem, xa_width),
            cache_mem_v[l].reshape(bs, n_mem, xa_width),
            g_l, w_in_b, wkv_t, pool_bd, p_scale, w_out_b, g_fin,
            dims=dims, tiles=(512, ts, 256, ts), apply_final=final)
        kp_l.append(kp)
        vp_l.append(vp)
        hp_l.append(hp)
        mkp_l.append(mk.reshape(bp, n_mem, xa_heads, xa_dim))
        mvp_l.append(mv.reshape(bp, n_mem, xa_heads, xa_dim))
        ks_l.append(kn)
        vs_l.append(vn)
        hs_l.append(hn)
    stack = lambda xs: jnp.stack(xs, axis=0)
    return (yp, ys, stack(kp_l), stack(vp_l), stack(hp_l), stack(mkp_l), stack(mvp_l),
            stack(ks_l), stack(vs_l), stack(hs_l))
```
